```python
import jax, jax.numpy as jnp
from jax import lax
import numpy as np

D_MODEL = 1024
BATCH = 8
SEQ = 4096
DEPTH = 1

N_HEADS_A = 8
HEAD_DIM_A = 64
N_IDX_HEADS = 8
IDX_DIM = 64
TOPK_MAX = 256
Q_BLOCK = 128
N_HEADS_B = 4
HEAD_K_B = 128
HEAD_V_B = 128
CHUNK_B = 64
N_HEADS_C = 4
HEAD_DIM_C = 128
MEM_LEN = 256
N_BRANCH = 3
EPS = 1e-6

WIDTH_A = N_HEADS_A * HEAD_DIM_A
WIDTH_BK = N_HEADS_B * HEAD_K_B
WIDTH_BV = N_HEADS_B * HEAD_V_B
WIDTH_C = N_HEADS_C * HEAD_DIM_C
BRANCH_WIDTH = WIDTH_A
IDX_SCALE = (IDX_DIM ** -0.5) * (N_IDX_HEADS ** -0.5)
IN_SPLITS = (WIDTH_A, WIDTH_A, WIDTH_A, WIDTH_A,
             N_IDX_HEADS * IDX_DIM, IDX_DIM, N_IDX_HEADS,
             WIDTH_BK, WIDTH_BK, WIDTH_BV, WIDTH_BV,
             WIDTH_C, WIDTH_C,
             N_BRANCH * D_MODEL)
N_IN = sum(IN_SPLITS)

kernel_name = "hybrid_dsa_hgrn2_memxattn_gated"


def rms_norm(x, g):
    xf = x.astype(jnp.float32)
    y = xf * lax.rsqrt(jnp.mean(xf * xf, axis=-1, keepdims=True) + EPS)
    return (y * g.astype(jnp.float32)).astype(x.dtype)


def alibi_slopes(n):
    return jnp.exp2(-8.0 * (jnp.arange(n, dtype=jnp.float32) + 1.0) / n)


def dsa_attention(q, k, v, qi, ki, wi):
    B, S = q.shape[0], q.shape[1]
    topk = min(TOPK_MAX, S // 4)
    nb = S // Q_BLOCK
    slopes = alibi_slopes(N_HEADS_A)
    neg = jnp.finfo(jnp.float32).min
    kif = ki.astype(jnp.float32)
    spos = jnp.arange(S, dtype=jnp.int32)

    def to_blocks(a):
        return jnp.swapaxes(a.reshape((B, nb, Q_BLOCK) + a.shape[2:]), 0, 1)

    def block(args):
        qb, qib, wib, start = args
        tpos = start + jnp.arange(Q_BLOCK, dtype=jnp.int32)
        rel = jnp.einsum('bthd,bsd->bths', qib.astype(jnp.float32), kif)
        score = jnp.einsum('bths,bth->bts', jax.nn.relu(rel), wib.astype(jnp.float32)) * IDX_SCALE
        causal = spos[None, :] <= tpos[:, None]
        score = jnp.where(causal[None], score, -jnp.inf)
        _, idx = lax.top_k(score, topk)
        kg = jax.vmap(lambda kk, ii: kk[ii])(k, idx)
        vg = jax.vmap(lambda vv, ii: vv[ii])(v, idx)
        logits = jnp.einsum('bthd,btkhd->bhtk', qb, kg).astype(jnp.float32) * (HEAD_DIM_A ** -0.5)
        dist = (tpos[None, :, None] - idx).astype(jnp.float32)
        logits = logits - slopes[None, :, None, None] * dist[:, None]
        valid = idx <= tpos[None, :, None]
        logits = jnp.where(valid[:, None], logits, neg)
        p = jax.nn.softmax(logits, axis=-1).astype(v.dtype)
        return jnp.einsum('bhtk,btkhd->bthd', p, vg)

    starts = jnp.arange(nb, dtype=jnp.int32) * Q_BLOCK
    o = lax.map(block, (to_blocks(q), to_blocks(qi), to_blocks(wi), starts))
    return jnp.swapaxes(o, 0, 1).reshape(B, S, N_HEADS_A, HEAD_DIM_A)


def hgrn2_recurrence(q, f_pre, i, lb):
    B, S = q.shape[0], q.shape[1]
    n = S // CHUNK_B
    f = lb + (1.0 - lb) * jax.nn.sigmoid(f_pre.astype(jnp.float32))
    logf = jnp.log(f)
    kk = 1.0 - f

    def chunks(a, d):
        return a.astype(jnp.float32).reshape(B, n, CHUNK_B, N_HEADS_B, d).transpose(1, 0, 3, 2, 4)

    qs, ks, gs = chunks(q, HEAD_K_B), chunks(kk, HEAD_K_B), chunks(logf, HEAD_K_B)
    vs = chunks(i, HEAD_V_B)
    tril = jnp.tril(jnp.ones((CHUNK_B, CHUNK_B), dtype=bool))

    def step(state, inp):
        qc, kc, vc, gc = inp
        b = jnp.cumsum(gc, axis=-2)
        diff = b[..., :, None, :] - b[..., None, :, :]
        decay = jnp.exp(jnp.where(tril[:, :, None], diff, -jnp.inf))
        scores = jnp.einsum('bhtsd,bhsd->bhts', qc[..., :, None, :] * decay, kc)
        o = (jnp.einsum('bhts,bhsv->bhtv', scores, vc)
             + jnp.einsum('bhtd,bhdv->bhtv', qc * jnp.exp(b), state))
        b_last = b[..., -1:, :]
        state = (jnp.exp(b_last[..., 0, :])[..., None] * state
                 + jnp.einsum('bhsd,bhsv->bhdv', kc * jnp.exp(b_last - b), vc))
        return state, o

    s0 = jnp.zeros((B, N_HEADS_B, HEAD_K_B, HEAD_V_B), jnp.float32)
    _, o = lax.scan(step, s0, (qs, ks, vs, gs))
    return o.transpose(1, 0, 3, 2, 4).reshape(B, S, N_HEADS_B, HEAD_V_B).astype(i.dtype)


def memory_attention(q, k, v):
    logits = jnp.einsum('bthd,bmhd->bhtm', q, k).astype(jnp.float32) * (HEAD_DIM_C ** -0.5)
    p = jax.nn.softmax(logits, axis=-1).astype(v.dtype)
    return jnp.einsum('bhtm,bmhd->bthd', p, v)


def hybrid_layer(x, mem, norm_in, norm_mem, w_in, q_norm_a, k_norm_a, lb,
                 o_norm_b, w_mem_kv, q_norm_c, k_norm_c, w_branch, w_out):
    B, S, _ = x.shape
    M = mem.shape[1]
    h = rms_norm(x, norm_in)
    proj = h @ w_in
    offsets = np.cumsum(IN_SPLITS)[:-1].tolist()
    (qa, ka, va, za, qi, ki, wi, qb, fb, ib, gb, qc, zc, gates) = jnp.split(proj, offsets, axis=-1)

    qa = rms_norm(qa.reshape(B, S, N_HEADS_A, HEAD_DIM_A), q_norm_a)
    ka = rms_norm(ka.reshape(B, S, N_HEADS_A, HEAD_DIM_A), k_norm_a)
    va = va.reshape(B, S, N_HEADS_A, HEAD_DIM_A)
    oa = dsa_attention(qa, ka, va, qi.reshape(B, S, N_IDX_HEADS, IDX_DIM), ki, wi)
    oa = oa.reshape(B, S, WIDTH_A) * jax.nn.silu(za)

    ob = hgrn2_recurrence(jax.nn.silu(qb), fb, ib, lb)
    ob = rms_norm(ob, o_norm_b).reshape(B, S, WIDTH_BV) * jax.nn.silu(gb)

    mkv = rms_norm(mem, norm_mem) @ w_mem_kv
    kc, vc = jnp.split(mkv, 2, axis=-1)
    kc = rms_norm(kc.reshape(B, M, N_HEADS_C, HEAD_DIM_C), k_norm_c)
    vc = vc.reshape(B, M, N_HEADS_C, HEAD_DIM_C)
    qc = rms_norm(qc.reshape(B, S, N_HEADS_C, HEAD_DIM_C), q_norm_c)
    oc = memory_attention(qc, kc, vc).reshape(B, S, WIDTH_C) * jax.nn.silu(zc)

    branches = jnp.stack([oa, ob, oc], axis=2)
    ys = jnp.einsum('bsnc,ncd->bsnd', branches, w_branch)
    g = jax.nn.sigmoid(gates.reshape(B, S, N_BRANCH, D_MODEL))
    merged = jnp.einsum('bsnd,bsnd->bsd', g, ys)
    return x + merged @ w_out


def setup_inputs(seed: int = 0) -> dict:
    key = jax.random.key(seed)
    ks = jax.random.split(key, 14)
    f32 = jnp.float32

    def gain(k, shape):
        return 1.0 + 0.1 * jax.random.normal(k, shape, f32)

    return {
        "x": jax.random.normal(ks[0], (BATCH, SEQ, D_MODEL), f32),
        "mem": jax.random.normal(ks[1], (BATCH, MEM_LEN, D_MODEL), f32),
        "norm_in": gain(ks[2], (DEPTH, D_MODEL)),
        "norm_mem": gain(ks[3], (DEPTH, D_MODEL)),
        "w_in": jax.random.normal(ks[4], (DEPTH, D_MODEL, N_IN), f32) * D_MODEL ** -0.5,
        "q_norm_a": gain(ks[5], (DEPTH, HEAD_DIM_A)),
        "k_norm_a": gain(ks[6], (DEPTH, HEAD_DIM_A)),
        "lower_bounds": 0.1 * jax.random.normal(ks[7], (DEPTH + 1, WIDTH_BK), f32),
        "o_norm_b": gain(ks[8], (DEPTH, HEAD_V_B)),
        "w_mem_kv": jax.random.normal(ks[9], (DEPTH, D_MODEL, 2 * WIDTH_C), f32) * D_MODEL ** -0.5,
        "q_norm_c": gain(ks[10], (DEPTH, HEAD_DIM_C)),
        "k_norm_c": gain(ks[11], (DEPTH, HEAD_DIM_C)),
        "w_branch": jax.random.normal(ks[12], (DEPTH, N_BRANCH, BRANCH_WIDTH, D_MODEL), f32) * BRANCH_WIDTH ** -0.5,
        "w_out": jax.random.normal(ks[13], (DEPTH, D_MODEL, D_MODEL), f32) * D_MODEL ** -0.5,
    }


def reference(x, mem, norm_in, norm_mem, w_in, q_norm_a, k_norm_a, lower_bounds,
              o_norm_b, w_mem_kv, q_norm_c, k_norm_c, w_branch, w_out):
    lb_all = jnp.cumsum(jax.nn.softmax(lower_bounds.astype(jnp.float32), axis=0), axis=0)
    for layer in range(DEPTH):
        x = hybrid_layer(x, mem, norm_in[layer], norm_mem[layer], w_in[layer],
                         q_norm_a[layer], k_norm_a[layer], lb_all[layer],
                         o_norm_b[layer], w_mem_kv[layer], q_norm_c[layer], k_norm_c[layer],
                         w_branch[layer], w_out[layer])
    return x
```

```python
import functools

import jax
import jax.numpy as jnp
import numpy as np
from jax import lax
from jax.experimental import pallas as pl
from jax.experimental.pallas import tpu as pltpu

F32 = jnp.float32
BF16 = jnp.bfloat16
I32 = jnp.int32

N_HEADS_A = 8
HEAD_DIM_A = 64
N_IDX_HEADS = 8
IDX_DIM = 64
TOPK_MAX = 256
N_HEADS_B = 4
HEAD_B = 128
N_HEADS_C = 4
HEAD_DIM_C = 128
N_BRANCH = 3
EPS = 1e-6
WIDTH = 512
IDX_SCALE = (IDX_DIM ** -0.5) * (N_IDX_HEADS ** -0.5)

LANES = 128
VMEM_LIMIT = 56 * 1024 * 1024

TM_PROJ = 256
TQ = 128
TK = 256
CHUNK = 64
SUB = 16
T_HGRN = 128
TM_MERGE = 256

NEG_BIG = -1e30
INT_MIN = -2147483648

NT = (((1,), (1,)), ((), ()))


def _dot(a, b):
    return jnp.dot(a, b, preferred_element_type=F32)


def _dot_nt(a, b):
    return lax.dot_general(a, b, NT, preferred_element_type=F32)


def _group_meansq(v, ones_ref, group):
    sq = v * v
    hi = sq.astype(BF16)
    lo = (sq - hi.astype(F32)).astype(BF16)
    s = _dot(hi, ones_ref[...]) + _dot(lo, ones_ref[...])
    return s * (1.0 / group)


_C_QA, _C_KA, _C_ZA, _C_QB, _C_FB, _C_IB, _C_GB, _C_QC, _C_ZC = [WIDTH * i for i in range(9)]
_C_GATES = 9 * WIDTH
_C_QI = _C_GATES + 3 * 1024
_C_KI = _C_QI + WIDTH
_C_END = _C_KI + LANES


def _proj_body(x_ref, nin_ref, w_ref, wvt_ref, wwt_ref, lbp_ref, gqa_ref, gka_ref, gqc_ref, ones64_ref, ones128_ref,
               qa_ref, ka_ref, vat_ref, sza_ref, qi_ref, ki2_ref, wit_ref,
               sqb_ref, logf_ref, kk_ref, ib_ref, ibt_ref, sgb_ref, qc_ref, szc_ref, gates_ref, *, d_model):
    x = x_ref[...]
    ms = jnp.mean(x * x, axis=-1, keepdims=True)
    h = (x * lax.rsqrt(ms + EPS)) * nin_ref[...]
    hb = h.astype(BF16)

    def proj(c0, width):
        return _dot(hb, w_ref[:, c0:c0 + width])

    qa = proj(_C_QA, WIDTH)
    qa = qa * lax.rsqrt(_group_meansq(qa, ones64_ref, HEAD_DIM_A) + EPS) * gqa_ref[...]
    qa_ref[...] = (qa * (HEAD_DIM_A ** -0.5)).astype(BF16)
    ka = proj(_C_KA, WIDTH)
    ka = ka * lax.rsqrt(_group_meansq(ka, ones64_ref, HEAD_DIM_A) + EPS) * gka_ref[...]
    ka_ref[...] = ka.astype(BF16)
    vat_ref[0] = _dot_nt(wvt_ref[0:WIDTH, :], hb).astype(BF16)
    sza_ref[...] = jax.nn.silu(proj(_C_ZA, WIDTH)).astype(BF16)
    qi_ref[...] = proj(_C_QI, WIDTH).astype(BF16)
    ki2_ref[...] = proj(_C_KI, LANES).astype(BF16)
    wit_ref[...] = _dot_nt(wwt_ref[...], hb)
    sqb_ref[...] = jax.nn.silu(proj(_C_QB, WIDTH)).astype(BF16)
    lbp = lbp_ref[...]
    lbe = jnp.exp(lbp - jnp.max(lbp, axis=0, keepdims=True))
    lb = lbe[0:1, :] / jnp.sum(lbe, axis=0, keepdims=True)
    f = lb + (1.0 - lb) * jax.nn.sigmoid(proj(_C_FB, WIDTH))
    logf_ref[...] = jnp.log(f)
    kk_ref[...] = (1.0 - f).astype(BF16)
    ib_ref[...] = proj(_C_IB, WIDTH).astype(BF16)
    ibt_ref[0] = _dot_nt(wvt_ref[WIDTH:2 * WIDTH, :], hb).astype(BF16)
    sgb_ref[...] = jax.nn.silu(proj(_C_GB, WIDTH)).astype(BF16)
    qc = proj(_C_QC, WIDTH)
    qc = qc * lax.rsqrt(_group_meansq(qc, ones128_ref, HEAD_DIM_C) + EPS) * gqc_ref[...]
    qc_ref[...] = (qc * (HEAD_DIM_C ** -0.5)).astype(BF16)
    szc_ref[...] = jax.nn.silu(proj(_C_ZC, WIDTH)).astype(BF16)
    for n in range(N_BRANCH):
        gates_ref[:, n * d_model:(n + 1) * d_model] = jax.nn.sigmoid(
            proj(_C_GATES + n * d_model, d_model)).astype(BF16)


def _const_spec(shape):
    nd = len(shape)
    return pl.BlockSpec(shape, lambda *_: (0,) * nd, pipeline_mode=pl.Buffered(1))


def _block_ones(width, group):
    g = np.arange(width) // group
    return jnp.asarray((g[:, None] == g[None, :]).astype(np.float32), dtype=BF16)


def _proj_call(x2, norm_in, w_in, lower_bounds, q_norm_a, k_norm_a, q_norm_c):
    rows, d_model = x2.shape
    tm = TM_PROJ
    assert rows % tm == 0 and d_model == 1024
    offs = np.cumsum([0, 512, 512, 512, 512, 512, 64, 8, 512, 512, 512, 512, 512, 512, 3 * d_model])
    (o_qa, o_ka, o_va, o_za, o_qi, o_ki, o_wi, o_qb, o_fb, o_ib, o_gb, o_qc, o_zc, o_g, o_end) = [int(o) for o in offs]
    assert o_end == w_in.shape[1]
    wb = w_in.astype(BF16)
    col = lambda a, b: wb[:, a:b]
    w_main = jnp.concatenate([
        col(o_qa, o_ka), col(o_ka, o_va), col(o_za, o_qi), col(o_qb, o_fb), col(o_fb, o_ib), col(o_ib, o_gb),
        col(o_gb, o_qc), col(o_qc, o_zc), col(o_zc, o_g), col(o_g, o_end), col(o_qi, o_ki),
        col(o_ki, o_wi), col(o_ki, o_wi)], axis=1)
    assert w_main.shape[1] == _C_END
    wvt = jnp.concatenate([col(o_va, o_za).T, col(o_ib, o_gb).T], axis=0)
    wwt = jnp.concatenate([col(o_wi, o_qb).T, jnp.zeros((8, d_model), BF16)], axis=0)
    tile = lambda g, reps: jnp.tile(g.astype(F32), reps)[None, :]
    n_slots = lower_bounds.shape[0]

    row_spec = lambda w: pl.BlockSpec((tm, w), lambda i: (i, 0))
    t_spec = pl.BlockSpec((1, WIDTH, tm), lambda i: (i, 0, 0))
    out_shape = [
        jax.ShapeDtypeStruct((rows, WIDTH), BF16),
        jax.ShapeDtypeStruct((rows, WIDTH), BF16),
        jax.ShapeDtypeStruct((rows // tm, WIDTH, tm), BF16),
        jax.ShapeDtypeStruct((rows, WIDTH), BF16),
        jax.ShapeDtypeStruct((rows, WIDTH), BF16),
        jax.ShapeDtypeStruct((rows, LANES), BF16),
        jax.ShapeDtypeStruct((16, rows), F32),
        jax.ShapeDtypeStruct((rows, WIDTH), BF16),
        jax.ShapeDtypeStruct((rows, WIDTH), F32),
        jax.ShapeDtypeStruct((rows, WIDTH), BF16),
        jax.ShapeDtypeStruct((rows, WIDTH), BF16),
        jax.ShapeDtypeStruct((rows // tm, WIDTH, tm), BF16),
        jax.ShapeDtypeStruct((rows, WIDTH), BF16),
        jax.ShapeDtypeStruct((rows, WIDTH), BF16),
        jax.ShapeDtypeStruct((rows, WIDTH), BF16),
        jax.ShapeDtypeStruct((rows, N_BRANCH * d_model), BF16),
    ]
    out_specs = [row_spec(WIDTH), row_spec(WIDTH), t_spec, row_spec(WIDTH), row_spec(WIDTH), row_spec(LANES),
                 pl.BlockSpec((16, tm), lambda i: (0, i)),
                 row_spec(WIDTH), row_spec(WIDTH), row_spec(WIDTH), row_spec(WIDTH), t_spec, row_spec(WIDTH),
                 row_spec(WIDTH), row_spec(WIDTH), row_spec(N_BRANCH * d_model)]
    in_specs = [row_spec(d_model), _const_spec((1, d_model)), _const_spec(w_main.shape), _const_spec(wvt.shape),
                _const_spec(wwt.shape), _const_spec((n_slots, WIDTH)), _const_spec((1, WIDTH)),
                _const_spec((1, WIDTH)), _const_spec((1, WIDTH)), _const_spec((WIDTH, WIDTH)),
                _const_spec((WIDTH, WIDTH))]
    return pl.pallas_call(
        functools.partial(_proj_body, d_model=d_model),
        out_shape=out_shape, grid=(rows // tm,), in_specs=in_specs, out_specs=out_specs,
        compiler_params=pltpu.CompilerParams(dimension_semantics=("parallel",), vmem_limit_bytes=VMEM_LIMIT),
        name="proj",
    )(x2, norm_in.astype(F32)[None, :], w_main, wvt, wwt, lower_bounds.astype(F32),
      tile(q_norm_a, N_HEADS_A), tile(k_norm_a, N_HEADS_A), tile(q_norm_c, N_HEADS_C),
      _block_ones(WIDTH, HEAD_DIM_A), _block_ones(WIDTH, HEAD_DIM_C))


def _memkv_body(m_ref, nm_ref, w_ref, gk_ref, ones128_ref, kc_ref, vc_ref):
    x = m_ref[...]
    ms = jnp.mean(x * x, axis=-1, keepdims=True)
    hb = ((x * lax.rsqrt(ms + EPS)) * nm_ref[...]).astype(BF16)
    kc = _dot(hb, w_ref[:, 0:WIDTH])
    kc = kc * lax.rsqrt(_group_meansq(kc, ones128_ref, HEAD_DIM_C) + EPS) * gk_ref[...]
    kc_ref[...] = kc.astype(BF16)
    vc_ref[...] = _dot(hb, w_ref[:, WIDTH:2 * WIDTH]).astype(BF16)


def _memkv_call(mem2, norm_mem, w_mem_kv, k_norm_c, tm):
    rows, d_model = mem2.shape
    row_spec = lambda w: pl.BlockSpec((tm, w), lambda i: (i, 0))
    return pl.pallas_call(
        _memkv_body,
        out_shape=[jax.ShapeDtypeStruct((rows, WIDTH), BF16)] * 2,
        grid=(rows // tm,),
        in_specs=[row_spec(d_model), _const_spec((1, d_model)), _const_spec((d_model, 2 * WIDTH)),
                  _const_spec((1, WIDTH)), _const_spec((WIDTH, WIDTH))],
        out_specs=[row_spec(WIDTH), row_spec(WIDTH)],
        compiler_params=pltpu.CompilerParams(dimension_semantics=("parallel",), vmem_limit_bytes=VMEM_LIMIT),
        name="memkv",
    )(mem2, norm_mem.astype(F32)[None, :], w_mem_kv.astype(BF16),
      jnp.tile(k_norm_c.astype(F32), N_HEADS_C)[None, :], _block_ones(WIDTH, HEAD_DIM_C))


def _dsa_body(qa_ref, qi_ref, wit_ref, sza_ref, ka_ref, vat_ref, ki2_ref, out_ref,
              keys_ref, bias_ref, qih_ref, qah_ref, ot_ref, *, topk, pos_bits):
    i = pl.program_id(1)
    q0 = i * TQ
    nkt = (q0 + TQ + TK - 1) // TK

    lane_lo = lax.broadcasted_iota(I32, (TQ, LANES), 1) < HEAD_DIM_A
    for h in range(N_HEADS_A):
        p = h // 2
        keep = lane_lo if h % 2 == 0 else jnp.logical_not(lane_lo)
        qih_ref[h] = jnp.where(keep, qi_ref[:, p * LANES:(p + 1) * LANES], jnp.zeros((), BF16))
        qah_ref[h] = jnp.where(keep, qa_ref[:, p * LANES:(p + 1) * LANES], jnp.zeros((), BF16))

    row_iota = lax.broadcasted_iota(I32, (TK, TQ), 0)
    tpos = q0 + lax.broadcasted_iota(I32, (TK, TQ), 1)

    def score_tile(j, carry):
        kt = ki2_ref[pl.ds(pl.multiple_of(j * TK, TK), TK), :]
        acc = jnp.zeros((TK, TQ), F32)
        for h in range(N_IDX_HEADS):
            rel = _dot_nt(kt, qih_ref[h])
            acc = acc + jnp.maximum(rel, 0.0) * wit_ref[h:h + 1, :]
        sc = acc * IDX_SCALE
        sc = jnp.where(sc == 0.0, 0.0, sc)
        bits = pltpu.bitcast(sc, I32)
        key = jnp.where(bits < 0, bits ^ jnp.int32(0x7FFFFFFF), bits)
        keys_ref[j] = jnp.where(j * TK + row_iota <= tpos, key, jnp.int32(INT_MIN))
        return carry

    lax.fori_loop(0, nkt, score_tile, 0)

    def count(pred):
        def tile(j, acc):
            c = jnp.where(pred(keys_ref[j], j), jnp.int32(1), jnp.int32(0))
            return acc + jnp.sum(c.reshape(TK // 8, 8, TQ), axis=0)
        acc = lax.fori_loop(0, nkt, tile, jnp.zeros((8, TQ), I32))
        return jnp.sum(acc, axis=0, keepdims=True)

    def bisect(it, p):
        cand = p ^ lax.shift_left(jnp.int32(1), 31 - it)
        n = count(lambda k, j: k >= cand)
        return jnp.where(n >= topk, cand, p)

    kth = lax.fori_loop(0, 32, bisect, jnp.full((1, TQ), INT_MIN, I32))
    thr = jnp.maximum(kth, jnp.int32(INT_MIN + 1))
    n_gt = count(lambda k, j: k > kth)
    n_ge = count(lambda k, j: k >= thr)
    need = topk - n_gt
    has_tie = n_ge > topk

    def tie_cut():
        def step(it, jp):
            cand = jp | lax.shift_left(jnp.int32(1), pos_bits - 1 - it)
            n = count(lambda k, j: jnp.logical_and(k == kth, j * TK + row_iota < cand))
            return jnp.where(n < need, cand, jp)
        return lax.fori_loop(0, pos_bits, step, jnp.zeros((1, TQ), I32))

    cut = lax.cond(jnp.max(jnp.where(has_tie, 1, 0)) > 0, tie_cut, lambda: jnp.zeros((1, TQ), I32))
    cut = jnp.where(has_tie, cut, jnp.int32(2 ** 30))

    def bias_tile(j, carry):
        k = keys_ref[j]
        sel = jnp.logical_and(k >= thr, jnp.logical_or(k > kth, j * TK + row_iota <= cut))
        bias_ref[j] = jnp.where(sel, 0.0, NEG_BIG)
        return carry

    lax.fori_loop(0, nkt, bias_tile, 0)

    row_lo = lax.broadcasted_iota(I32, (LANES, TQ), 0) < HEAD_DIM_A
    for p in range(N_HEADS_A // 2):
        res = []
        for h in (2 * p, 2 * p + 1):
            slope = float(2.0 ** (-8.0 * (h + 1) / N_HEADS_A))

            def att_tile(j, carry, h=h, slope=slope, p=p):
                m, l, acc = carry
                kt = ka_ref[pl.ds(pl.multiple_of(j * TK, TK), TK), p * LANES:(p + 1) * LANES]
                s = _dot_nt(kt, qah_ref[h])
                rel_pos = (j * TK - q0 + row_iota).astype(F32)
                s = s + slope * rel_pos + bias_ref[j]
                m_new = jnp.maximum(m, jnp.max(s, axis=0, keepdims=True))
                alpha = jnp.exp(m - m_new)
                pr = jnp.exp(s - m_new)
                l = alpha * l + jnp.sum(pr, axis=0, keepdims=True)
                pv = _dot(vat_ref[j, p * LANES:(p + 1) * LANES, :], pr.astype(BF16))
                return m_new, l, alpha * acc + pv

            init = (jnp.full((1, TQ), NEG_BIG, F32), jnp.zeros((1, TQ), F32), jnp.zeros((LANES, TQ), F32))
            _, l, acc = lax.fori_loop(0, nkt, att_tile, init)
            res.append(acc * (1.0 / l))
        ot_ref[p * LANES:(p + 1) * LANES, :] = jnp.where(row_lo, res[0], res[1])

    out_ref[...] = (ot_ref[...].T * sza_ref[...].astype(F32)).astype(BF16)


def _dsa_call(qa, qi, wit, sza, ka, vat, ki2, batch, seq):
    assert seq % TK == 0 and TK % TQ == 0
    nq = seq // TQ
    nkt = seq // TK
    topk = min(TOPK_MAX, seq // 4)
    pos_bits = max(1, int(np.ceil(np.log2(seq))))
    q_spec = lambda w: pl.BlockSpec((TQ, w), lambda b, i: (b * nq + i, 0))
    b_spec = lambda w: pl.BlockSpec((seq, w), lambda b, i: (b, 0))
    return pl.pallas_call(
        functools.partial(_dsa_body, topk=topk, pos_bits=pos_bits),
        out_shape=jax.ShapeDtypeStruct((batch * seq, WIDTH), BF16),
        grid=(batch, nq),
        in_specs=[q_spec(WIDTH), q_spec(WIDTH), pl.BlockSpec((16, TQ), lambda b, i: (0, b * nq + i)), q_spec(WIDTH),
                  b_spec(WIDTH), pl.BlockSpec((nkt, WIDTH, TK), lambda b, i: (b, 0, 0)), b_spec(LANES)],
        out_specs=q_spec(WIDTH),
        scratch_shapes=[pltpu.VMEM((nkt, TK, TQ), I32), pltpu.VMEM((nkt, TK, TQ), F32),
                        pltpu.VMEM((N_IDX_HEADS, TQ, LANES), BF16), pltpu.VMEM((N_HEADS_A, TQ, LANES), BF16),
                        pltpu.VMEM((WIDTH, TQ), F32)],
        compiler_params=pltpu.CompilerParams(dimension_semantics=("parallel", "arbitrary"),
                                             vmem_limit_bytes=VMEM_LIMIT),
        name="dsa",
    )(qa, qi, wit, sza, ka, vat, ki2)


def _split3(v):
    hi = v.astype(BF16)
    r = v - hi.astype(F32)
    mid = r.astype(BF16)
    lo = (r - mid.astype(F32)).astype(BF16)
    return hi, mid, lo


def _hgrn_body(sqb_ref, logf_ref, kk_ref, ib_ref, ibt_ref, sgb_ref, gon_ref, out_ref, st_ref):
    @pl.when(pl.program_id(1) == 0)
    def _():
        st_ref[...] = jnp.zeros_like(st_ref)

    r_i = lax.broadcasted_iota(I32, (CHUNK, CHUNK), 0)
    c_i = lax.broadcasted_iota(I32, (CHUNK, CHUNK), 1)
    tril = c_i <= r_i
    tri = jnp.where(tril, 1.0, 0.0).astype(BF16)
    n_sub = CHUNK // SUB
    row_c = lax.broadcasted_iota(I32, (CHUNK, LANES), 0)
    row_t = lax.broadcasted_iota(I32, (T_HGRN, LANES), 0)

    for c in range(T_HGRN // CHUNK):
        rows = slice(c * CHUNK, (c + 1) * CHUNK)
        g = logf_ref[rows, :]
        b = sum(_dot(tri, part) for part in _split3(g))
        q = sqb_ref[rows, :].astype(F32)
        kk = kk_ref[rows, :].astype(F32)
        b_last = b[CHUNK - 1:CHUNK, :]
        ref_k = jnp.concatenate(
            [jnp.broadcast_to(b[(j + 1) * SUB - 1:(j + 1) * SUB, :], (SUB, WIDTH)) for j in range(n_sub)], axis=0)
        k_in = kk * jnp.exp(ref_k - b)
        q_out = (q * jnp.exp(b)).astype(BF16)
        k_st = (kk * jnp.exp(b_last - b)).astype(BF16)
        q_in = []
        for j in range(n_sub):
            lo = j * SUB
            qj = q[lo:, :] * jnp.exp(b[lo:, :] - b[lo + SUB - 1:lo + SUB, :])
            if lo:
                qj = jnp.concatenate([jnp.zeros((lo, WIDTH), F32), qj], axis=0)
            q_in.append(qj.astype(BF16))
        for hd in range(N_HEADS_B):
            cols = slice(hd * HEAD_B, (hd + 1) * HEAD_B)
            qs = jnp.concatenate([qj[:, cols] for qj in q_in], axis=1)
            kh = k_in[:, cols]
            ks = jnp.concatenate(
                [jnp.where((row_c >= j * SUB) & (row_c < (j + 1) * SUB), kh, 0.0) for j in range(n_sub)],
                axis=1).astype(BF16)
            a = jnp.where(tril, _dot_nt(qs, ks), 0.0).astype(BF16)
            st = st_ref[hd]
            o = _dot(a, ib_ref[rows, cols]) + _dot_nt(q_out[:, cols], st.astype(BF16))
            k_pad = jnp.where((row_t >= c * CHUNK) & (row_t < (c + 1) * CHUNK),
                              jnp.concatenate([k_st[:, cols]] * (T_HGRN // CHUNK), axis=0), jnp.zeros((), BF16))
            st_ref[hd] = st * jnp.exp(b_last[:, cols]) + _dot(ibt_ref[0, cols, :], k_pad)
            ms = jnp.mean(o * o, axis=-1, keepdims=True)
            o = o * lax.rsqrt(ms + EPS) * gon_ref[...]
            out_ref[rows, cols] = (o * sgb_ref[rows, cols].astype(F32)).astype(BF16)


def _hgrn_call(sqb, logf, kk, ib, ibt, sgb, o_norm_b, batch, seq):
    assert TM_PROJ % T_HGRN == 0 and seq % TM_PROJ == 0
    nt = seq // T_HGRN
    per = TM_PROJ // T_HGRN
    spec = pl.BlockSpec((T_HGRN, WIDTH), lambda b, t: (b * nt + t, 0))
    ibt_spec = pl.BlockSpec((1, WIDTH, T_HGRN), lambda b, t: ((b * nt + t) // per, 0, (b * nt + t) % per))
    return pl.pallas_call(
        _hgrn_body,
        out_shape=jax.ShapeDtypeStruct((batch * seq, WIDTH), BF16),
        grid=(batch, nt),
        in_specs=[spec, spec, spec, spec, ibt_spec, spec, _const_spec((1, HEAD_B))],
        out_specs=spec,
        scratch_shapes=[pltpu.VMEM((N_HEADS_B, HEAD_B, HEAD_B), F32)],
        compiler_params=pltpu.CompilerParams(dimension_semantics=("parallel", "arbitrary"),
                                             vmem_limit_bytes=VMEM_LIMIT),
        name="hgrn",
    )(sqb, logf, kk, ib, ibt, sgb, o_norm_b.astype(F32)[None, :])


def _merge_body(x_ref, oa_ref, ob_ref, qc_ref, szc_ref, gates_ref, kc_ref, vc_ref, wb_ref, wo_ref, out_ref, *, d_model):
    oc = []
    for hd in range(N_HEADS_C):
        cols = slice(hd * HEAD_DIM_C, (hd + 1) * HEAD_DIM_C)
        s = _dot_nt(qc_ref[:, cols], kc_ref[:, cols])
        m = jnp.max(s, axis=-1, keepdims=True)
        p = jnp.exp(s - m)
        l = jnp.sum(p, axis=-1, keepdims=True)
        oc.append(_dot(p.astype(BF16), vc_ref[:, cols]) * (1.0 / l))
    oc = (jnp.concatenate(oc, axis=1) * szc_ref[...].astype(F32)).astype(BF16)
    merged = jnp.zeros((x_ref.shape[0], d_model), F32)
    for n, br in enumerate((oa_ref[...], ob_ref[...], oc)):
        y = _dot(br, wb_ref[n])
        merged = merged + gates_ref[:, n * d_model:(n + 1) * d_model].astype(F32) * y
    out_ref[...] = x_ref[...] + _dot(merged.astype(BF16), wo_ref[...])


def _merge_call(x2, oa, ob, qc, szc, gates, kc, vc, w_branch, w_out, batch, seq, mem_len):
    rows, d_model = x2.shape
    tm = TM_MERGE
    nt = seq // tm
    row_spec = lambda w: pl.BlockSpec((tm, w), lambda b, t: (b * nt + t, 0))
    mem_spec = pl.BlockSpec((mem_len, WIDTH), lambda b, t: (b, 0))
    return pl.pallas_call(
        functools.partial(_merge_body, d_model=d_model),
        out_shape=jax.ShapeDtypeStruct((rows, d_model), F32),
        grid=(batch, nt),
        in_specs=[row_spec(d_model), row_spec(WIDTH), row_spec(WIDTH), row_spec(WIDTH), row_spec(WIDTH),
                  row_spec(N_BRANCH * d_model), mem_spec, mem_spec,
                  _const_spec((N_BRANCH, WIDTH, d_model)), _const_spec((d_model, d_model))],
        out_specs=row_spec(d_model),
        compiler_params=pltpu.CompilerParams(dimension_semantics=("parallel", "parallel"),
                                             vmem_limit_bytes=VMEM_LIMIT),
        name="merge",
    )(x2, oa, ob, qc, szc, gates, kc, vc, w_branch.astype(BF16), w_out.astype(BF16))


def _layer(x, mem, norm_in, norm_mem, w_in, q_norm_a, k_norm_a, lower_bounds, o_norm_b, w_mem_kv, q_norm_c, k_norm_c,
           w_branch, w_out):
    batch, seq, d_model = x.shape
    mem_len = mem.shape[1]
    x2 = x.reshape(batch * seq, d_model)
    (qa, ka, vat, sza, qi, ki2, wit, sqb, logf, kk, ib, ibt, sgb, qc, szc, gates) = _proj_call(
        x2, norm_in, w_in, lower_bounds, q_norm_a, k_norm_a, q_norm_c)
    kc, vc = _memkv_call(mem.reshape(batch * mem_len, d_model), norm_mem, w_mem_kv, k_norm_c, mem_len)
    oa = _dsa_call(qa, qi, wit, sza, ka, vat, ki2, batch, seq)
    ob = _hgrn_call(sqb, logf, kk, ib, ibt, sgb, o_norm_b, batch, seq)
    out = _merge_call(x2, oa, ob, qc, szc, gates, kc, vc, w_branch, w_out, batch, seq, mem_len)
    return out.reshape(batch, seq, d_model)


def kernel(x, mem, norm_in, norm_mem, w_in, q_norm_a, k_norm_a, lower_bounds, o_norm_b, w_mem_kv, q_norm_c, k_norm_c,
           w_branch, w_out):
    assert norm_in.shape[0] == 1, "single-layer block"
    return _layer(x, mem, norm_in[0], norm_mem[0], w_in[0], q_norm_a[0], k_norm_a[0], lower_bounds, o_norm_b[0],
                  w_mem_kv[0], q_norm_c[0], k_norm_c[0], w_branch[0], w_out[0])
```

```python
import functools

import jax
import jax.numpy as jnp
import numpy as np
from jax import lax
from jax.experimental import pallas as pl
from jax.experimental.pallas import tpu as pltpu

F32 = jnp.float32
BF16 = jnp.bfloat16
I32 = jnp.int32

N_HEADS_A = 8
HEAD_DIM_A = 64
N_IDX_HEADS = 8
IDX_DIM = 64
TOPK_MAX = 256
N_HEADS_B = 4
HEAD_B = 128
N_HEADS_C = 4
HEAD_DIM_C = 128
N_BRANCH = 3
EPS = 1e-6
WIDTH = 512
IDX_SCALE = (IDX_DIM ** -0.5) * (N_IDX_HEADS ** -0.5)

LANES = 128
VMEM_LIMIT = 56 * 1024 * 1024

TM_PROJ = 256
TQ = 128
TKS = 128
TKC = 512
CHUNK = 64
SUB = 16
T_HGRN = 128
TM_MERGE = 256

NEG_BIG = -1e30
INT_MIN = -2147483648
KEY_LOWEST = -2139095040

NT = (((1,), (1,)), ((), ()))


def _dot(a, b):
    return jnp.dot(a, b, preferred_element_type=F32)


def _dot_nt(a, b):
    return lax.dot_general(a, b, NT, preferred_element_type=F32)


def _group_meansq(v, ones_ref, group):
    sq = v * v
    hi = sq.astype(BF16)
    lo = (sq - hi.astype(F32)).astype(BF16)
    s = _dot(hi, ones_ref[...]) + _dot(lo, ones_ref[...])
    return s * (1.0 / group)


_C_QA, _C_KA, _C_ZA, _C_QB, _C_FB, _C_IB, _C_GB, _C_QC, _C_ZC = [WIDTH * i for i in range(9)]
_C_GATES = 9 * WIDTH
_C_QI = _C_GATES + 3 * 1024
_C_KI = _C_QI + WIDTH
_C_END = _C_KI + LANES


def _proj_body(x_ref, nin_ref, w_ref, wvt_ref, wwt_ref, lbp_ref, gqa_ref, gka_ref, gqc_ref, ones64_ref, ones128_ref,
               qa_ref, ka_ref, vat_ref, sza_ref, qi_ref, ki2_ref, wit_ref,
               sqb_ref, logf_ref, kk_ref, ib_ref, ibt_ref, sgb_ref, qc_ref, szc_ref, gates_ref, *, d_model):
    x = x_ref[...]
    ms = jnp.mean(x * x, axis=-1, keepdims=True)
    h = (x * lax.rsqrt(ms + EPS)) * nin_ref[...]
    hb = h.astype(BF16)

    def proj(c0, width):
        return _dot(hb, w_ref[:, c0:c0 + width])

    qa = proj(_C_QA, WIDTH)
    qa = qa * lax.rsqrt(_group_meansq(qa, ones64_ref, HEAD_DIM_A) + EPS) * gqa_ref[...]
    qa_ref[...] = (qa * (HEAD_DIM_A ** -0.5)).astype(BF16)
    ka = proj(_C_KA, WIDTH)
    ka = ka * lax.rsqrt(_group_meansq(ka, ones64_ref, HEAD_DIM_A) + EPS) * gka_ref[...]
    ka_ref[...] = ka.astype(BF16)
    vat = _dot_nt(wvt_ref[0:WIDTH, :], hb).astype(BF16)
    for t in range(vat_ref.shape[0]):
        vat_ref[t] = vat[:, t * LANES:(t + 1) * LANES]
    sza_ref[...] = jax.nn.silu(proj(_C_ZA, WIDTH)).astype(BF16)
    qi_ref[...] = proj(_C_QI, WIDTH).astype(BF16)
    ki2_ref[...] = proj(_C_KI, LANES).astype(BF16)
    wit_ref[...] = _dot_nt(wwt_ref[...], hb)
    sqb_ref[...] = jax.nn.silu(proj(_C_QB, WIDTH)).astype(BF16)
    lbp = lbp_ref[...]
    lbe = jnp.exp(lbp - jnp.max(lbp, axis=0, keepdims=True))
    lb = lbe[0:1, :] / jnp.sum(lbe, axis=0, keepdims=True)
    f = lb + (1.0 - lb) * jax.nn.sigmoid(proj(_C_FB, WIDTH))
    logf_ref[...] = jnp.log(f)
    kk_ref[...] = (1.0 - f).astype(BF16)
    ib_ref[...] = proj(_C_IB, WIDTH).astype(BF16)
    ibt = _dot_nt(wvt_ref[WIDTH:2 * WIDTH, :], hb).astype(BF16)
    for t in range(ibt_ref.shape[0]):
        ibt_ref[t] = ibt[:, t * LANES:(t + 1) * LANES]
    sgb_ref[...] = jax.nn.silu(proj(_C_GB, WIDTH)).astype(BF16)
    qc = proj(_C_QC, WIDTH)
    qc = qc * lax.rsqrt(_group_meansq(qc, ones128_ref, HEAD_DIM_C) + EPS) * gqc_ref[...]
    qc_ref[...] = (qc * (HEAD_DIM_C ** -0.5)).astype(BF16)
    szc_ref[...] = jax.nn.silu(proj(_C_ZC, WIDTH)).astype(BF16)
    for n in range(N_BRANCH):
        gates_ref[:, n * d_model:(n + 1) * d_model] = jax.nn.sigmoid(
            proj(_C_GATES + n * d_model, d_model)).astype(BF16)


def _const_spec(shape):
    nd = len(shape)
    return pl.BlockSpec(shape, lambda *_: (0,) * nd, pipeline_mode=pl.Buffered(1))


def _block_ones(width, group):
    g = np.arange(width) // group
    return jnp.asarray((g[:, None] == g[None, :]).astype(np.float32), dtype=BF16)


def _proj_call(x2, norm_in, w_in, lower_bounds, q_norm_a, k_norm_a, q_norm_c):
    rows, d_model = x2.shape
    tm = TM_PROJ
    assert rows % tm == 0 and d_model == 1024
    offs = np.cumsum([0, 512, 512, 512, 512, 512, 64, 8, 512, 512, 512, 512, 512, 512, 3 * d_model])
    (o_qa, o_ka, o_va, o_za, o_qi, o_ki, o_wi, o_qb, o_fb, o_ib, o_gb, o_qc, o_zc, o_g, o_end) = [int(o) for o in offs]
    assert o_end == w_in.shape[1]
    wb = w_in.astype(BF16)
    col = lambda a, b: wb[:, a:b]
    w_main = jnp.concatenate([
        col(o_qa, o_ka), col(o_ka, o_va), col(o_za, o_qi), col(o_qb, o_fb), col(o_fb, o_ib), col(o_ib, o_gb),
        col(o_gb, o_qc), col(o_qc, o_zc), col(o_zc, o_g), col(o_g, o_end), col(o_qi, o_ki),
        col(o_ki, o_wi), col(o_ki, o_wi)], axis=1)
    assert w_main.shape[1] == _C_END
    wvt = jnp.concatenate([col(o_va, o_za).T, col(o_ib, o_gb).T], axis=0)
    wwt = jnp.concatenate([col(o_wi, o_qb).T, jnp.zeros((8, d_model), BF16)], axis=0)
    tile = lambda g, reps: jnp.tile(g.astype(F32), reps)[None, :]
    n_slots = lower_bounds.shape[0]

    row_spec = lambda w: pl.BlockSpec((tm, w), lambda i: (i, 0))
    t_spec = pl.BlockSpec((tm // LANES, WIDTH, LANES), lambda i: (i, 0, 0))
    out_shape = [
        jax.ShapeDtypeStruct((rows, WIDTH), BF16),
        jax.ShapeDtypeStruct((rows, WIDTH), BF16),
        jax.ShapeDtypeStruct((rows // LANES, WIDTH, LANES), BF16),
        jax.ShapeDtypeStruct((rows, WIDTH), BF16),
        jax.ShapeDtypeStruct((rows, WIDTH), BF16),
        jax.ShapeDtypeStruct((rows, LANES), BF16),
        jax.ShapeDtypeStruct((16, rows), F32),
        jax.ShapeDtypeStruct((rows, WIDTH), BF16),
        jax.ShapeDtypeStruct((rows, WIDTH), F32),
        jax.ShapeDtypeStruct((rows, WIDTH), BF16),
        jax.ShapeDtypeStruct((rows, WIDTH), BF16),
        jax.ShapeDtypeStruct((rows // LANES, WIDTH, LANES), BF16),
        jax.ShapeDtypeStruct((rows, WIDTH), BF16),
        jax.ShapeDtypeStruct((rows, WIDTH), BF16),
        jax.ShapeDtypeStruct((rows, WIDTH), BF16),
        jax.ShapeDtypeStruct((rows, N_BRANCH * d_model), BF16),
    ]
    out_specs = [row_spec(WIDTH), row_spec(WIDTH), t_spec, row_spec(WIDTH), row_spec(WIDTH), row_spec(LANES),
                 pl.BlockSpec((16, tm), lambda i: (0, i)),
                 row_spec(WIDTH), row_spec(WIDTH), row_spec(WIDTH), row_spec(WIDTH), t_spec, row_spec(WIDTH),
                 row_spec(WIDTH), row_spec(WIDTH), row_spec(N_BRANCH * d_model)]
    in_specs = [row_spec(d_model), _const_spec((1, d_model)), _const_spec(w_main.shape), _const_spec(wvt.shape),
                _const_spec(wwt.shape), _const_spec((n_slots, WIDTH)), _const_spec((1, WIDTH)),
                _const_spec((1, WIDTH)), _const_spec((1, WIDTH)), _const_spec((WIDTH, WIDTH)),
                _const_spec((WIDTH, WIDTH))]
    return pl.pallas_call(
        functools.partial(_proj_body, d_model=d_model),
        out_shape=out_shape, grid=(rows // tm,), in_specs=in_specs, out_specs=out_specs,
        compiler_params=pltpu.CompilerParams(dimension_semantics=("parallel",), vmem_limit_bytes=VMEM_LIMIT),
        name="proj",
    )(x2, norm_in.astype(F32)[None, :], w_main, wvt, wwt, lower_bounds.astype(F32),
      tile(q_norm_a, N_HEADS_A), tile(k_norm_a, N_HEADS_A), tile(q_norm_c, N_HEADS_C),
      _block_ones(WIDTH, HEAD_DIM_A), _block_ones(WIDTH, HEAD_DIM_C))


def _memkv_body(m_ref, nm_ref, w_ref, gk_ref, ones128_ref, kc_ref, vc_ref):
    x = m_ref[...]
    ms = jnp.mean(x * x, axis=-1, keepdims=True)
    hb = ((x * lax.rsqrt(ms + EPS)) * nm_ref[...]).astype(BF16)
    kc = _dot(hb, w_ref[:, 0:WIDTH])
    kc = kc * lax.rsqrt(_group_meansq(kc, ones128_ref, HEAD_DIM_C) + EPS) * gk_ref[...]
    kc_ref[...] = kc.astype(BF16)
    vc_ref[...] = _dot(hb, w_ref[:, WIDTH:2 * WIDTH]).astype(BF16)


def _memkv_call(mem2, norm_mem, w_mem_kv, k_norm_c, tm):
    rows, d_model = mem2.shape
    row_spec = lambda w: pl.BlockSpec((tm, w), lambda i: (i, 0))
    return pl.pallas_call(
        _memkv_body,
        out_shape=[jax.ShapeDtypeStruct((rows, WIDTH), BF16)] * 2,
        grid=(rows // tm,),
        in_specs=[row_spec(d_model), _const_spec((1, d_model)), _const_spec((d_model, 2 * WIDTH)),
                  _const_spec((1, WIDTH)), _const_spec((WIDTH, WIDTH))],
        out_specs=[row_spec(WIDTH), row_spec(WIDTH)],
        compiler_params=pltpu.CompilerParams(dimension_semantics=("parallel",), vmem_limit_bytes=VMEM_LIMIT),
        name="memkv",
    )(mem2, norm_mem.astype(F32)[None, :], w_mem_kv.astype(BF16),
      jnp.tile(k_norm_c.astype(F32), N_HEADS_C)[None, :], _block_ones(WIDTH, HEAD_DIM_C))


def _key_to_f32(k):
    return pltpu.bitcast(jnp.where(k < 0, k ^ jnp.int32(0x7FFFFFFF), k), F32)


def _dsa_body(qa_ref, qi_ref, wit_ref, sza_ref, ka_ref, vat_ref, ki2_ref, out_ref,
              sc_ref, qir_ref, qar_ref, m_ref, l_ref, acc_ref, ot_ref, *, topk, pos_bits):
    i = pl.program_id(1)
    q0 = i * TQ
    n_att = i + 1
    n_cnt = (q0 + TQ + TKC - 1) // TKC
    n_sc = n_cnt * (TKC // TKS)

    lane = lax.broadcasted_iota(I32, (TQ, LANES), 1)
    lane_lo = lane < HEAD_DIM_A
    zero = jnp.zeros((), BF16)
    for p in range(N_HEADS_A // 2):
        cols = slice(p * LANES, (p + 1) * LANES)
        for half, keep in ((0, lane_lo), (1, jnp.logical_not(lane_lo))):
            rows = slice(half * TQ, (half + 1) * TQ)
            slope = 2.0 ** (-8.0 * (2 * p + half + 1) / N_HEADS_A)
            qir_ref[p, rows, :] = jnp.where(keep, qi_ref[:, cols], zero)
            qar_ref[p, rows, 0:LANES] = jnp.where(keep, qa_ref[:, cols], zero)
            qar_ref[p, rows, LANES:2 * LANES] = jnp.where(lane == 0, slope, 0.0).astype(BF16)
    m_ref[...] = jnp.full(m_ref.shape, NEG_BIG, F32)
    l_ref[...] = jnp.zeros(l_ref.shape, F32)
    acc_ref[...] = jnp.zeros(acc_ref.shape, F32)

    row_s = lax.broadcasted_iota(I32, (TKS, TQ), 0)
    tpos = q0 + lax.broadcasted_iota(I32, (TKS, TQ), 1)

    def score_tile(j, carry):
        r0 = pl.multiple_of(j * TKS, TKS)
        kt = ki2_ref[pl.ds(r0, TKS), :]
        acc = jnp.zeros((TKS, TQ), F32)
        for p in range(N_IDX_HEADS // 2):
            rel = jnp.maximum(_dot_nt(kt, qir_ref[p]), 0.0)
            acc = acc + rel[:, :TQ] * wit_ref[2 * p:2 * p + 1, :] + rel[:, TQ:] * wit_ref[2 * p + 1:2 * p + 2, :]
        sc = acc * IDX_SCALE
        sc = jnp.where(sc == 0.0, 0.0, sc)
        sc_ref[pl.ds(r0, TKS), :] = jnp.where(r0 + row_s <= tpos, sc, -jnp.inf)
        return carry

    lax.fori_loop(0, n_sc, score_tile, 0)

    row_c = lax.broadcasted_iota(I32, (TKC, TQ), 0)

    def count(pred):
        def tile(j, acc):
            r0 = pl.multiple_of(j * TKC, TKC)
            c = jnp.where(pred(sc_ref[pl.ds(r0, TKC), :], r0), jnp.int32(1), jnp.int32(0))
            return acc + jnp.sum(c.reshape(TKC // 8, 8, TQ), axis=0)
        acc = lax.fori_loop(0, n_cnt, tile, jnp.zeros((8, TQ), I32))
        return jnp.sum(acc, axis=0, keepdims=True)

    def bisect(it, p):
        cand = p ^ lax.shift_left(jnp.int32(1), 31 - it)
        cand_f = _key_to_f32(cand)
        n = count(lambda s, r0: s >= cand_f)
        return jnp.where(n >= topk, cand, p)

    kth = lax.fori_loop(0, 32, bisect, jnp.full((1, TQ), INT_MIN, I32))
    kth = jnp.maximum(kth, jnp.int32(KEY_LOWEST))
    thr = _key_to_f32(kth)
    nxt = _key_to_f32(kth + 1)
    n_gt = count(lambda s, r0: s >= nxt)
    n_ge = count(lambda s, r0: s >= thr)
    need = topk - n_gt
    has_tie = n_ge > topk

    def tie_cut():
        def step(it, jp):
            cand = jp | lax.shift_left(jnp.int32(1), pos_bits - 1 - it)
            n = count(lambda s, r0: (s >= thr) & (s < nxt) & (r0 + row_c < cand))
            return jnp.where(n < need, cand, jp)
        return lax.fori_loop(0, pos_bits, step, jnp.zeros((1, TQ), I32))

    cut = lax.cond(jnp.max(jnp.where(has_tie, 1, 0)) > 0, tie_cut, lambda: jnp.zeros((1, TQ), I32))
    cut = jnp.where(has_tie, cut, jnp.int32(2 ** 30))

    lane_a = lax.broadcasted_iota(I32, (TKS, LANES), 1)
    aug = jnp.where(lane_a == 0, lax.broadcasted_iota(I32, (TKS, LANES), 0).astype(F32), 0.0).astype(BF16)
    half2 = lax.broadcasted_iota(I32, (1, 2 * TQ), 1) < TQ

    def att_tile(j, carry):
        r0 = pl.multiple_of(j * TKS, TKS)
        s_idx = sc_ref[pl.ds(r0, TKS), :]
        sel = (s_idx >= thr) & ((s_idx >= nxt) | (r0 + row_s <= cut))
        bias = jnp.where(sel, 0.0, NEG_BIG)
        bias2 = jnp.concatenate([bias, bias], axis=1)
        base = (r0 - q0).astype(F32)
        for p in range(N_HEADS_A // 2):
            s0 = 2.0 ** (-8.0 * (2 * p + 1) / N_HEADS_A)
            s1 = 2.0 ** (-8.0 * (2 * p + 2) / N_HEADS_A)
            shift = jnp.where(half2, s0, s1) * base
            kt = jnp.concatenate([ka_ref[pl.ds(r0, TKS), p * LANES:(p + 1) * LANES], aug], axis=1)
            s = _dot_nt(kt, qar_ref[p]) + bias2
            m_old = m_ref[p]
            m_new = jnp.maximum(m_old, jnp.max(s, axis=0, keepdims=True) + shift)
            alpha = jnp.exp(m_old - m_new)
            pr = jnp.exp(s - (m_new - shift))
            l_ref[p] = alpha * l_ref[p] + jnp.sum(pr, axis=0, keepdims=True)
            pv = _dot(vat_ref[j, p * LANES:(p + 1) * LANES, :], pr.astype(BF16))
            acc_ref[p] = alpha * acc_ref[p] + pv
            m_ref[p] = m_new
        return carry

    lax.fori_loop(0, n_att, att_tile, 0)

    row_lo = lax.broadcasted_iota(I32, (LANES, TQ), 0) < HEAD_DIM_A
    for p in range(N_HEADS_A // 2):
        o = acc_ref[p] * (1.0 / l_ref[p])
        ot_ref[p * LANES:(p + 1) * LANES, :] = jnp.where(row_lo, o[:, :TQ], o[:, TQ:])
    out_ref[...] = (ot_ref[...].T * sza_ref[...].astype(F32)).astype(BF16)


def _dsa_call(qa, qi, wit, sza, ka, vat, ki2, batch, seq):
    assert TQ == TKS and seq % TKC == 0
    nq = seq // TQ
    topk = min(TOPK_MAX, seq // 4)
    pos_bits = max(1, int(np.ceil(np.log2(seq))))
    q_spec = lambda w: pl.BlockSpec((TQ, w), lambda b, i: (b * nq + i, 0))
    b_spec = lambda w: pl.BlockSpec((seq, w), lambda b, i: (b, 0))
    n_pair = N_HEADS_A // 2
    return pl.pallas_call(
        functools.partial(_dsa_body, topk=topk, pos_bits=pos_bits),
        out_shape=jax.ShapeDtypeStruct((batch * seq, WIDTH), BF16),
        grid=(batch, nq),
        in_specs=[q_spec(WIDTH), q_spec(WIDTH), pl.BlockSpec((16, TQ), lambda b, i: (0, b * nq + i)), q_spec(WIDTH),
                  b_spec(WIDTH), pl.BlockSpec((seq // TKS, WIDTH, TKS), lambda b, i: (b, 0, 0)), b_spec(LANES)],
        out_specs=q_spec(WIDTH),
        scratch_shapes=[pltpu.VMEM((seq, TQ), F32),
                        pltpu.VMEM((n_pair, 2 * TQ, LANES), BF16), pltpu.VMEM((n_pair, 2 * TQ, 2 * LANES), BF16),
                        pltpu.VMEM((n_pair, 1, 2 * TQ), F32), pltpu.VMEM((n_pair, 1, 2 * TQ), F32),
                        pltpu.VMEM((n_pair, LANES, 2 * TQ), F32), pltpu.VMEM((WIDTH, TQ), F32)],
        compiler_params=pltpu.CompilerParams(dimension_semantics=("parallel", "arbitrary"),
                                             vmem_limit_bytes=VMEM_LIMIT),
        name="dsa",
    )(qa, qi, wit, sza, ka, vat, ki2)


def _split3(v):
    hi = v.astype(BF16)
    r = v - hi.astype(F32)
    mid = r.astype(BF16)
    lo = (r - mid.astype(F32)).astype(BF16)
    return hi, mid, lo


def _hgrn_body(sqb_ref, logf_ref, kk_ref, ib_ref, ibt_ref, sgb_ref, gon_ref, out_ref, st_ref):
    @pl.when(pl.program_id(1) == 0)
    def _():
        st_ref[...] = jnp.zeros_like(st_ref)

    r_i = lax.broadcasted_iota(I32, (CHUNK, CHUNK), 0)
    c_i = lax.broadcasted_iota(I32, (CHUNK, CHUNK), 1)
    tril = c_i <= r_i
    tri = jnp.where(tril, 1.0, 0.0).astype(BF16)
    n_sub = CHUNK // SUB
    row_c = lax.broadcasted_iota(I32, (CHUNK, LANES), 0)
    row_t = lax.broadcasted_iota(I32, (T_HGRN, LANES), 0)

    for c in range(T_HGRN // CHUNK):
        rows = slice(c * CHUNK, (c + 1) * CHUNK)
        g = logf_ref[rows, :]
        b = sum(_dot(tri, part) for part in _split3(g))
        q = sqb_ref[rows, :].astype(F32)
        kk = kk_ref[rows, :].astype(F32)
        b_last = b[CHUNK - 1:CHUNK, :]
        ref_k = jnp.concatenate(
            [jnp.broadcast_to(b[(j + 1) * SUB - 1:(j + 1) * SUB, :], (SUB, WIDTH)) for j in range(n_sub)], axis=0)
        k_in = kk * jnp.exp(ref_k - b)
        q_out = (q * jnp.exp(b)).astype(BF16)
        k_st = (kk * jnp.exp(b_last - b)).astype(BF16)
        q_in = []
        for j in range(n_sub):
            lo = j * SUB
            qj = q[lo:, :] * jnp.exp(b[lo:, :] - b[lo + SUB - 1:lo + SUB, :])
            if lo:
                qj = jnp.concatenate([jnp.zeros((lo, WIDTH), F32), qj], axis=0)
            q_in.append(qj.astype(BF16))
        for hd in range(N_HEADS_B):
            cols = slice(hd * HEAD_B, (hd + 1) * HEAD_B)
            qs = jnp.concatenate([qj[:, cols] for qj in q_in], axis=1)
            kh = k_in[:, cols]
            ks = jnp.concatenate(
                [jnp.where((row_c >= j * SUB) & (row_c < (j + 1) * SUB), kh, 0.0) for j in range(n_sub)],
                axis=1).astype(BF16)
            a = jnp.where(tril, _dot_nt(qs, ks), 0.0).astype(BF16)
            st = st_ref[hd]
            o = _dot(a, ib_ref[rows, cols]) + _dot_nt(q_out[:, cols], st.astype(BF16))
            k_pad = jnp.where((row_t >= c * CHUNK) & (row_t < (c + 1) * CHUNK),
                              jnp.concatenate([k_st[:, cols]] * (T_HGRN // CHUNK), axis=0), jnp.zeros((), BF16))
            st_ref[hd] = st * jnp.exp(b_last[:, cols]) + _dot(ibt_ref[0, cols, :], k_pad)
            ms = jnp.mean(o * o, axis=-1, keepdims=True)
            o = o * lax.rsqrt(ms + EPS) * gon_ref[...]
            out_ref[rows, cols] = (o * sgb_ref[rows, cols].astype(F32)).astype(BF16)


def _hgrn_call(sqb, logf, kk, ib, ibt, sgb, o_norm_b, batch, seq):
    assert T_HGRN == LANES and seq % T_HGRN == 0
    nt = seq // T_HGRN
    spec = pl.BlockSpec((T_HGRN, WIDTH), lambda b, t: (b * nt + t, 0))
    ibt_spec = pl.BlockSpec((1, WIDTH, T_HGRN), lambda b, t: (b * nt + t, 0, 0))
    return pl.pallas_call(
        _hgrn_body,
        out_shape=jax.ShapeDtypeStruct((batch * seq, WIDTH), BF16),
        grid=(batch, nt),
        in_specs=[spec, spec, spec, spec, ibt_spec, spec, _const_spec((1, HEAD_B))],
        out_specs=spec,
        scratch_shapes=[pltpu.VMEM((N_HEADS_B, HEAD_B, HEAD_B), F32)],
        compiler_params=pltpu.CompilerParams(dimension_semantics=("parallel", "arbitrary"),
                                             vmem_limit_bytes=VMEM_LIMIT),
        name="hgrn",
    )(sqb, logf, kk, ib, ibt, sgb, o_norm_b.astype(F32)[None, :])


def _merge_body(x_ref, oa_ref, ob_ref, qc_ref, szc_ref, gates_ref, kc_ref, vc_ref, wb_ref, wo_ref, out_ref, *, d_model):
    oc = []
    for hd in range(N_HEADS_C):
        cols = slice(hd * HEAD_DIM_C, (hd + 1) * HEAD_DIM_C)
        s = _dot_nt(qc_ref[:, cols], kc_ref[:, cols])
        m = jnp.max(s, axis=-1, keepdims=True)
        p = jnp.exp(s - m)
        l = jnp.sum(p, axis=-1, keepdims=True)
        oc.append(_dot(p.astype(BF16), vc_ref[:, cols]) * (1.0 / l))
    oc = (jnp.concatenate(oc, axis=1) * szc_ref[...].astype(F32)).astype(BF16)
    merged = jnp.zeros((x_ref.shape[0], d_model), F32)
    for n, br in enumerate((oa_ref[...], ob_ref[...], oc)):
        y = _dot(br, wb_ref[n])
        merged = merged + gates_ref[:, n * d_model:(n + 1) * d_model].astype(F32) * y
    out_ref[...] = x_ref[...] + _dot(merged.astype(BF16), wo_ref[...])


def _merge_call(x2, oa, ob, qc, szc, gates, kc, vc, w_branch, w_out, batch, seq, mem_len):
    rows, d_model = x2.shape
    tm = TM_MERGE
    nt = seq // tm
    row_spec = lambda w: pl.BlockSpec((tm, w), lambda b, t: (b * nt + t, 0))
    mem_spec = pl.BlockSpec((mem_len, WIDTH), lambda b, t: (b, 0))
    return pl.pallas_call(
        functools.partial(_merge_body, d_model=d_model),
        out_shape=jax.ShapeDtypeStruct((rows, d_model), F32),
        grid=(batch, nt),
        in_specs=[row_spec(d_model), row_spec(WIDTH), row_spec(WIDTH), row_spec(WIDTH), row_spec(WIDTH),
                  row_spec(N_BRANCH * d_model), mem_spec, mem_spec,
                  _const_spec((N_BRANCH, WIDTH, d_model)), _const_spec((d_model, d_model))],
        out_specs=row_spec(d_model),
        compiler_params=pltpu.CompilerParams(dimension_semantics=("parallel", "parallel"),
                                             vmem_limit_bytes=VMEM_LIMIT),
        name="merge",
    )(x2, oa, ob, qc, szc, gates, kc, vc, w_branch.astype(BF16), w_out.astype(BF16))


def _layer(x, mem, norm_in, norm_mem, w_in, q_norm_a, k_norm_a, lower_bounds, o_norm_b, w_mem_kv, q_norm_c, k_norm_c,
           w_branch, w_out):
    batch, seq, d_model = x.shape
    mem_len = mem.shape[1]
    x2 = x.reshape(batch * seq, d_model)
    (qa, ka, vat, sza, qi, ki2, wit, sqb, logf, kk, ib, ibt, sgb, qc, szc, gates) = _proj_call(
        x2, norm_in, w_in, lower_bounds, q_norm_a, k_norm_a, q_norm_c)
    kc, vc = _memkv_call(mem.reshape(batch * mem_len, d_model), norm_mem, w_mem_kv, k_norm_c, mem_len)
    oa = _dsa_call(qa, qi, wit, sza, ka, vat, ki2, batch, seq)
    ob = _hgrn_call(sqb, logf, kk, ib, ibt, sgb, o_norm_b, batch, seq)
    out = _merge_call(x2, oa, ob, qc, szc, gates, kc, vc, w_branch, w_out, batch, seq, mem_len)
    return out.reshape(batch, seq, d_model)


def kernel(x, mem, norm_in, norm_mem, w_in, q_norm_a, k_norm_a, lower_bounds, o_norm_b, w_mem_kv, q_norm_c, k_norm_c,
           w_branch, w_out):
    assert norm_in.shape[0] == 1, "single-layer block"
    return _layer(x, mem, norm_in[0], norm_mem[0], w_in[0], q_norm_a[0], k_norm_a[0], lower_bounds, o_norm_b[0],
                  w_mem_kv[0], q_norm_c[0], k_norm_c[0], w_branch[0], w_out[0])
```

```python
import functools

import jax
import jax.numpy as jnp
import numpy as np
from jax import lax
from jax.experimental import pallas as pl
from jax.experimental.pallas import tpu as pltpu

F32 = jnp.float32
BF16 = jnp.bfloat16
I32 = jnp.int32

N_HEADS_A = 8
HEAD_DIM_A = 64
N_IDX_HEADS = 8
IDX_DIM = 64
TOPK_MAX = 256
N_HEADS_B = 4
HEAD_B = 128
N_HEADS_C = 4
HEAD_DIM_C = 128
N_BRANCH = 3
EPS = 1e-6
WIDTH = 512
IDX_SCALE = (IDX_DIM ** -0.5) * (N_IDX_HEADS ** -0.5)

LANES = 128
VMEM_LIMIT = 56 * 1024 * 1024

TM_PROJ = 256
TQ = 128
TKS = 256
TKC = 512
CHUNK = 64
SUB = 16
T_HGRN = 128
TM_MERGE = 256

NEG_BIG = -1e30
INT_MIN = -2147483648
KEY_LOWEST = -2139095040

NT = (((1,), (1,)), ((), ()))


def _dot(a, b):
    return jnp.dot(a, b, preferred_element_type=F32)


def _dot_nt(a, b):
    return lax.dot_general(a, b, NT, preferred_element_type=F32)


def _group_meansq(v, ones_ref, group):
    sq = v * v
    hi = sq.astype(BF16)
    lo = (sq - hi.astype(F32)).astype(BF16)
    s = _dot(hi, ones_ref[...]) + _dot(lo, ones_ref[...])
    return s * (1.0 / group)


_C_QA, _C_KA, _C_ZA, _C_QB, _C_FB, _C_IB, _C_GB, _C_QC, _C_ZC = [WIDTH * i for i in range(9)]
_C_GATES = 9 * WIDTH
_C_QI = _C_GATES + 3 * 1024
_C_KI = _C_QI + WIDTH
_C_END = _C_KI + LANES


def _proj_body(x_ref, nin_ref, w_ref, wvt_ref, wwt_ref, lbp_ref, gqa_ref, gka_ref, gqc_ref, ones64_ref, ones128_ref,
               qa_ref, ka_ref, vat_ref, sza_ref, qi_ref, ki2_ref, wit_ref,
               sqb_ref, logf_ref, kk_ref, ib_ref, ibt_ref, sgb_ref, qc_ref, szc_ref, gates_ref, *, d_model):
    x = x_ref[...]
    ms = jnp.mean(x * x, axis=-1, keepdims=True)
    h = (x * lax.rsqrt(ms + EPS)) * nin_ref[...]
    hb = h.astype(BF16)

    def proj(c0, width):
        return _dot(hb, w_ref[:, c0:c0 + width])

    qa = proj(_C_QA, WIDTH)
    qa = qa * lax.rsqrt(_group_meansq(qa, ones64_ref, HEAD_DIM_A) + EPS) * gqa_ref[...]
    qa_ref[...] = (qa * (HEAD_DIM_A ** -0.5)).astype(BF16)
    ka = proj(_C_KA, WIDTH)
    ka = ka * lax.rsqrt(_group_meansq(ka, ones64_ref, HEAD_DIM_A) + EPS) * gka_ref[...]
    ka_ref[...] = ka.astype(BF16)
    vat = _dot_nt(wvt_ref[0:WIDTH, :], hb).astype(BF16)
    for t in range(vat_ref.shape[0]):
        vat_ref[t] = vat[:, t * LANES:(t + 1) * LANES]
    sza_ref[...] = jax.nn.silu(proj(_C_ZA, WIDTH)).astype(BF16)
    qi_ref[...] = proj(_C_QI, WIDTH).astype(BF16)
    ki2_ref[...] = proj(_C_KI, LANES).astype(BF16)
    wit_ref[...] = _dot_nt(wwt_ref[...], hb)
    sqb_ref[...] = jax.nn.silu(proj(_C_QB, WIDTH)).astype(BF16)
    lbp = lbp_ref[...]
    lbe = jnp.exp(lbp - jnp.max(lbp, axis=0, keepdims=True))
    lb = lbe[0:1, :] / jnp.sum(lbe, axis=0, keepdims=True)
    f = lb + (1.0 - lb) * jax.nn.sigmoid(proj(_C_FB, WIDTH))
    logf_ref[...] = jnp.log(f)
    kk_ref[...] = (1.0 - f).astype(BF16)
    ib_ref[...] = proj(_C_IB, WIDTH).astype(BF16)
    ibt = _dot_nt(wvt_ref[WIDTH:2 * WIDTH, :], hb).astype(BF16)
    for t in range(ibt_ref.shape[0]):
        ibt_ref[t] = ibt[:, t * LANES:(t + 1) * LANES]
    sgb_ref[...] = jax.nn.silu(proj(_C_GB, WIDTH)).astype(BF16)
    qc = proj(_C_QC, WIDTH)
    qc = qc * lax.rsqrt(_group_meansq(qc, ones128_ref, HEAD_DIM_C) + EPS) * gqc_ref[...]
    qc_ref[...] = (qc * (HEAD_DIM_C ** -0.5)).astype(BF16)
    szc_ref[...] = jax.nn.silu(proj(_C_ZC, WIDTH)).astype(BF16)
    for n in range(N_BRANCH):
        gates_ref[:, n * d_model:(n + 1) * d_model] = jax.nn.sigmoid(
            proj(_C_GATES + n * d_model, d_model)).astype(BF16)


def _const_spec(shape):
    nd = len(shape)
    return pl.BlockSpec(shape, lambda *_: (0,) * nd, pipeline_mode=pl.Buffered(1))


def _block_ones(width, group):
    g = np.arange(width) // group
    return jnp.asarray((g[:, None] == g[None, :]).astype(np.float32), dtype=BF16)


def _proj_call(x2, norm_in, w_in, lower_bounds, q_norm_a, k_norm_a, q_norm_c):
    rows, d_model = x2.shape
    tm = TM_PROJ
    assert rows % tm == 0 and d_model == 1024
    offs = np.cumsum([0, 512, 512, 512, 512, 512, 64, 8, 512, 512, 512, 512, 512, 512, 3 * d_model])
    (o_qa, o_ka, o_va, o_za, o_qi, o_ki, o_wi, o_qb, o_fb, o_ib, o_gb, o_qc, o_zc, o_g, o_end) = [int(o) for o in offs]
    assert o_end == w_in.shape[1]
    wb = w_in.astype(BF16)
    col = lambda a, b: wb[:, a:b]
    w_main = jnp.concatenate([
        col(o_qa, o_ka), col(o_ka, o_va), col(o_za, o_qi), col(o_qb, o_fb), col(o_fb, o_ib), col(o_ib, o_gb),
        col(o_gb, o_qc), col(o_qc, o_zc), col(o_zc, o_g), col(o_g, o_end), col(o_qi, o_ki),
        col(o_ki, o_wi), col(o_ki, o_wi)], axis=1)
    assert w_main.shape[1] == _C_END
    wvt = jnp.concatenate([col(o_va, o_za).T, col(o_ib, o_gb).T], axis=0)
    wwt = jnp.concatenate([col(o_wi, o_qb).T, jnp.zeros((8, d_model), BF16)], axis=0)
    tile = lambda g, reps: jnp.tile(g.astype(F32), reps)[None, :]
    n_slots = lower_bounds.shape[0]

    row_spec = lambda w: pl.BlockSpec((tm, w), lambda i: (i, 0))
    t_spec = pl.BlockSpec((tm // LANES, WIDTH, LANES), lambda i: (i, 0, 0))
    out_shape = [
        jax.ShapeDtypeStruct((rows, WIDTH), BF16),
        jax.ShapeDtypeStruct((rows, WIDTH), BF16),
        jax.ShapeDtypeStruct((rows // LANES, WIDTH, LANES), BF16),
        jax.ShapeDtypeStruct((rows, WIDTH), BF16),
        jax.ShapeDtypeStruct((rows, WIDTH), BF16),
        jax.ShapeDtypeStruct((rows, LANES), BF16),
        jax.ShapeDtypeStruct((16, rows), F32),
        jax.ShapeDtypeStruct((rows, WIDTH), BF16),
        jax.ShapeDtypeStruct((rows, WIDTH), F32),
        jax.ShapeDtypeStruct((rows, WIDTH), BF16),
        jax.ShapeDtypeStruct((rows, WIDTH), BF16),
        jax.ShapeDtypeStruct((rows // LANES, WIDTH, LANES), BF16),
        jax.ShapeDtypeStruct((rows, WIDTH), BF16),
        jax.ShapeDtypeStruct((rows, WIDTH), BF16),
        jax.ShapeDtypeStruct((rows, WIDTH), BF16),
        jax.ShapeDtypeStruct((rows, N_BRANCH * d_model), BF16),
    ]
    out_specs = [row_spec(WIDTH), row_spec(WIDTH), t_spec, row_spec(WIDTH), row_spec(WIDTH), row_spec(LANES),
                 pl.BlockSpec((16, tm), lambda i: (0, i)),
                 row_spec(WIDTH), row_spec(WIDTH), row_spec(WIDTH), row_spec(WIDTH), t_spec, row_spec(WIDTH),
                 row_spec(WIDTH), row_spec(WIDTH), row_spec(N_BRANCH * d_model)]
    in_specs = [row_spec(d_model), _const_spec((1, d_model)), _const_spec(w_main.shape), _const_spec(wvt.shape),
                _const_spec(wwt.shape), _const_spec((n_slots, WIDTH)), _const_spec((1, WIDTH)),
                _const_spec((1, WIDTH)), _const_spec((1, WIDTH)), _const_spec((WIDTH, WIDTH)),
                _const_spec((WIDTH, WIDTH))]
    return pl.pallas_call(
        functools.partial(_proj_body, d_model=d_model),
        out_shape=out_shape, grid=(rows // tm,), in_specs=in_specs, out_specs=out_specs,
        compiler_params=pltpu.CompilerParams(dimension_semantics=("parallel",), vmem_limit_bytes=VMEM_LIMIT),
        name="proj",
    )(x2, norm_in.astype(F32)[None, :], w_main, wvt, wwt, lower_bounds.astype(F32),
      tile(q_norm_a, N_HEADS_A), tile(k_norm_a, N_HEADS_A), tile(q_norm_c, N_HEADS_C),
      _block_ones(WIDTH, HEAD_DIM_A), _block_ones(WIDTH, HEAD_DIM_C))


def _memkv_body(m_ref, nm_ref, w_ref, gk_ref, ones128_ref, kc_ref, vc_ref):
    x = m_ref[...]
    ms = jnp.mean(x * x, axis=-1, keepdims=True)
    hb = ((x * lax.rsqrt(ms + EPS)) * nm_ref[...]).astype(BF16)
    kc = _dot(hb, w_ref[:, 0:WIDTH])
    kc = kc * lax.rsqrt(_group_meansq(kc, ones128_ref, HEAD_DIM_C) + EPS) * gk_ref[...]
    kc_ref[...] = kc.astype(BF16)
    vc_ref[...] = _dot(hb, w_ref[:, WIDTH:2 * WIDTH]).astype(BF16)


def _memkv_call(mem2, norm_mem, w_mem_kv, k_norm_c, tm):
    rows, d_model = mem2.shape
    row_spec = lambda w: pl.BlockSpec((tm, w), lambda i: (i, 0))
    return pl.pallas_call(
        _memkv_body,
        out_shape=[jax.ShapeDtypeStruct((rows, WIDTH), BF16)] * 2,
        grid=(rows // tm,),
        in_specs=[row_spec(d_model), _const_spec((1, d_model)), _const_spec((d_model, 2 * WIDTH)),
                  _const_spec((1, WIDTH)), _const_spec((WIDTH, WIDTH))],
        out_specs=[row_spec(WIDTH), row_spec(WIDTH)],
        compiler_params=pltpu.CompilerParams(dimension_semantics=("parallel",), vmem_limit_bytes=VMEM_LIMIT),
        name="memkv",
    )(mem2, norm_mem.astype(F32)[None, :], w_mem_kv.astype(BF16),
      jnp.tile(k_norm_c.astype(F32), N_HEADS_C)[None, :], _block_ones(WIDTH, HEAD_DIM_C))


def _key_to_f32(k):
    return pltpu.bitcast(jnp.where(k < 0, k ^ jnp.int32(0x7FFFFFFF), k), F32)


def _dsa_body(qa_ref, qi_ref, wit_ref, sza_ref, ka_ref, vat_ref, ki2_ref, out_ref,
              sc_ref, qir_ref, qar_ref, m_ref, l_ref, acc_ref, ot_ref, *, topk, pos_bits):
    i = pl.program_id(1)
    q0 = i * TQ
    n_att = (q0 + TQ + TKS - 1) // TKS
    n_cnt = (q0 + TQ + TKC - 1) // TKC
    n_sc = n_cnt * (TKC // TKS)

    row = lax.broadcasted_iota(I32, (LANES, TQ), 0)
    row_lo = row < HEAD_DIM_A
    for p in range(N_HEADS_A // 2):
        cols = slice(p * LANES, (p + 1) * LANES)
        qit = qi_ref[:, cols].astype(F32).T
        qat = qa_ref[:, cols].astype(F32).T
        qir_ref[p] = jnp.concatenate([jnp.where(row_lo, qit, 0.0), jnp.where(row_lo, 0.0, qit)], axis=1).astype(BF16)
        qar_ref[p, 0:LANES, :] = jnp.concatenate(
            [jnp.where(row_lo, qat, 0.0), jnp.where(row_lo, 0.0, qat)], axis=1).astype(BF16)
        s0 = 2.0 ** (-8.0 * (2 * p + 1) / N_HEADS_A)
        s1 = 2.0 ** (-8.0 * (2 * p + 2) / N_HEADS_A)
        qar_ref[p, LANES:2 * LANES, :] = jnp.concatenate(
            [jnp.where(row == 0, s0, 0.0), jnp.where(row == 0, s1, 0.0)], axis=1).astype(BF16)
    m_ref[...] = jnp.full(m_ref.shape, NEG_BIG, F32)
    l_ref[...] = jnp.zeros(l_ref.shape, F32)
    acc_ref[...] = jnp.zeros(acc_ref.shape, F32)

    row_s = lax.broadcasted_iota(I32, (TKS, TQ), 0)
    tpos = q0 + lax.broadcasted_iota(I32, (TKS, TQ), 1)

    def score_tile(j, carry):
        r0 = pl.multiple_of(j * TKS, TKS)
        kt = ki2_ref[pl.ds(r0, TKS), :]
        acc = jnp.zeros((TKS, TQ), F32)
        for p in range(N_IDX_HEADS // 2):
            rel = jnp.maximum(_dot(kt, qir_ref[p]), 0.0)
            acc = acc + rel[:, :TQ] * wit_ref[2 * p:2 * p + 1, :] + rel[:, TQ:] * wit_ref[2 * p + 1:2 * p + 2, :]
        sc = acc * IDX_SCALE
        sc = jnp.where(sc == 0.0, 0.0, sc)
        sc_ref[pl.ds(r0, TKS), :] = jnp.where(r0 + row_s <= tpos, sc, -jnp.inf)
        return carry

    lax.fori_loop(0, n_sc, score_tile, 0)

    row_c = lax.broadcasted_iota(I32, (TKC, TQ), 0)

    def count(pred):
        def tile(j, acc):
            r0 = pl.multiple_of(j * TKC, TKC)
            c = jnp.where(pred(sc_ref[pl.ds(r0, TKC), :], r0), jnp.int32(1), jnp.int32(0))
            return acc + jnp.sum(c.reshape(TKC // 8, 8, TQ), axis=0)
        acc = lax.fori_loop(0, n_cnt, tile, jnp.zeros((8, TQ), I32))
        return jnp.sum(acc, axis=0, keepdims=True)

    def bisect(it, p):
        cand = p ^ lax.shift_left(jnp.int32(1), 31 - it)
        cand_f = _key_to_f32(cand)
        n = count(lambda s, r0: s >= cand_f)
        return jnp.where(n >= topk, cand, p)

    kth = lax.fori_loop(0, 32, bisect, jnp.full((1, TQ), INT_MIN, I32))
    kth = jnp.maximum(kth, jnp.int32(KEY_LOWEST))
    thr = _key_to_f32(kth)
    nxt = _key_to_f32(kth + 1)
    n_gt = count(lambda s, r0: s >= nxt)
    n_ge = count(lambda s, r0: s >= thr)
    need = topk - n_gt
    has_tie = n_ge > topk

    def tie_cut():
        def step(it, jp):
            cand = jp | lax.shift_left(jnp.int32(1), pos_bits - 1 - it)
            n = count(lambda s, r0: (s >= thr) & (s < nxt) & (r0 + row_c < cand))
            return jnp.where(n < need, cand, jp)
        return lax.fori_loop(0, pos_bits, step, jnp.zeros((1, TQ), I32))

    cut = lax.cond(jnp.max(jnp.where(has_tie, 1, 0)) > 0, tie_cut, lambda: jnp.zeros((1, TQ), I32))
    cut = jnp.where(has_tie, cut, jnp.int32(2 ** 30))

    lane_a = lax.broadcasted_iota(I32, (TKS, LANES), 1)
    aug = jnp.where(lane_a == 0, lax.broadcasted_iota(I32, (TKS, LANES), 0).astype(F32), 0.0).astype(BF16)
    half2 = lax.broadcasted_iota(I32, (1, 2 * TQ), 1) < TQ

    def att_tile(j, carry):
        r0 = pl.multiple_of(j * TKS, TKS)
        s_idx = sc_ref[pl.ds(r0, TKS), :]
        sel = (s_idx >= thr) & ((s_idx >= nxt) | (r0 + row_s <= cut))
        bias = jnp.where(sel, 0.0, NEG_BIG)
        bias2 = jnp.concatenate([bias, bias], axis=1)
        base = (r0 - q0).astype(F32)
        pairs = range(N_HEADS_A // 2)
        ss = [_dot(jnp.concatenate([ka_ref[pl.ds(r0, TKS), p * LANES:(p + 1) * LANES], aug], axis=1), qar_ref[p])
              for p in pairs]
        prs, alphas = [], []
        for p in pairs:
            s0 = 2.0 ** (-8.0 * (2 * p + 1) / N_HEADS_A)
            s1 = 2.0 ** (-8.0 * (2 * p + 2) / N_HEADS_A)
            shift = jnp.where(half2, s0, s1) * base
            s = ss[p] + bias2
            m_old = m_ref[p]
            m_new = jnp.maximum(m_old, jnp.max(s, axis=0, keepdims=True) + shift)
            alpha = jnp.exp(m_old - m_new)
            pr = jnp.exp(s - (m_new - shift))
            l_ref[p] = alpha * l_ref[p] + jnp.sum(pr, axis=0, keepdims=True)
            m_ref[p] = m_new
            prs.append(pr.astype(BF16))
            alphas.append(alpha)
        for p in pairs:
            vt = jnp.concatenate([vat_ref[j * (TKS // LANES) + t, p * LANES:(p + 1) * LANES, :]
                                  for t in range(TKS // LANES)], axis=1)
            acc_ref[p] = alphas[p] * acc_ref[p] + _dot(vt, prs[p])
        return carry

    lax.fori_loop(0, n_att, att_tile, 0)

    row_lo = lax.broadcasted_iota(I32, (LANES, TQ), 0) < HEAD_DIM_A
    for p in range(N_HEADS_A // 2):
        o = acc_ref[p] * (1.0 / l_ref[p])
        ot_ref[p * LANES:(p + 1) * LANES, :] = jnp.where(row_lo, o[:, :TQ], o[:, TQ:])
    out_ref[...] = (ot_ref[...].T * sza_ref[...].astype(F32)).astype(BF16)


def _dsa_call(qa, qi, wit, sza, ka, vat, ki2, batch, seq):
    assert TQ == LANES and TKS % LANES == 0 and TKC % TKS == 0 and seq % TKC == 0
    nq = seq // TQ
    topk = min(TOPK_MAX, seq // 4)
    pos_bits = max(1, int(np.ceil(np.log2(seq))))
    q_spec = lambda w: pl.BlockSpec((TQ, w), lambda b, i: (b * nq + i, 0))
    b_spec = lambda w: pl.BlockSpec((seq, w), lambda b, i: (b, 0))
    n_pair = N_HEADS_A // 2
    return pl.pallas_call(
        functools.partial(_dsa_body, topk=topk, pos_bits=pos_bits),
        out_shape=jax.ShapeDtypeStruct((batch * seq, WIDTH), BF16),
        grid=(batch, nq),
        in_specs=[q_spec(WIDTH), q_spec(WIDTH), pl.BlockSpec((16, TQ), lambda b, i: (0, b * nq + i)), q_spec(WIDTH),
                  b_spec(WIDTH), pl.BlockSpec((seq // LANES, WIDTH, LANES), lambda b, i: (b, 0, 0)), b_spec(LANES)],
        out_specs=q_spec(WIDTH),
        scratch_shapes=[pltpu.VMEM((seq, TQ), F32),
                        pltpu.VMEM((n_pair, LANES, 2 * TQ), BF16), pltpu.VMEM((n_pair, 2 * LANES, 2 * TQ), BF16),
                        pltpu.VMEM((n_pair, 1, 2 * TQ), F32), pltpu.VMEM((n_pair, 1, 2 * TQ), F32),
                        pltpu.VMEM((n_pair, LANES, 2 * TQ), F32), pltpu.VMEM((WIDTH, TQ), F32)],
        compiler_params=pltpu.CompilerParams(dimension_semantics=("parallel", "arbitrary"),
                                             vmem_limit_bytes=VMEM_LIMIT),
        name="dsa",
    )(qa, qi, wit, sza, ka, vat, ki2)


def _split3(v):
    hi = v.astype(BF16)
    r = v - hi.astype(F32)
    mid = r.astype(BF16)
    lo = (r - mid.astype(F32)).astype(BF16)
    return hi, mid, lo


def _hgrn_body(sqb_ref, logf_ref, kk_ref, ib_ref, ibt_ref, sgb_ref, gon_ref, out_ref, st_ref):
    @pl.when(pl.program_id(1) == 0)
    def _():
        st_ref[...] = jnp.zeros_like(st_ref)

    r_i = lax.broadcasted_iota(I32, (CHUNK, CHUNK), 0)
    c_i = lax.broadcasted_iota(I32, (CHUNK, CHUNK), 1)
    tril = c_i <= r_i
    tri = jnp.where(tril, 1.0, 0.0).astype(BF16)
    n_sub = CHUNK // SUB
    row_c = lax.broadcasted_iota(I32, (CHUNK, LANES), 0)
    row_t = lax.broadcasted_iota(I32, (T_HGRN, LANES), 0)

    for c in range(T_HGRN // CHUNK):
        rows = slice(c * CHUNK, (c + 1) * CHUNK)
        g = logf_ref[rows, :]
        b = sum(_dot(tri, part) for part in _split3(g))
        q = sqb_ref[rows, :].astype(F32)
        kk = kk_ref[rows, :].astype(F32)
        b_last = b[CHUNK - 1:CHUNK, :]
        ref_k = jnp.concatenate(
            [jnp.broadcast_to(b[(j + 1) * SUB - 1:(j + 1) * SUB, :], (SUB, WIDTH)) for j in range(n_sub)], axis=0)
        k_in = kk * jnp.exp(ref_k - b)
        q_out = (q * jnp.exp(b)).astype(BF16)
        k_st = (kk * jnp.exp(b_last - b)).astype(BF16)
        q_in = []
        for j in range(n_sub):
            lo = j * SUB
            qj = q[lo:, :] * jnp.exp(b[lo:, :] - b[lo + SUB - 1:lo + SUB, :])
            if lo:
                qj = jnp.concatenate([jnp.zeros((lo, WIDTH), F32), qj], axis=0)
            q_in.append(qj.astype(BF16))
        for hd in range(N_HEADS_B):
            cols = slice(hd * HEAD_B, (hd + 1) * HEAD_B)
            qs = jnp.concatenate([qj[:, cols] for qj in q_in], axis=1)
            kh = k_in[:, cols]
            ks = jnp.concatenate(
                [jnp.where((row_c >= j * SUB) & (row_c < (j + 1) * SUB), kh, 0.0) for j in range(n_sub)],
                axis=1).astype(BF16)
            a = jnp.where(tril, _dot_nt(qs, ks), 0.0).astype(BF16)
            st = st_ref[hd]
            o = _dot(a, ib_ref[rows, cols]) + _dot_nt(q_out[:, cols], st.astype(BF16))
            k_pad = jnp.where((row_t >= c * CHUNK) & (row_t < (c + 1) * CHUNK),
                              jnp.concatenate([k_st[:, cols]] * (T_HGRN // CHUNK), axis=0), jnp.zeros((), BF16))
            st_ref[hd] = st * jnp.exp(b_last[:, cols]) + _dot(ibt_ref[0, cols, :], k_pad)
            ms = jnp.mean(o * o, axis=-1, keepdims=True)
            o = o * lax.rsqrt(ms + EPS) * gon_ref[...]
            out_ref[rows, cols] = (o * sgb_ref[rows, cols].astype(F32)).astype(BF16)


def _hgrn_call(sqb, logf, kk, ib, ibt, sgb, o_norm_b, batch, seq):
    assert T_HGRN == LANES and seq % T_HGRN == 0
    nt = seq // T_HGRN
    spec = pl.BlockSpec((T_HGRN, WIDTH), lambda b, t: (b * nt + t, 0))
    ibt_spec = pl.BlockSpec((1, WIDTH, T_HGRN), lambda b, t: (b * nt + t, 0, 0))
    return pl.pallas_call(
        _hgrn_body,
        out_shape=jax.ShapeDtypeStruct((batch * seq, WIDTH), BF16),
        grid=(batch, nt),
        in_specs=[spec, spec, spec, spec, ibt_spec, spec, _const_spec((1, HEAD_B))],
        out_specs=spec,
        scratch_shapes=[pltpu.VMEM((N_HEADS_B, HEAD_B, HEAD_B), F32)],
        compiler_params=pltpu.CompilerParams(dimension_semantics=("parallel", "arbitrary"),
                                             vmem_limit_bytes=VMEM_LIMIT),
        name="hgrn",
    )(sqb, logf, kk, ib, ibt, sgb, o_norm_b.astype(F32)[None, :])


def _merge_body(x_ref, oa_ref, ob_ref, qc_ref, szc_ref, gates_ref, kc_ref, vc_ref, wb_ref, wo_ref, out_ref, *, d_model):
    oc = []
    for hd in range(N_HEADS_C):
        cols = slice(hd * HEAD_DIM_C, (hd + 1) * HEAD_DIM_C)
        s = _dot_nt(qc_ref[:, cols], kc_ref[:, cols])
        m = jnp.max(s, axis=-1, keepdims=True)
        p = jnp.exp(s - m)
        l = jnp.sum(p, axis=-1, keepdims=True)
        oc.append(_dot(p.astype(BF16), vc_ref[:, cols]) * (1.0 / l))
    oc = (jnp.concatenate(oc, axis=1) * szc_ref[...].astype(F32)).astype(BF16)
    merged = jnp.zeros((x_ref.shape[0], d_model), F32)
    for n, br in enumerate((oa_ref[...], ob_ref[...], oc)):
        y = _dot(br, wb_ref[n])
        merged = merged + gates_ref[:, n * d_model:(n + 1) * d_model].astype(F32) * y
    out_ref[...] = x_ref[...] + _dot(merged.astype(BF16), wo_ref[...])


def _merge_call(x2, oa, ob, qc, szc, gates, kc, vc, w_branch, w_out, batch, seq, mem_len):
    rows, d_model = x2.shape
    tm = TM_MERGE
    nt = seq // tm
    row_spec = lambda w: pl.BlockSpec((tm, w), lambda b, t: (b * nt + t, 0))
    mem_spec = pl.BlockSpec((mem_len, WIDTH), lambda b, t: (b, 0))
    return pl.pallas_call(
        functools.partial(_merge_body, d_model=d_model),
        out_shape=jax.ShapeDtypeStruct((rows, d_model), F32),
        grid=(batch, nt),
        in_specs=[row_spec(d_model), row_spec(WIDTH), row_spec(WIDTH), row_spec(WIDTH), row_spec(WIDTH),
                  row_spec(N_BRANCH * d_model), mem_spec, mem_spec,
                  _const_spec((N_BRANCH, WIDTH, d_model)), _const_spec((d_model, d_model))],
        out_specs=row_spec(d_model),
        compiler_params=pltpu.CompilerParams(dimension_semantics=("parallel", "parallel"),
                                             vmem_limit_bytes=VMEM_LIMIT),
        name="merge",
    )(x2, oa, ob, qc, szc, gates, kc, vc, w_branch.astype(BF16), w_out.astype(BF16))


def _layer(x, mem, norm_in, norm_mem, w_in, q_norm_a, k_norm_a, lower_bounds, o_norm_b, w_mem_kv, q_norm_c, k_norm_c,
           w_branch, w_out):
    batch, seq, d_model = x.shape
    mem_len = mem.shape[1]
    x2 = x.reshape(batch * seq, d_model)
    (qa, ka, vat, sza, qi, ki2, wit, sqb, logf, kk, ib, ibt, sgb, qc, szc, gates) = _proj_call(
        x2, norm_in, w_in, lower_bounds, q_norm_a, k_norm_a, q_norm_c)
    kc, vc = _memkv_call(mem.reshape(batch * mem_len, d_model), norm_mem, w_mem_kv, k_norm_c, mem_len)
    oa = _dsa_call(qa, qi, wit, sza, ka, vat, ki2, batch, seq)
    ob = _hgrn_call(sqb, logf, kk, ib, ibt, sgb, o_norm_b, batch, seq)
    out = _merge_call(x2, oa, ob, qc, szc, gates, kc, vc, w_branch, w_out, batch, seq, mem_len)
    return out.reshape(batch, seq, d_model)


def kernel(x, mem, norm_in, norm_mem, w_in, q_norm_a, k_norm_a, lower_bounds, o_norm_b, w_mem_kv, q_norm_c, k_norm_c,
           w_branch, w_out):
    assert norm_in.shape[0] == 1, "single-layer block"
    return _layer(x, mem, norm_in[0], norm_mem[0], w_in[0], q_norm_a[0], k_norm_a[0], lower_bounds, o_norm_b[0],
                  w_mem_kv[0], q_norm_c[0], k_norm_c[0], w_branch[0], w_out[0])
```

```python
import functools

import jax
import jax.numpy as jnp
import numpy as np
from jax import lax
from jax.experimental import pallas as pl
from jax.experimental.pallas import tpu as pltpu

F32 = jnp.float32
BF16 = jnp.bfloat16
I32 = jnp.int32

N_HEADS_A = 8
HEAD_DIM_A = 64
N_IDX_HEADS = 8
IDX_DIM = 64
TOPK_MAX = 256
N_HEADS_B = 4
HEAD_B = 128
N_HEADS_C = 4
HEAD_DIM_C = 128
N_BRANCH = 3
EPS = 1e-6
WIDTH = 512
IDX_SCALE = (IDX_DIM ** -0.5) * (N_IDX_HEADS ** -0.5)

LANES = 128
VMEM_LIMIT = 56 * 1024 * 1024

TM_PROJ = 256
TQ = 128
TKS = 256
TKC = 512
CHUNK = 64
SUB = 16
T_HGRN = 128
TM_MERGE = 256

NEG_BIG = -1e30
LOG2E = 1.4426950408889634
INT_MIN = -2147483648
KEY_LOWEST = -2139095040

NT = (((1,), (1,)), ((), ()))


def _dot(a, b):
    return jnp.dot(a, b, preferred_element_type=F32)


def _dot_nt(a, b):
    return lax.dot_general(a, b, NT, preferred_element_type=F32)


def _group_meansq(v, ones_ref, group):
    sq = v * v
    hi = sq.astype(BF16)
    lo = (sq - hi.astype(F32)).astype(BF16)
    s = _dot(hi, ones_ref[...]) + _dot(lo, ones_ref[...])
    return s * (1.0 / group)


_C_QA, _C_KA, _C_ZA, _C_QB, _C_FB, _C_IB, _C_GB, _C_QC, _C_ZC = [WIDTH * i for i in range(9)]
_C_GATES = 9 * WIDTH
_C_QI = _C_GATES + 3 * 1024
_C_KI = _C_QI + WIDTH
_C_END = _C_KI + LANES


def _proj_body(x_ref, nin_ref, w_ref, wvt_ref, wwt_ref, lbp_ref, gqa_ref, gka_ref, gqc_ref, ones64_ref, ones128_ref,
               qa_ref, ka_ref, vat_ref, sza_ref, qi_ref, ki2_ref, wit_ref,
               sqb_ref, logf_ref, kk_ref, ib_ref, ibt_ref, sgb_ref, qc_ref, szc_ref, gates_ref, *, d_model):
    x = x_ref[...]
    ms = jnp.mean(x * x, axis=-1, keepdims=True)
    h = (x * lax.rsqrt(ms + EPS)) * nin_ref[...]
    hb = h.astype(BF16)

    def proj(c0, width):
        return _dot(hb, w_ref[:, c0:c0 + width])

    qa = proj(_C_QA, WIDTH)
    qa = qa * lax.rsqrt(_group_meansq(qa, ones64_ref, HEAD_DIM_A) + EPS) * gqa_ref[...]
    qa_ref[...] = (qa * (HEAD_DIM_A ** -0.5 * LOG2E)).astype(BF16)
    ka = proj(_C_KA, WIDTH)
    ka = ka * lax.rsqrt(_group_meansq(ka, ones64_ref, HEAD_DIM_A) + EPS) * gka_ref[...]
    ka_ref[...] = ka.astype(BF16)
    vat = _dot_nt(wvt_ref[0:WIDTH, :], hb).astype(BF16)
    for t in range(vat_ref.shape[0]):
        vat_ref[t] = vat[:, t * LANES:(t + 1) * LANES]
    sza_ref[...] = jax.nn.silu(proj(_C_ZA, WIDTH)).astype(BF16)
    qi_ref[...] = proj(_C_QI, WIDTH).astype(BF16)
    ki2_ref[...] = proj(_C_KI, LANES).astype(BF16)
    wit_ref[...] = _dot_nt(wwt_ref[...], hb)
    sqb_ref[...] = jax.nn.silu(proj(_C_QB, WIDTH)).astype(BF16)
    lbp = lbp_ref[...]
    lbe = jnp.exp(lbp - jnp.max(lbp, axis=0, keepdims=True))
    lb = lbe[0:1, :] / jnp.sum(lbe, axis=0, keepdims=True)
    f = lb + (1.0 - lb) * jax.nn.sigmoid(proj(_C_FB, WIDTH))
    logf_ref[...] = jnp.log(f)
    kk_ref[...] = (1.0 - f).astype(BF16)
    ib_ref[...] = proj(_C_IB, WIDTH).astype(BF16)
    ibt = _dot_nt(wvt_ref[WIDTH:2 * WIDTH, :], hb).astype(BF16)
    for t in range(ibt_ref.shape[0]):
        ibt_ref[t] = ibt[:, t * LANES:(t + 1) * LANES]
    sgb_ref[...] = jax.nn.silu(proj(_C_GB, WIDTH)).astype(BF16)
    qc = proj(_C_QC, WIDTH)
    qc = qc * lax.rsqrt(_group_meansq(qc, ones128_ref, HEAD_DIM_C) + EPS) * gqc_ref[...]
    qc_ref[...] = (qc * (HEAD_DIM_C ** -0.5)).astype(BF16)
    szc_ref[...] = jax.nn.silu(proj(_C_ZC, WIDTH)).astype(BF16)
    for n in range(N_BRANCH):
        gates_ref[:, n * d_model:(n + 1) * d_model] = jax.nn.sigmoid(
            proj(_C_GATES + n * d_model, d_model)).astype(BF16)


def _const_spec(shape):
    nd = len(shape)
    return pl.BlockSpec(shape, lambda *_: (0,) * nd, pipeline_mode=pl.Buffered(1))


def _block_ones(width, group):
    g = np.arange(width) // group
    return jnp.asarray((g[:, None] == g[None, :]).astype(np.float32), dtype=BF16)


def _proj_call(x2, norm_in, w_in, lower_bounds, q_norm_a, k_norm_a, q_norm_c):
    rows, d_model = x2.shape
    tm = TM_PROJ
    assert rows % tm == 0 and d_model == 1024
    offs = np.cumsum([0, 512, 512, 512, 512, 512, 64, 8, 512, 512, 512, 512, 512, 512, 3 * d_model])
    (o_qa, o_ka, o_va, o_za, o_qi, o_ki, o_wi, o_qb, o_fb, o_ib, o_gb, o_qc, o_zc, o_g, o_end) = [int(o) for o in offs]
    assert o_end == w_in.shape[1]
    wb = w_in.astype(BF16)
    col = lambda a, b: wb[:, a:b]
    w_main = jnp.concatenate([
        col(o_qa, o_ka), col(o_ka, o_va), col(o_za, o_qi), col(o_qb, o_fb), col(o_fb, o_ib), col(o_ib, o_gb),
        col(o_gb, o_qc), col(o_qc, o_zc), col(o_zc, o_g), col(o_g, o_end), col(o_qi, o_ki),
        col(o_ki, o_wi), col(o_ki, o_wi)], axis=1)
    assert w_main.shape[1] == _C_END
    wvt = jnp.concatenate([col(o_va, o_za).T, col(o_ib, o_gb).T], axis=0)
    wwt = jnp.concatenate([col(o_wi, o_qb).T, jnp.zeros((8, d_model), BF16)], axis=0)
    tile = lambda g, reps: jnp.tile(g.astype(F32), reps)[None, :]
    n_slots = lower_bounds.shape[0]

    row_spec = lambda w: pl.BlockSpec((tm, w), lambda i: (i, 0))
    t_spec = pl.BlockSpec((tm // LANES, WIDTH, LANES), lambda i: (i, 0, 0))
    out_shape = [
        jax.ShapeDtypeStruct((rows, WIDTH), BF16),
        jax.ShapeDtypeStruct((rows, WIDTH), BF16),
        jax.ShapeDtypeStruct((rows // LANES, WIDTH, LANES), BF16),
        jax.ShapeDtypeStruct((rows, WIDTH), BF16),
        jax.ShapeDtypeStruct((rows, WIDTH), BF16),
        jax.ShapeDtypeStruct((rows, LANES), BF16),
        jax.ShapeDtypeStruct((16, rows), F32),
        jax.ShapeDtypeStruct((rows, WIDTH), BF16),
        jax.ShapeDtypeStruct((rows, WIDTH), F32),
        jax.ShapeDtypeStruct((rows, WIDTH), BF16),
        jax.ShapeDtypeStruct((rows, WIDTH), BF16),
        jax.ShapeDtypeStruct((rows // LANES, WIDTH, LANES), BF16),
        jax.ShapeDtypeStruct((rows, WIDTH), BF16),
        jax.ShapeDtypeStruct((rows, WIDTH), BF16),
        jax.ShapeDtypeStruct((rows, WIDTH), BF16),
        jax.ShapeDtypeStruct((rows, N_BRANCH * d_model), BF16),
    ]
    out_specs = [row_spec(WIDTH), row_spec(WIDTH), t_spec, row_spec(WIDTH), row_spec(WIDTH), row_spec(LANES),
                 pl.BlockSpec((16, tm), lambda i: (0, i)),
                 row_spec(WIDTH), row_spec(WIDTH), row_spec(WIDTH), row_spec(WIDTH), t_spec, row_spec(WIDTH),
                 row_spec(WIDTH), row_spec(WIDTH), row_spec(N_BRANCH * d_model)]
    in_specs = [row_spec(d_model), _const_spec((1, d_model)), _const_spec(w_main.shape), _const_spec(wvt.shape),
                _const_spec(wwt.shape), _const_spec((n_slots, WIDTH)), _const_spec((1, WIDTH)),
                _const_spec((1, WIDTH)), _const_spec((1, WIDTH)), _const_spec((WIDTH, WIDTH)),
                _const_spec((WIDTH, WIDTH))]
    return pl.pallas_call(
        functools.partial(_proj_body, d_model=d_model),
        out_shape=out_shape, grid=(rows // tm,), in_specs=in_specs, out_specs=out_specs,
        compiler_params=pltpu.CompilerParams(dimension_semantics=("parallel",), vmem_limit_bytes=VMEM_LIMIT),
        name="proj",
    )(x2, norm_in.astype(F32)[None, :], w_main, wvt, wwt, lower_bounds.astype(F32),
      tile(q_norm_a, N_HEADS_A), tile(k_norm_a, N_HEADS_A), tile(q_norm_c, N_HEADS_C),
      _block_ones(WIDTH, HEAD_DIM_A), _block_ones(WIDTH, HEAD_DIM_C))


def _memkv_body(m_ref, nm_ref, w_ref, gk_ref, ones128_ref, kc_ref, vc_ref):
    x = m_ref[...]
    ms = jnp.mean(x * x, axis=-1, keepdims=True)
    hb = ((x * lax.rsqrt(ms + EPS)) * nm_ref[...]).astype(BF16)
    kc = _dot(hb, w_ref[:, 0:WIDTH])
    kc = kc * lax.rsqrt(_group_meansq(kc, ones128_ref, HEAD_DIM_C) + EPS) * gk_ref[...]
    kc_ref[...] = kc.astype(BF16)
    vc_ref[...] = _dot(hb, w_ref[:, WIDTH:2 * WIDTH]).astype(BF16)


def _memkv_call(mem2, norm_mem, w_mem_kv, k_norm_c, tm):
    rows, d_model = mem2.shape
    row_spec = lambda w: pl.BlockSpec((tm, w), lambda i: (i, 0))
    return pl.pallas_call(
        _memkv_body,
        out_shape=[jax.ShapeDtypeStruct((rows, WIDTH), BF16)] * 2,
        grid=(rows // tm,),
        in_specs=[row_spec(d_model), _const_spec((1, d_model)), _const_spec((d_model, 2 * WIDTH)),
                  _const_spec((1, WIDTH)), _const_spec((WIDTH, WIDTH))],
        out_specs=[row_spec(WIDTH), row_spec(WIDTH)],
        compiler_params=pltpu.CompilerParams(dimension_semantics=("parallel",), vmem_limit_bytes=VMEM_LIMIT),
        name="memkv",
    )(mem2, norm_mem.astype(F32)[None, :], w_mem_kv.astype(BF16),
      jnp.tile(k_norm_c.astype(F32), N_HEADS_C)[None, :], _block_ones(WIDTH, HEAD_DIM_C))


def _key_to_f32(k):
    return pltpu.bitcast(jnp.where(k < 0, k ^ jnp.int32(0x7FFFFFFF), k), F32)


def _slope_log2(h):
    return (2.0 ** (-8.0 * (h + 1) / N_HEADS_A)) * LOG2E


N_SLOPE_PARTS = 3


def _slope_rows(h, row):
    rest = np.float32(_slope_log2(h))
    out = jnp.zeros(row.shape, F32)
    for r in range(N_SLOPE_PARTS):
        part = np.float32(np.asarray(rest, dtype=jnp.bfloat16))
        out = jnp.where(row == r, float(part), out)
        rest = np.float32(rest - part)
    return out


def _dsa_body(qa_ref, qi_ref, wit_ref, sza_ref, ka_ref, vat_ref, ki2_ref, out_ref,
              sc_ref, qir_ref, qar_ref, m_ref, l_ref, alpha_ref, acc_ref, pbuf_ref, sbuf_ref, ot_ref,
              *, topk, pos_bits):
    i = pl.program_id(1)
    q0 = i * TQ
    n_att = (q0 + TQ + TKS - 1) // TKS
    n_cnt = (q0 + TQ + TKC - 1) // TKC
    n_sc = n_cnt * (TKC // TKS)

    row = lax.broadcasted_iota(I32, (LANES, TQ), 0)
    row_lo = row < HEAD_DIM_A
    for p in range(N_HEADS_A // 2):
        cols = slice(p * LANES, (p + 1) * LANES)
        qit = qi_ref[:, cols].astype(F32).T
        qat = qa_ref[:, cols].astype(F32).T
        qir_ref[p] = jnp.concatenate([jnp.where(row_lo, qit, 0.0), jnp.where(row_lo, 0.0, qit)], axis=1).astype(BF16)
        qar_ref[p, 0:LANES, :] = jnp.concatenate(
            [jnp.where(row_lo, qat, 0.0), jnp.where(row_lo, 0.0, qat)], axis=1).astype(BF16)
        qar_ref[p, LANES:2 * LANES, :] = jnp.concatenate(
            [_slope_rows(2 * p, row), _slope_rows(2 * p + 1, row)], axis=1).astype(BF16)
    m_ref[...] = jnp.full(m_ref.shape, NEG_BIG, F32)
    l_ref[...] = jnp.zeros(l_ref.shape, F32)
    acc_ref[...] = jnp.zeros(acc_ref.shape, F32)
    alpha_ref[...] = jnp.ones(alpha_ref.shape, F32)
    pbuf_ref[...] = jnp.zeros(pbuf_ref.shape, BF16)

    row_s = lax.broadcasted_iota(I32, (TKS, TQ), 0)
    tpos = q0 + lax.broadcasted_iota(I32, (TKS, TQ), 1)

    def score_tile(j, carry):
        r0 = pl.multiple_of(j * TKS, TKS)
        kt = ki2_ref[pl.ds(r0, TKS), :]
        acc = jnp.zeros((TKS, TQ), F32)
        for p in range(N_IDX_HEADS // 2):
            rel = jnp.maximum(_dot(kt, qir_ref[p]), 0.0)
            acc = acc + rel[:, :TQ] * wit_ref[2 * p:2 * p + 1, :] + rel[:, TQ:] * wit_ref[2 * p + 1:2 * p + 2, :]
        sc = acc * IDX_SCALE
        sc = jnp.where(sc == 0.0, 0.0, sc)
        sc_ref[pl.ds(r0, TKS), :] = jnp.where(r0 + row_s <= tpos, sc, -jnp.inf)
        return carry

    lax.fori_loop(0, n_sc, score_tile, 0)

    row_c = lax.broadcasted_iota(I32, (TKC, TQ), 0)

    def count(pred):
        def tile(j, acc):
            r0 = pl.multiple_of(j * TKC, TKC)
            c = jnp.where(pred(sc_ref[pl.ds(r0, TKC), :], r0), jnp.int32(1), jnp.int32(0))
            return acc + jnp.sum(c.reshape(TKC // 8, 8, TQ), axis=0)
        acc = lax.fori_loop(0, n_cnt, tile, jnp.zeros((8, TQ), I32))
        return jnp.sum(acc, axis=0, keepdims=True)

    def bisect(it, p):
        cand = p ^ lax.shift_left(jnp.int32(1), 31 - it)
        cand_f = _key_to_f32(cand)
        n = count(lambda s, r0: s >= cand_f)
        return jnp.where(n >= topk, cand, p)

    kth = lax.fori_loop(0, 32, bisect, jnp.full((1, TQ), INT_MIN, I32))
    kth = jnp.maximum(kth, jnp.int32(KEY_LOWEST))
    thr = _key_to_f32(kth)
    nxt = _key_to_f32(kth + 1)
    n_gt = count(lambda s, r0: s >= nxt)
    n_ge = count(lambda s, r0: s >= thr)
    need = topk - n_gt
    has_tie = n_ge > topk

    def tie_cut():
        def step(it, jp):
            cand = jp | lax.shift_left(jnp.int32(1), pos_bits - 1 - it)
            n = count(lambda s, r0: (s >= thr) & (s < nxt) & (r0 + row_c < cand))
            return jnp.where(n < need, cand, jp)
        return lax.fori_loop(0, pos_bits, step, jnp.zeros((1, TQ), I32))

    cut = lax.cond(jnp.max(jnp.where(has_tie, 1, 0)) > 0, tie_cut, lambda: jnp.zeros((1, TQ), I32))
    cut = jnp.where(has_tie, cut, jnp.int32(2 ** 30))

    lane_a = lax.broadcasted_iota(I32, (TKS, LANES), 1)
    aug = jnp.where(lane_a < N_SLOPE_PARTS, lax.broadcasted_iota(I32, (TKS, LANES), 0).astype(F32), 0.0).astype(BF16)
    half2 = lax.broadcasted_iota(I32, (1, 2 * TQ), 1) < TQ

    pairs = range(N_HEADS_A // 2)

    def value_update(jt):
        for p in pairs:
            vt = jnp.concatenate([vat_ref[jt * (TKS // LANES) + t, p * LANES:(p + 1) * LANES, :]
                                  for t in range(TKS // LANES)], axis=1)
            acc_ref[p] = alpha_ref[p] * acc_ref[p] + _dot(vt, pbuf_ref[p])

    def score_dots(jt):
        r0 = pl.multiple_of(jt * TKS, TKS)
        return [_dot(jnp.concatenate([ka_ref[pl.ds(r0, TKS), p * LANES:(p + 1) * LANES], aug], axis=1), qar_ref[p])
                for p in pairs]

    for p, s in zip(pairs, score_dots(0)):
        sbuf_ref[p] = s

    def att_tile(j, carry):
        r0 = pl.multiple_of(j * TKS, TKS)
        s_idx = sc_ref[pl.ds(r0, TKS), :]
        sel = (s_idx >= thr) & ((s_idx >= nxt) | (r0 + row_s <= cut))
        bias = jnp.where(sel, 0.0, NEG_BIG)
        bias2 = jnp.concatenate([bias, bias], axis=1)
        base = (r0 - q0).astype(F32)
        value_update(jnp.maximum(j - 1, 0))
        ahead = score_dots(jnp.minimum(j + 1, n_att - 1))
        for p in pairs:
            shift = jnp.where(half2, _slope_log2(2 * p), _slope_log2(2 * p + 1)) * base
            s = sbuf_ref[p] + bias2
            m_old = m_ref[p]
            m_new = jnp.maximum(m_old, jnp.max(s, axis=0, keepdims=True) + shift)
            alpha = jnp.exp2(m_old - m_new)
            pr = jnp.exp2(s - (m_new - shift))
            l_ref[p] = alpha * l_ref[p] + jnp.sum(pr, axis=0, keepdims=True)
            m_ref[p] = m_new
            alpha_ref[p] = alpha
            pbuf_ref[p] = pr.astype(BF16)
        for p in pairs:
            sbuf_ref[p] = ahead[p]
        return carry

    lax.fori_loop(0, n_att, att_tile, 0)
    value_update(n_att - 1)

    row_lo = lax.broadcasted_iota(I32, (LANES, TQ), 0) < HEAD_DIM_A
    for p in range(N_HEADS_A // 2):
        o = acc_ref[p] * (1.0 / l_ref[p])
        ot_ref[p * LANES:(p + 1) * LANES, :] = jnp.where(row_lo, o[:, :TQ], o[:, TQ:])
    out_ref[...] = (ot_ref[...].T * sza_ref[...].astype(F32)).astype(BF16)


def _dsa_call(qa, qi, wit, sza, ka, vat, ki2, batch, seq):
    assert TQ == LANES and TKS % LANES == 0 and TKC % TKS == 0 and seq % TKC == 0
    nq = seq // TQ
    topk = min(TOPK_MAX, seq // 4)
    pos_bits = max(1, int(np.ceil(np.log2(seq))))
    q_spec = lambda w: pl.BlockSpec((TQ, w), lambda b, i: (b * nq + i, 0))
    b_spec = lambda w: pl.BlockSpec((seq, w), lambda b, i: (b, 0))
    n_pair = N_HEADS_A // 2
    return pl.pallas_call(
        functools.partial(_dsa_body, topk=topk, pos_bits=pos_bits),
        out_shape=jax.ShapeDtypeStruct((batch * seq, WIDTH), BF16),
        grid=(batch, nq),
        in_specs=[q_spec(WIDTH), q_spec(WIDTH), pl.BlockSpec((16, TQ), lambda b, i: (0, b * nq + i)), q_spec(WIDTH),
                  b_spec(WIDTH), pl.BlockSpec((seq // LANES, WIDTH, LANES), lambda b, i: (b, 0, 0)), b_spec(LANES)],
        out_specs=q_spec(WIDTH),
        scratch_shapes=[pltpu.VMEM((seq, TQ), F32),
                        pltpu.VMEM((n_pair, LANES, 2 * TQ), BF16), pltpu.VMEM((n_pair, 2 * LANES, 2 * TQ), BF16),
                        pltpu.VMEM((n_pair, 1, 2 * TQ), F32), pltpu.VMEM((n_pair, 1, 2 * TQ), F32),
                        pltpu.VMEM((n_pair, 1, 2 * TQ), F32), pltpu.VMEM((n_pair, LANES, 2 * TQ), F32),
                        pltpu.VMEM((n_pair, TKS, 2 * TQ), BF16), pltpu.VMEM((n_pair, TKS, 2 * TQ), F32),
                        pltpu.VMEM((WIDTH, TQ), F32)],
        compiler_params=pltpu.CompilerParams(dimension_semantics=("parallel", "arbitrary"),
                                             vmem_limit_bytes=VMEM_LIMIT),
        name="dsa",
    )(qa, qi, wit, sza, ka, vat, ki2)


def _split3(v):
    hi = v.astype(BF16)
    r = v - hi.astype(F32)
    mid = r.astype(BF16)
    lo = (r - mid.astype(F32)).astype(BF16)
    return hi, mid, lo


def _hgrn_body(sqb_ref, logf_ref, kk_ref, ib_ref, ibt_ref, sgb_ref, gon_ref, out_ref, st_ref):
    @pl.when(pl.program_id(1) == 0)
    def _():
        st_ref[...] = jnp.zeros_like(st_ref)

    r_i = lax.broadcasted_iota(I32, (CHUNK, CHUNK), 0)
    c_i = lax.broadcasted_iota(I32, (CHUNK, CHUNK), 1)
    tril = c_i <= r_i
    tri = jnp.where(tril, 1.0, 0.0).astype(BF16)
    n_sub = CHUNK // SUB
    row_c = lax.broadcasted_iota(I32, (CHUNK, LANES), 0)
    row_t = lax.broadcasted_iota(I32, (T_HGRN, LANES), 0)

    for c in range(T_HGRN // CHUNK):
        rows = slice(c * CHUNK, (c + 1) * CHUNK)
        g = logf_ref[rows, :]
        b = sum(_dot(tri, part) for part in _split3(g))
        q = sqb_ref[rows, :].astype(F32)
        kk = kk_ref[rows, :].astype(F32)
        b_last = b[CHUNK - 1:CHUNK, :]
        ref_k = jnp.concatenate(
            [jnp.broadcast_to(b[(j + 1) * SUB - 1:(j + 1) * SUB, :], (SUB, WIDTH)) for j in range(n_sub)], axis=0)
        k_in = kk * jnp.exp(ref_k - b)
        q_out = (q * jnp.exp(b)).astype(BF16)
        k_st = (kk * jnp.exp(b_last - b)).astype(BF16)
        q_in = []
        for j in range(n_sub):
            lo = j * SUB
            qj = q[lo:, :] * jnp.exp(b[lo:, :] - b[lo + SUB - 1:lo + SUB, :])
            if lo:
                qj = jnp.concatenate([jnp.zeros((lo, WIDTH), F32), qj], axis=0)
            q_in.append(qj.astype(BF16))
        for hd in range(N_HEADS_B):
            cols = slice(hd * HEAD_B, (hd + 1) * HEAD_B)
            qs = jnp.concatenate([qj[:, cols] for qj in q_in], axis=1)
            kh = k_in[:, cols]
            ks = jnp.concatenate(
                [jnp.where((row_c >= j * SUB) & (row_c < (j + 1) * SUB), kh, 0.0) for j in range(n_sub)],
                axis=1).astype(BF16)
            a = jnp.where(tril, _dot_nt(qs, ks), 0.0).astype(BF16)
            st = st_ref[hd]
            o = _dot(a, ib_ref[rows, cols]) + _dot_nt(q_out[:, cols], st.astype(BF16))
            k_pad = jnp.where((row_t >= c * CHUNK) & (row_t < (c + 1) * CHUNK),
                              jnp.concatenate([k_st[:, cols]] * (T_HGRN // CHUNK), axis=0), jnp.zeros((), BF16))
            st_ref[hd] = st * jnp.exp(b_last[:, cols]) + _dot(ibt_ref[0, cols, :], k_pad)
            ms = jnp.mean(o * o, axis=-1, keepdims=True)
            o = o * lax.rsqrt(ms + EPS) * gon_ref[...]
            out_ref[rows, cols] = (o * sgb_ref[rows, cols].astype(F32)).astype(BF16)


def _hgrn_call(sqb, logf, kk, ib, ibt, sgb, o_norm_b, batch, seq):
    assert T_HGRN == LANES and seq % T_HGRN == 0
    nt = seq // T_HGRN
    spec = pl.BlockSpec((T_HGRN, WIDTH), lambda b, t: (b * nt + t, 0))
    ibt_spec = pl.BlockSpec((1, WIDTH, T_HGRN), lambda b, t: (b * nt + t, 0, 0))
    return pl.pallas_call(
        _hgrn_body,
        out_shape=jax.ShapeDtypeStruct((batch * seq, WIDTH), BF16),
        grid=(batch, nt),
        in_specs=[spec, spec, spec, spec, ibt_spec, spec, _const_spec((1, HEAD_B))],
        out_specs=spec,
        scratch_shapes=[pltpu.VMEM((N_HEADS_B, HEAD_B, HEAD_B), F32)],
        compiler_params=pltpu.CompilerParams(dimension_semantics=("parallel", "arbitrary"),
                                             vmem_limit_bytes=VMEM_LIMIT),
        name="hgrn",
    )(sqb, logf, kk, ib, ibt, sgb, o_norm_b.astype(F32)[None, :])


def _merge_body(x_ref, oa_ref, ob_ref, qc_ref, szc_ref, gates_ref, kc_ref, vc_ref, wb_ref, wo_ref, out_ref, *, d_model):
    oc = []
    for hd in range(N_HEADS_C):
        cols = slice(hd * HEAD_DIM_C, (hd + 1) * HEAD_DIM_C)
        s = _dot_nt(qc_ref[:, cols], kc_ref[:, cols])
        m = jnp.max(s, axis=-1, keepdims=True)
        p = jnp.exp(s - m)
        l = jnp.sum(p, axis=-1, keepdims=True)
        oc.append(_dot(p.astype(BF16), vc_ref[:, cols]) * (1.0 / l))
    oc = (jnp.concatenate(oc, axis=1) * szc_ref[...].astype(F32)).astype(BF16)
    merged = jnp.zeros((x_ref.shape[0], d_model), F32)
    for n, br in enumerate((oa_ref[...], ob_ref[...], oc)):
        y = _dot(br, wb_ref[n])
        merged = merged + gates_ref[:, n * d_model:(n + 1) * d_model].astype(F32) * y
    out_ref[...] = x_ref[...] + _dot(merged.astype(BF16), wo_ref[...])


def _merge_call(x2, oa, ob, qc, szc, gates, kc, vc, w_branch, w_out, batch, seq, mem_len):
    rows, d_model = x2.shape
    tm = TM_MERGE
    nt = seq // tm
    row_spec = lambda w: pl.BlockSpec((tm, w), lambda b, t: (b * nt + t, 0))
    mem_spec = pl.BlockSpec((mem_len, WIDTH), lambda b, t: (b, 0))
    return pl.pallas_call(
        functools.partial(_merge_body, d_model=d_model),
        out_shape=jax.ShapeDtypeStruct((rows, d_model), F32),
        grid=(batch, nt),
        in_specs=[row_spec(d_model), row_spec(WIDTH), row_spec(WIDTH), row_spec(WIDTH), row_spec(WIDTH),
                  row_spec(N_BRANCH * d_model), mem_spec, mem_spec,
                  _const_spec((N_BRANCH, WIDTH, d_model)), _const_spec((d_model, d_model))],
        out_specs=row_spec(d_model),
        compiler_params=pltpu.CompilerParams(dimension_semantics=("parallel", "parallel"),
                                             vmem_limit_bytes=VMEM_LIMIT),
        name="merge",
    )(x2, oa, ob, qc, szc, gates, kc, vc, w_branch.astype(BF16), w_out.astype(BF16))


def _layer(x, mem, norm_in, norm_mem, w_in, q_norm_a, k_norm_a, lower_bounds, o_norm_b, w_mem_kv, q_norm_c, k_norm_c,
           w_branch, w_out):
    batch, seq, d_model = x.shape
    mem_len = mem.shape[1]
    x2 = x.reshape(batch * seq, d_model)
    (qa, ka, vat, sza, qi, ki2, wit, sqb, logf, kk, ib, ibt, sgb, qc, szc, gates) = _proj_call(
        x2, norm_in, w_in, lower_bounds, q_norm_a, k_norm_a, q_norm_c)
    kc, vc = _memkv_call(mem.reshape(batch * mem_len, d_model), norm_mem, w_mem_kv, k_norm_c, mem_len)
    oa = _dsa_call(qa, qi, wit, sza, ka, vat, ki2, batch, seq)
    ob = _hgrn_call(sqb, logf, kk, ib, ibt, sgb, o_norm_b, batch, seq)
    out = _merge_call(x2, oa, ob, qc, szc, gates, kc, vc, w_branch, w_out, batch, seq, mem_len)
    return out.reshape(batch, seq, d_model)


def kernel(x, mem, norm_in, norm_mem, w_in, q_norm_a, k_norm_a, lower_bounds, o_norm_b, w_mem_kv, q_norm_c, k_norm_c,
           w_branch, w_out):
    assert norm_in.shape[0] == 1, "single-layer block"
    return _layer(x, mem, norm_in[0], norm_mem[0], w_in[0], q_norm_a[0], k_norm_a[0], lower_bounds, o_norm_b[0],
                  w_mem_kv[0], q_norm_c[0], k_norm_c[0], w_branch[0], w_out[0])
```

```python
import functools

import jax
import jax.numpy as jnp
import numpy as np
from jax import lax
from jax.experimental import pallas as pl
from jax.experimental.pallas import tpu as pltpu

F32 = jnp.float32
BF16 = jnp.bfloat16
I32 = jnp.int32

N_HEADS_A = 8
HEAD_DIM_A = 64
N_IDX_HEADS = 8
IDX_DIM = 64
TOPK_MAX = 256
N_HEADS_B = 4
HEAD_B = 128
N_HEADS_C = 4
HEAD_DIM_C = 128
N_BRANCH = 3
EPS = 1e-6
WIDTH = 512
IDX_SCALE = (IDX_DIM ** -0.5) * (N_IDX_HEADS ** -0.5)

LANES = 128
VMEM_LIMIT = 56 * 1024 * 1024

TM_PROJ = 256
TQ = 128
TKS = 256
TKC = 512
CHUNK = 64
SUB = 16
T_HGRN = 128
TM_MERGE = 256

NEG_BIG = -1e30
LOG2E = 1.4426950408889634
INT_MIN = -2147483648
KEY_LOWEST = -2139095040
FLT_MAX = 3.4028234663852886e38
RANK16 = jnp.bfloat16
BF16_ROWS = 16
STAGE2_BITS = 17
STAGE2_GROUP = 6

NT = (((1,), (1,)), ((), ()))


def _dot(a, b):
    return jnp.dot(a, b, preferred_element_type=F32)


def _dot_nt(a, b):
    return lax.dot_general(a, b, NT, preferred_element_type=F32)


def _group_meansq(v, ones_ref, group):
    sq = v * v
    hi = sq.astype(BF16)
    lo = (sq - hi.astype(F32)).astype(BF16)
    s = _dot(hi, ones_ref[...]) + _dot(lo, ones_ref[...])
    return s * (1.0 / group)


_C_QA, _C_KA, _C_ZA, _C_QB, _C_FB, _C_IB, _C_GB, _C_QC, _C_ZC = [WIDTH * i for i in range(9)]
_C_GATES = 9 * WIDTH
_C_QI = _C_GATES + 3 * 1024
_C_KI = _C_QI + WIDTH
_C_END = _C_KI + LANES


def _proj_body(x_ref, nin_ref, w_ref, wvt_ref, wwt_ref, lbp_ref, gqa_ref, gka_ref, gqc_ref, ones64_ref, ones128_ref,
               qa_ref, ka_ref, vat_ref, sza_ref, qi_ref, ki2_ref, wit_ref,
               sqb_ref, logf_ref, kk_ref, ib_ref, ibt_ref, sgb_ref, qc_ref, szc_ref, gates_ref, *, d_model):
    x = x_ref[...]
    ms = jnp.mean(x * x, axis=-1, keepdims=True)
    h = (x * lax.rsqrt(ms + EPS)) * nin_ref[...]
    hb = h.astype(BF16)

    def proj(c0, width):
        return _dot(hb, w_ref[:, c0:c0 + width])

    qa = proj(_C_QA, WIDTH)
    qa = qa * lax.rsqrt(_group_meansq(qa, ones64_ref, HEAD_DIM_A) + EPS) * gqa_ref[...]
    qa_ref[...] = (qa * (HEAD_DIM_A ** -0.5 * LOG2E)).astype(BF16)
    ka = proj(_C_KA, WIDTH)
    ka = ka * lax.rsqrt(_group_meansq(ka, ones64_ref, HEAD_DIM_A) + EPS) * gka_ref[...]
    ka_ref[...] = ka.astype(BF16)
    vat = _dot_nt(wvt_ref[0:WIDTH, :], hb).astype(BF16)
    for t in range(vat_ref.shape[0]):
        vat_ref[t] = vat[:, t * LANES:(t + 1) * LANES]
    sza_ref[...] = jax.nn.silu(proj(_C_ZA, WIDTH)).astype(BF16)
    qi_ref[...] = proj(_C_QI, WIDTH).astype(BF16)
    ki2_ref[...] = proj(_C_KI, LANES).astype(BF16)
    wit_ref[...] = _dot_nt(wwt_ref[...], hb)
    sqb_ref[...] = jax.nn.silu(proj(_C_QB, WIDTH)).astype(BF16)
    lbp = lbp_ref[...]
    lbe = jnp.exp(lbp - jnp.max(lbp, axis=0, keepdims=True))
    lb = lbe[0:1, :] / jnp.sum(lbe, axis=0, keepdims=True)
    f = lb + (1.0 - lb) * jax.nn.sigmoid(proj(_C_FB, WIDTH))
    logf_ref[...] = jnp.log(f)
    kk_ref[...] = (1.0 - f).astype(BF16)
    ib_ref[...] = proj(_C_IB, WIDTH).astype(BF16)
    ibt = _dot_nt(wvt_ref[WIDTH:2 * WIDTH, :], hb).astype(BF16)
    for t in range(ibt_ref.shape[0]):
        ibt_ref[t] = ibt[:, t * LANES:(t + 1) * LANES]
    sgb_ref[...] = jax.nn.silu(proj(_C_GB, WIDTH)).astype(BF16)
    qc = proj(_C_QC, WIDTH)
    qc = qc * lax.rsqrt(_group_meansq(qc, ones128_ref, HEAD_DIM_C) + EPS) * gqc_ref[...]
    qc_ref[...] = (qc * (HEAD_DIM_C ** -0.5)).astype(BF16)
    szc_ref[...] = jax.nn.silu(proj(_C_ZC, WIDTH)).astype(BF16)
    for n in range(N_BRANCH):
        gates_ref[:, n * d_model:(n + 1) * d_model] = jax.nn.sigmoid(
            proj(_C_GATES + n * d_model, d_model)).astype(BF16)


def _const_spec(shape):
    nd = len(shape)
    return pl.BlockSpec(shape, lambda *_: (0,) * nd, pipeline_mode=pl.Buffered(1))


def _block_ones(width, group):
    g = np.arange(width) // group
    return jnp.asarray((g[:, None] == g[None, :]).astype(np.float32), dtype=BF16)


def _proj_call(x2, norm_in, w_in, lower_bounds, q_norm_a, k_norm_a, q_norm_c):
    rows, d_model = x2.shape
    tm = TM_PROJ
    assert rows % tm == 0 and d_model == 1024
    offs = np.cumsum([0, 512, 512, 512, 512, 512, 64, 8, 512, 512, 512, 512, 512, 512, 3 * d_model])
    (o_qa, o_ka, o_va, o_za, o_qi, o_ki, o_wi, o_qb, o_fb, o_ib, o_gb, o_qc, o_zc, o_g, o_end) = [int(o) for o in offs]
    assert o_end == w_in.shape[1]
    wb = w_in.astype(BF16)
    col = lambda a, b: wb[:, a:b]
    w_main = jnp.concatenate([
        col(o_qa, o_ka), col(o_ka, o_va), col(o_za, o_qi), col(o_qb, o_fb), col(o_fb, o_ib), col(o_ib, o_gb),
        col(o_gb, o_qc), col(o_qc, o_zc), col(o_zc, o_g), col(o_g, o_end), col(o_qi, o_ki),
        col(o_ki, o_wi), col(o_ki, o_wi)], axis=1)
    assert w_main.shape[1] == _C_END
    wvt = jnp.concatenate([col(o_va, o_za).T, col(o_ib, o_gb).T], axis=0)
    wwt = jnp.concatenate([col(o_wi, o_qb).T, jnp.zeros((8, d_model), BF16)], axis=0)
    tile = lambda g, reps: jnp.tile(g.astype(F32), reps)[None, :]
    n_slots = lower_bounds.shape[0]

    row_spec = lambda w: pl.BlockSpec((tm, w), lambda i: (i, 0))
    t_spec = pl.BlockSpec((tm // LANES, WIDTH, LANES), lambda i: (i, 0, 0))
    out_shape = [
        jax.ShapeDtypeStruct((rows, WIDTH), BF16),
        jax.ShapeDtypeStruct((rows, WIDTH), BF16),
        jax.ShapeDtypeStruct((rows // LANES, WIDTH, LANES), BF16),
        jax.ShapeDtypeStruct((rows, WIDTH), BF16),
        jax.ShapeDtypeStruct((rows, WIDTH), BF16),
        jax.ShapeDtypeStruct((rows, LANES), BF16),
        jax.ShapeDtypeStruct((16, rows), F32),
        jax.ShapeDtypeStruct((rows, WIDTH), BF16),
        jax.ShapeDtypeStruct((rows, WIDTH), F32),
        jax.ShapeDtypeStruct((rows, WIDTH), BF16),
        jax.ShapeDtypeStruct((rows, WIDTH), BF16),
        jax.ShapeDtypeStruct((rows // LANES, WIDTH, LANES), BF16),
        jax.ShapeDtypeStruct((rows, WIDTH), BF16),
        jax.ShapeDtypeStruct((rows, WIDTH), BF16),
        jax.ShapeDtypeStruct((rows, WIDTH), BF16),
        jax.ShapeDtypeStruct((rows, N_BRANCH * d_model), BF16),
    ]
    out_specs = [row_spec(WIDTH), row_spec(WIDTH), t_spec, row_spec(WIDTH), row_spec(WIDTH), row_spec(LANES),
                 pl.BlockSpec((16, tm), lambda i: (0, i)),
                 row_spec(WIDTH), row_spec(WIDTH), row_spec(WIDTH), row_spec(WIDTH), t_spec, row_spec(WIDTH),
                 row_spec(WIDTH), row_spec(WIDTH), row_spec(N_BRANCH * d_model)]
    in_specs = [row_spec(d_model), _const_spec((1, d_model)), _const_spec(w_main.shape), _const_spec(wvt.shape),
                _const_spec(wwt.shape), _const_spec((n_slots, WIDTH)), _const_spec((1, WIDTH)),
                _const_spec((1, WIDTH)), _const_spec((1, WIDTH)), _const_spec((WIDTH, WIDTH)),
                _const_spec((WIDTH, WIDTH))]
    return pl.pallas_call(
        functools.partial(_proj_body, d_model=d_model),
        out_shape=out_shape, grid=(rows // tm,), in_specs=in_specs, out_specs=out_specs,
        compiler_params=pltpu.CompilerParams(dimension_semantics=("parallel",), vmem_limit_bytes=VMEM_LIMIT),
        name="proj",
    )(x2, norm_in.astype(F32)[None, :], w_main, wvt, wwt, lower_bounds.astype(F32),
      tile(q_norm_a, N_HEADS_A), tile(k_norm_a, N_HEADS_A), tile(q_norm_c, N_HEADS_C),
      _block_ones(WIDTH, HEAD_DIM_A), _block_ones(WIDTH, HEAD_DIM_C))


def _memkv_body(m_ref, nm_ref, w_ref, gk_ref, ones128_ref, kc_ref, vc_ref):
    x = m_ref[...]
    ms = jnp.mean(x * x, axis=-1, keepdims=True)
    hb = ((x * lax.rsqrt(ms + EPS)) * nm_ref[...]).astype(BF16)
    kc = _dot(hb, w_ref[:, 0:WIDTH])
    kc = kc * lax.rsqrt(_group_meansq(kc, ones128_ref, HEAD_DIM_C) + EPS) * gk_ref[...]
    kc_ref[...] = kc.astype(BF16)
    vc_ref[...] = _dot(hb, w_ref[:, WIDTH:2 * WIDTH]).astype(BF16)


def _memkv_call(mem2, norm_mem, w_mem_kv, k_norm_c, tm):
    rows, d_model = mem2.shape
    row_spec = lambda w: pl.BlockSpec((tm, w), lambda i: (i, 0))
    return pl.pallas_call(
        _memkv_body,
        out_shape=[jax.ShapeDtypeStruct((rows, WIDTH), BF16)] * 2,
        grid=(rows // tm,),
        in_specs=[row_spec(d_model), _const_spec((1, d_model)), _const_spec((d_model, 2 * WIDTH)),
                  _const_spec((1, WIDTH)), _const_spec((WIDTH, WIDTH))],
        out_specs=[row_spec(WIDTH), row_spec(WIDTH)],
        compiler_params=pltpu.CompilerParams(dimension_semantics=("parallel",), vmem_limit_bytes=VMEM_LIMIT),
        name="memkv",
    )(mem2, norm_mem.astype(F32)[None, :], w_mem_kv.astype(BF16),
      jnp.tile(k_norm_c.astype(F32), N_HEADS_C)[None, :], _block_ones(WIDTH, HEAD_DIM_C))


def _key_to_f32(k):
    return pltpu.bitcast(jnp.where(k < 0, k ^ jnp.int32(0x7FFFFFFF), k), F32)


def _slope_log2(h):
    return (2.0 ** (-8.0 * (h + 1) / N_HEADS_A)) * LOG2E


N_SLOPE_PARTS = 3


def _slope_rows(h, row):
    rest = np.float32(_slope_log2(h))
    out = jnp.zeros(row.shape, F32)
    for r in range(N_SLOPE_PARTS):
        part = np.float32(np.asarray(rest, dtype=jnp.bfloat16))
        out = jnp.where(row == r, float(part), out)
        rest = np.float32(rest - part)
    return out


def _dsa_body(qa_ref, qi_ref, wit_ref, sza_ref, ka_ref, vat_ref, ki2_ref, out_ref,
              sc_ref, hi_ref, qir_ref, qar_ref, m_ref, l_ref, alpha_ref, acc_ref, pbuf_ref, sbuf_ref, ot_ref,
              *, topk, pos_bits):
    i = pl.program_id(1)
    q0 = i * TQ
    n_att = (q0 + TQ + TKS - 1) // TKS
    n_cnt = (q0 + TQ + TKC - 1) // TKC
    n_sc = n_cnt * (TKC // TKS)

    row = lax.broadcasted_iota(I32, (LANES, TQ), 0)
    row_lo = row < HEAD_DIM_A
    for p in range(N_HEADS_A // 2):
        cols = slice(p * LANES, (p + 1) * LANES)
        qit = qi_ref[:, cols].astype(F32).T
        qat = qa_ref[:, cols].astype(F32).T
        qir_ref[p] = jnp.concatenate([jnp.where(row_lo, qit, 0.0), jnp.where(row_lo, 0.0, qit)], axis=1).astype(BF16)
        qar_ref[p, 0:LANES, :] = jnp.concatenate(
            [jnp.where(row_lo, qat, 0.0), jnp.where(row_lo, 0.0, qat)], axis=1).astype(BF16)
        qar_ref[p, LANES:2 * LANES, :] = jnp.concatenate(
            [_slope_rows(2 * p, row), _slope_rows(2 * p + 1, row)], axis=1).astype(BF16)
    m_ref[...] = jnp.full(m_ref.shape, NEG_BIG, F32)
    l_ref[...] = jnp.zeros(l_ref.shape, F32)
    acc_ref[...] = jnp.zeros(acc_ref.shape, F32)
    alpha_ref[...] = jnp.ones(alpha_ref.shape, F32)
    pbuf_ref[...] = jnp.zeros(pbuf_ref.shape, BF16)

    row_s = lax.broadcasted_iota(I32, (TKS, TQ), 0)
    tpos = q0 + lax.broadcasted_iota(I32, (TKS, TQ), 1)

    def score_tile(j, carry):
        r0 = pl.multiple_of(j * TKS, TKS)
        kt = ki2_ref[pl.ds(r0, TKS), :]
        acc = jnp.zeros((TKS, TQ), F32)
        for p in range(N_IDX_HEADS // 2):
            rel = jnp.maximum(_dot(kt, qir_ref[p]), 0.0)
            acc = acc + rel[:, :TQ] * wit_ref[2 * p:2 * p + 1, :] + rel[:, TQ:] * wit_ref[2 * p + 1:2 * p + 2, :]
        sc = acc * IDX_SCALE
        sc = jnp.where(sc == 0.0, 0.0, sc)
        sc = jnp.where(r0 + row_s <= tpos, sc, -jnp.inf)
        sc_ref[pl.ds(r0, TKS), :] = sc
        hi_ref[pl.ds(r0, TKS), :] = sc.astype(RANK16)
        return carry

    lax.fori_loop(0, n_sc, score_tile, 0)

    row_c = lax.broadcasted_iota(I32, (TKC, TQ), 0)

    def count(pred):
        def tile(j, acc):
            r0 = pl.multiple_of(j * TKC, TKC)
            c = jnp.where(pred(sc_ref[pl.ds(r0, TKC), :], r0), jnp.int32(1), jnp.int32(0))
            return acc + jnp.sum(c.reshape(TKC // 8, 8, TQ), axis=0)
        acc = lax.fori_loop(0, n_cnt, tile, jnp.zeros((8, TQ), I32))
        return jnp.sum(acc, axis=0, keepdims=True)

    def count16(cand_f):
        cand = jnp.broadcast_to(cand_f, (BF16_ROWS, TQ)).astype(RANK16)
        one, zero = jnp.ones((), RANK16), jnp.zeros((), RANK16)

        def tile(j, acc):
            r0 = pl.multiple_of(j * TKC, TKC)
            t = hi_ref[pl.ds(r0, TKC), :].reshape(TKC // BF16_ROWS, BF16_ROWS, TQ)
            c = jnp.where(t >= cand[None], one, zero)
            parts = [c[g] for g in range(TKC // BF16_ROWS)]
            while len(parts) > 1:
                parts = [a + b for a, b in zip(parts[::2], parts[1::2])]
            return acc + parts[0].astype(F32)
        acc = lax.fori_loop(0, n_cnt, tile, jnp.zeros((BF16_ROWS, TQ), F32))
        return jnp.sum(acc, axis=0, keepdims=True)

    def bisect16(it, p):
        cand = p + lax.shift_left(jnp.int32(1), 15 - it)
        bits = lax.shift_left(jnp.where(cand < 0, cand ^ jnp.int32(0x7FFF), cand), 16)
        n = count16(pltpu.bitcast(bits, F32))
        return jnp.where(n >= topk, cand, p)

    k16 = lax.fori_loop(0, 16, bisect16, jnp.full((1, TQ), -32768, I32))
    key_v = lax.shift_left(k16, 16) + jnp.where(k16 < 0, jnp.int32(0xFFFF), jnp.int32(0))
    low = jnp.maximum(key_v, jnp.int32(INT_MIN + 0x8000)) - jnp.int32(0x8000)

    n_valid = count(lambda s, r0: s >= -FLT_MAX)

    def stage2_group(c):
        g, p, done = c
        for u in range(STAGE2_GROUP):
            step = g * STAGE2_GROUP + u
            bit = jnp.where(step < STAGE2_BITS, lax.shift_left(jnp.int32(1), jnp.maximum(STAGE2_BITS - 1 - step, 0)), 0)
            top = jnp.int32(2 ** 31 - 1)
            cand = jnp.where(p > top - bit, top, p + bit)
            cand_f = _key_to_f32(cand)
            n = count(lambda s, r0: s >= cand_f)
            p = jnp.where((n >= topk) & (done == 0), cand, p)
            done = jnp.where(n == topk, 1, done)
        return g + 1, p, done

    _, kth, _ = lax.while_loop(
        lambda c: (c[0] < -(-STAGE2_BITS // STAGE2_GROUP)) & (jnp.min(c[2]) == 0),
        stage2_group, (jnp.int32(0), low, jnp.where(n_valid <= topk, 1, 0)))
    kth = jnp.maximum(kth, jnp.int32(KEY_LOWEST))
    thr = _key_to_f32(kth)
    nxt = _key_to_f32(kth + 1)
    n_gt = count(lambda s, r0: s >= nxt)
    n_ge = count(lambda s, r0: s >= thr)
    need = topk - n_gt
    has_tie = n_ge > topk

    def tie_cut():
        def step(it, jp):
            cand = jp | lax.shift_left(jnp.int32(1), pos_bits - 1 - it)
            n = count(lambda s, r0: (s >= thr) & (s < nxt) & (r0 + row_c < cand))
            return jnp.where(n < need, cand, jp)
        return lax.fori_loop(0, pos_bits, step, jnp.zeros((1, TQ), I32))

    cut = lax.cond(jnp.max(jnp.where(has_tie, 1, 0)) > 0, tie_cut, lambda: jnp.zeros((1, TQ), I32))
    cut = jnp.where(has_tie, cut, jnp.int32(2 ** 30))

    lane_a = lax.broadcasted_iota(I32, (TKS, LANES), 1)
    aug = jnp.where(lane_a < N_SLOPE_PARTS, lax.broadcasted_iota(I32, (TKS, LANES), 0).astype(F32), 0.0).astype(BF16)
    half2 = lax.broadcasted_iota(I32, (1, 2 * TQ), 1) < TQ

    pairs = range(N_HEADS_A // 2)

    def value_update(jt):
        for p in pairs:
            vt = jnp.concatenate([vat_ref[jt * (TKS // LANES) + t, p * LANES:(p + 1) * LANES, :]
                                  for t in range(TKS // LANES)], axis=1)
            acc_ref[p] = alpha_ref[p] * acc_ref[p] + _dot(vt, pbuf_ref[p])

    def score_dots(jt):
        r0 = pl.multiple_of(jt * TKS, TKS)
        return [_dot(jnp.concatenate([ka_ref[pl.ds(r0, TKS), p * LANES:(p + 1) * LANES], aug], axis=1), qar_ref[p])
                for p in pairs]

    for p, s in zip(pairs, score_dots(0)):
        sbuf_ref[p] = s

    def att_tile(j, carry):
        r0 = pl.multiple_of(j * TKS, TKS)
        s_idx = sc_ref[pl.ds(r0, TKS), :]
        sel = (s_idx >= thr) & ((s_idx >= nxt) | (r0 + row_s <= cut))
        bias = jnp.where(sel, 0.0, NEG_BIG)
        bias2 = jnp.concatenate([bias, bias], axis=1)
        base = (r0 - q0).astype(F32)
        value_update(jnp.maximum(j - 1, 0))
        ahead = score_dots(jnp.minimum(j + 1, n_att - 1))
        for p in pairs:
            shift = jnp.where(half2, _slope_log2(2 * p), _slope_log2(2 * p + 1)) * base
            s = sbuf_ref[p] + bias2
            m_old = m_ref[p]
            m_new = jnp.maximum(m_old, jnp.max(s, axis=0, keepdims=True) + shift)
            alpha = jnp.exp2(m_old - m_new)
            pr = jnp.exp2(s - (m_new - shift))
            l_ref[p] = alpha * l_ref[p] + jnp.sum(pr, axis=0, keepdims=True)
            m_ref[p] = m_new
            alpha_ref[p] = alpha
            pbuf_ref[p] = pr.astype(BF16)
        for p in pairs:
            sbuf_ref[p] = ahead[p]
        return carry

    lax.fori_loop(0, n_att, att_tile, 0)
    value_update(n_att - 1)

    row_lo = lax.broadcasted_iota(I32, (LANES, TQ), 0) < HEAD_DIM_A
    for p in range(N_HEADS_A // 2):
        o = acc_ref[p] * (1.0 / l_ref[p])
        ot_ref[p * LANES:(p + 1) * LANES, :] = jnp.where(row_lo, o[:, :TQ], o[:, TQ:])
    out_ref[...] = (ot_ref[...].T * sza_ref[...].astype(F32)).astype(BF16)


def _dsa_call(qa, qi, wit, sza, ka, vat, ki2, batch, seq):
    assert TQ == LANES and TKS % LANES == 0 and TKC % TKS == 0 and seq % TKC == 0
    nq = seq // TQ
    topk = min(TOPK_MAX, seq // 4)
    pos_bits = max(1, int(np.ceil(np.log2(seq))))
    q_spec = lambda w: pl.BlockSpec((TQ, w), lambda b, i: (b * nq + i, 0))
    b_spec = lambda w: pl.BlockSpec((seq, w), lambda b, i: (b, 0))
    n_pair = N_HEADS_A // 2
    return pl.pallas_call(
        functools.partial(_dsa_body, topk=topk, pos_bits=pos_bits),
        out_shape=jax.ShapeDtypeStruct((batch * seq, WIDTH), BF16),
        grid=(batch, nq),
        in_specs=[q_spec(WIDTH), q_spec(WIDTH), pl.BlockSpec((16, TQ), lambda b, i: (0, b * nq + i)), q_spec(WIDTH),
                  b_spec(WIDTH), pl.BlockSpec((seq // LANES, WIDTH, LANES), lambda b, i: (b, 0, 0)), b_spec(LANES)],
        out_specs=q_spec(WIDTH),
        scratch_shapes=[pltpu.VMEM((seq, TQ), F32), pltpu.VMEM((seq, TQ), RANK16),
                        pltpu.VMEM((n_pair, LANES, 2 * TQ), BF16), pltpu.VMEM((n_pair, 2 * LANES, 2 * TQ), BF16),
                        pltpu.VMEM((n_pair, 1, 2 * TQ), F32), pltpu.VMEM((n_pair, 1, 2 * TQ), F32),
                        pltpu.VMEM((n_pair, 1, 2 * TQ), F32), pltpu.VMEM((n_pair, LANES, 2 * TQ), F32),
                        pltpu.VMEM((n_pair, TKS, 2 * TQ), BF16), pltpu.VMEM((n_pair, TKS, 2 * TQ), F32),
                        pltpu.VMEM((WIDTH, TQ), F32)],
        compiler_params=pltpu.CompilerParams(dimension_semantics=("parallel", "arbitrary"),
                                             vmem_limit_bytes=VMEM_LIMIT),
        name="dsa",
    )(qa, qi, wit, sza, ka, vat, ki2)


def _split3(v):
    hi = v.astype(BF16)
    r = v - hi.astype(F32)
    mid = r.astype(BF16)
    lo = (r - mid.astype(F32)).astype(BF16)
    return hi, mid, lo


def _hgrn_body(sqb_ref, logf_ref, kk_ref, ib_ref, ibt_ref, sgb_ref, gon_ref, out_ref, st_ref):
    @pl.when(pl.program_id(1) == 0)
    def _():
        st_ref[...] = jnp.zeros_like(st_ref)

    r_i = lax.broadcasted_iota(I32, (CHUNK, CHUNK), 0)
    c_i = lax.broadcasted_iota(I32, (CHUNK, CHUNK), 1)
    tril = c_i <= r_i
    tri = jnp.where(tril, 1.0, 0.0).astype(BF16)
    n_sub = CHUNK // SUB
    row_c = lax.broadcasted_iota(I32, (CHUNK, LANES), 0)
    row_t = lax.broadcasted_iota(I32, (T_HGRN, LANES), 0)

    for c in range(T_HGRN // CHUNK):
        rows = slice(c * CHUNK, (c + 1) * CHUNK)
        g = logf_ref[rows, :]
        b = sum(_dot(tri, part) for part in _split3(g))
        q = sqb_ref[rows, :].astype(F32)
        kk = kk_ref[rows, :].astype(F32)
        b_last = b[CHUNK - 1:CHUNK, :]
        ref_k = jnp.concatenate(
            [jnp.broadcast_to(b[(j + 1) * SUB - 1:(j + 1) * SUB, :], (SUB, WIDTH)) for j in range(n_sub)], axis=0)
        k_in = kk * jnp.exp(ref_k - b)
        q_out = (q * jnp.exp(b)).astype(BF16)
        k_st = (kk * jnp.exp(b_last - b)).astype(BF16)
        q_in = []
        for j in range(n_sub):
            lo = j * SUB
            qj = q[lo:, :] * jnp.exp(b[lo:, :] - b[lo + SUB - 1:lo + SUB, :])
            if lo:
                qj = jnp.concatenate([jnp.zeros((lo, WIDTH), F32), qj], axis=0)
            q_in.append(qj.astype(BF16))
        for hd in range(N_HEADS_B):
            cols = slice(hd * HEAD_B, (hd + 1) * HEAD_B)
            qs = jnp.concatenate([qj[:, cols] for qj in q_in], axis=1)
            kh = k_in[:, cols]
            ks = jnp.concatenate(
                [jnp.where((row_c >= j * SUB) & (row_c < (j + 1) * SUB), kh, 0.0) for j in range(n_sub)],
                axis=1).astype(BF16)
            a = jnp.where(tril, _dot_nt(qs, ks), 0.0).astype(BF16)
            st = st_ref[hd]
            o = _dot(a, ib_ref[rows, cols]) + _dot_nt(q_out[:, cols], st.astype(BF16))
            k_pad = jnp.where((row_t >= c * CHUNK) & (row_t < (c + 1) * CHUNK),
                              jnp.concatenate([k_st[:, cols]] * (T_HGRN // CHUNK), axis=0), jnp.zeros((), BF16))
            st_ref[hd] = st * jnp.exp(b_last[:, cols]) + _dot(ibt_ref[0, cols, :], k_pad)
            ms = jnp.mean(o * o, axis=-1, keepdims=True)
            o = o * lax.rsqrt(ms + EPS) * gon_ref[...]
            out_ref[rows, cols] = (o * sgb_ref[rows, cols].astype(F32)).astype(BF16)


def _hgrn_call(sqb, logf, kk, ib, ibt, sgb, o_norm_b, batch, seq):
    assert T_HGRN == LANES and seq % T_HGRN == 0
    nt = seq // T_HGRN
    spec = pl.BlockSpec((T_HGRN, WIDTH), lambda b, t: (b * nt + t, 0))
    ibt_spec = pl.BlockSpec((1, WIDTH, T_HGRN), lambda b, t: (b * nt + t, 0, 0))
    return pl.pallas_call(
        _hgrn_body,
        out_shape=jax.ShapeDtypeStruct((batch * seq, WIDTH), BF16),
        grid=(batch, nt),
        in_specs=[spec, spec, spec, spec, ibt_spec, spec, _const_spec((1, HEAD_B))],
        out_specs=spec,
        scratch_shapes=[pltpu.VMEM((N_HEADS_B, HEAD_B, HEAD_B), F32)],
        compiler_params=pltpu.CompilerParams(dimension_semantics=("parallel", "arbitrary"),
                                             vmem_limit_bytes=VMEM_LIMIT),
        name="hgrn",
    )(sqb, logf, kk, ib, ibt, sgb, o_norm_b.astype(F32)[None, :])


def _merge_body(x_ref, oa_ref, ob_ref, qc_ref, szc_ref, gates_ref, kc_ref, vc_ref, wb_ref, wo_ref, out_ref, *, d_model):
    oc = []
    for hd in range(N_HEADS_C):
        cols = slice(hd * HEAD_DIM_C, (hd + 1) * HEAD_DIM_C)
        s = _dot_nt(qc_ref[:, cols], kc_ref[:, cols])
        m = jnp.max(s, axis=-1, keepdims=True)
        p = jnp.exp(s - m)
        l = jnp.sum(p, axis=-1, keepdims=True)
        oc.append(_dot(p.astype(BF16), vc_ref[:, cols]) * (1.0 / l))
    oc = (jnp.concatenate(oc, axis=1) * szc_ref[...].astype(F32)).astype(BF16)
    merged = jnp.zeros((x_ref.shape[0], d_model), F32)
    for n, br in enumerate((oa_ref[...], ob_ref[...], oc)):
        y = _dot(br, wb_ref[n])
        merged = merged + gates_ref[:, n * d_model:(n + 1) * d_model].astype(F32) * y
    out_ref[...] = x_ref[...] + _dot(merged.astype(BF16), wo_ref[...])


def _merge_call(x2, oa, ob, qc, szc, gates, kc, vc, w_branch, w_out, batch, seq, mem_len):
    rows, d_model = x2.shape
    tm = TM_MERGE
    nt = seq // tm
    row_spec = lambda w: pl.BlockSpec((tm, w), lambda b, t: (b * nt + t, 0))
    mem_spec = pl.BlockSpec((mem_len, WIDTH), lambda b, t: (b, 0))
    return pl.pallas_call(
        functools.partial(_merge_body, d_model=d_model),
        out_shape=jax.ShapeDtypeStruct((rows, d_model), F32),
        grid=(batch, nt),
        in_specs=[row_spec(d_model), row_spec(WIDTH), row_spec(WIDTH), row_spec(WIDTH), row_spec(WIDTH),
                  row_spec(N_BRANCH * d_model), mem_spec, mem_spec,
                  _const_spec((N_BRANCH, WIDTH, d_model)), _const_spec((d_model, d_model))],
        out_specs=row_spec(d_model),
        compiler_params=pltpu.CompilerParams(dimension_semantics=("parallel", "parallel"),
                                             vmem_limit_bytes=VMEM_LIMIT),
        name="merge",
    )(x2, oa, ob, qc, szc, gates, kc, vc, w_branch.astype(BF16), w_out.astype(BF16))


def _layer(x, mem, norm_in, norm_mem, w_in, q_norm_a, k_norm_a, lower_bounds, o_norm_b, w_mem_kv, q_norm_c, k_norm_c,
           w_branch, w_out):
    batch, seq, d_model = x.shape
    mem_len = mem.shape[1]
    x2 = x.reshape(batch * seq, d_model)
    (qa, ka, vat, sza, qi, ki2, wit, sqb, logf, kk, ib, ibt, sgb, qc, szc, gates) = _proj_call(
        x2, norm_in, w_in, lower_bounds, q_norm_a, k_norm_a, q_norm_c)
    kc, vc = _memkv_call(mem.reshape(batch * mem_len, d_model), norm_mem, w_mem_kv, k_norm_c, mem_len)
    oa = _dsa_call(qa, qi, wit, sza, ka, vat, ki2, batch, seq)
    ob = _hgrn_call(sqb, logf, kk, ib, ibt, sgb, o_norm_b, batch, seq)
    out = _merge_call(x2, oa, ob, qc, szc, gates, kc, vc, w_branch, w_out, batch, seq, mem_len)
    return out.reshape(batch, seq, d_model)


def kernel(x, mem, norm_in, norm_mem, w_in, q_norm_a, k_norm_a, lower_bounds, o_norm_b, w_mem_kv, q_norm_c, k_norm_c,
           w_branch, w_out):
    assert norm_in.shape[0] == 1, "single-layer block"
    return _layer(x, mem, norm_in[0], norm_mem[0], w_in[0], q_norm_a[0], k_norm_a[0], lower_bounds, o_norm_b[0],
                  w_mem_kv[0], q_norm_c[0], k_norm_c[0], w_branch[0], w_out[0])
```

```python
import functools

import jax
import jax.numpy as jnp
import numpy as np
from jax import lax
from jax.experimental import pallas as pl
from jax.experimental.pallas import tpu as pltpu

F32 = jnp.float32
BF16 = jnp.bfloat16
I32 = jnp.int32

N_HEADS_A = 8
HEAD_DIM_A = 64
N_IDX_HEADS = 8
IDX_DIM = 64
TOPK_MAX = 256
N_HEADS_B = 4
HEAD_B = 128
N_HEADS_C = 4
HEAD_DIM_C = 128
N_BRANCH = 3
EPS = 1e-6
WIDTH = 512
IDX_SCALE = (IDX_DIM ** -0.5) * (N_IDX_HEADS ** -0.5)

LANES = 128
VMEM_LIMIT = 56 * 1024 * 1024

TM_PROJ = 256
TQ = 128
TKS = 256
TKC = 512
CHUNK = 64
SUB = 16
T_HGRN = 128
TM_MERGE = 256

NEG_BIG = -1e30
LOG2E = 1.4426950408889634
INT_MIN = -2147483648
KEY_LOWEST = -2139095040

NT = (((1,), (1,)), ((), ()))


def _dot(a, b):
    return jnp.dot(a, b, preferred_element_type=F32)


def _dot_nt(a, b):
    return lax.dot_general(a, b, NT, preferred_element_type=F32)


def _group_meansq(v, ones_ref, group):
    sq = v * v
    hi = sq.astype(BF16)
    lo = (sq - hi.astype(F32)).astype(BF16)
    s = _dot(hi, ones_ref[...]) + _dot(lo, ones_ref[...])
    return s * (1.0 / group)


_C_QA, _C_KA, _C_ZA, _C_QB, _C_FB, _C_IB, _C_GB, _C_QC, _C_ZC = [WIDTH * i for i in range(9)]
_C_GATES = 9 * WIDTH
_C_QI = _C_GATES + 3 * 1024
_C_KI = _C_QI + WIDTH
_C_END = _C_KI + LANES


def _proj_body(x_ref, nin_ref, w_ref, wvt_ref, wwt_ref, lbp_ref, gqa_ref, gka_ref, gqc_ref, ones64_ref, ones128_ref,
               qa_ref, ka_ref, vat_ref, sza_ref, qi_ref, ki2_ref, wit_ref,
               sqb_ref, logf_ref, kk_ref, ib_ref, ibt_ref, sgb_ref, qc_ref, szc_ref, gates_ref, *, d_model):
    x = x_ref[...]
    ms = jnp.mean(x * x, axis=-1, keepdims=True)
    h = (x * lax.rsqrt(ms + EPS)) * nin_ref[...]
    hb = h.astype(BF16)

    def proj(c0, width):
        return _dot(hb, w_ref[:, c0:c0 + width])

    qa = proj(_C_QA, WIDTH)
    qa = qa * lax.rsqrt(_group_meansq(qa, ones64_ref, HEAD_DIM_A) + EPS) * gqa_ref[...]
    qa_ref[...] = (qa * (HEAD_DIM_A ** -0.5 * LOG2E)).astype(BF16)
    ka = proj(_C_KA, WIDTH)
    ka = ka * lax.rsqrt(_group_meansq(ka, ones64_ref, HEAD_DIM_A) + EPS) * gka_ref[...]
    ka_ref[...] = ka.astype(BF16)
    vat = _dot_nt(wvt_ref[0:WIDTH, :], hb).astype(BF16)
    for t in range(vat_ref.shape[0]):
        vat_ref[t] = vat[:, t * LANES:(t + 1) * LANES]
    sza_ref[...] = jax.nn.silu(proj(_C_ZA, WIDTH)).astype(BF16)
    qi_ref[...] = proj(_C_QI, WIDTH).astype(BF16)
    ki2_ref[...] = proj(_C_KI, LANES).astype(BF16)
    wit_ref[...] = _dot_nt(wwt_ref[...], hb)
    sqb_ref[...] = jax.nn.silu(proj(_C_QB, WIDTH)).astype(BF16)
    lbp = lbp_ref[...]
    lbe = jnp.exp(lbp - jnp.max(lbp, axis=0, keepdims=True))
    lb = lbe[0:1, :] / jnp.sum(lbe, axis=0, keepdims=True)
    f = lb + (1.0 - lb) * jax.nn.sigmoid(proj(_C_FB, WIDTH))
    logf_ref[...] = jnp.log(f)
    kk_ref[...] = (1.0 - f).astype(BF16)
    ib_ref[...] = proj(_C_IB, WIDTH).astype(BF16)
    ibt = _dot_nt(wvt_ref[WIDTH:2 * WIDTH, :], hb).astype(BF16)
    for t in range(ibt_ref.shape[0]):
        ibt_ref[t] = ibt[:, t * LANES:(t + 1) * LANES]
    sgb_ref[...] = jax.nn.silu(proj(_C_GB, WIDTH)).astype(BF16)
    qc = proj(_C_QC, WIDTH)
    qc = qc * lax.rsqrt(_group_meansq(qc, ones128_ref, HEAD_DIM_C) + EPS) * gqc_ref[...]
    qc_ref[...] = (qc * (HEAD_DIM_C ** -0.5)).astype(BF16)
    szc_ref[...] = jax.nn.silu(proj(_C_ZC, WIDTH)).astype(BF16)
    for n in range(N_BRANCH):
        gates_ref[:, n * d_model:(n + 1) * d_model] = jax.nn.sigmoid(
            proj(_C_GATES + n * d_model, d_model)).astype(BF16)


def _const_spec(shape):
    nd = len(shape)
    return pl.BlockSpec(shape, lambda *_: (0,) * nd, pipeline_mode=pl.Buffered(1))


def _block_ones(width, group):
    g = np.arange(width) // group
    return jnp.asarray((g[:, None] == g[None, :]).astype(np.float32), dtype=BF16)


def _proj_call(x2, norm_in, w_in, lower_bounds, q_norm_a, k_norm_a, q_norm_c):
    rows, d_model = x2.shape
    tm = TM_PROJ
    assert rows % tm == 0 and d_model == 1024
    offs = np.cumsum([0, 512, 512, 512, 512, 512, 64, 8, 512, 512, 512, 512, 512, 512, 3 * d_model])
    (o_qa, o_ka, o_va, o_za, o_qi, o_ki, o_wi, o_qb, o_fb, o_ib, o_gb, o_qc, o_zc, o_g, o_end) = [int(o) for o in offs]
    assert o_end == w_in.shape[1]
    wb = w_in.astype(BF16)
    col = lambda a, b: wb[:, a:b]
    w_main = jnp.concatenate([
        col(o_qa, o_ka), col(o_ka, o_va), col(o_za, o_qi), col(o_qb, o_fb), col(o_fb, o_ib), col(o_ib, o_gb),
        col(o_gb, o_qc), col(o_qc, o_zc), col(o_zc, o_g), col(o_g, o_end), col(o_qi, o_ki),
        col(o_ki, o_wi), col(o_ki, o_wi)], axis=1)
    assert w_main.shape[1] == _C_END
    wvt = jnp.concatenate([col(o_va, o_za).T, col(o_ib, o_gb).T], axis=0)
    wwt = jnp.concatenate([col(o_wi, o_qb).T, jnp.zeros((8, d_model), BF16)], axis=0)
    tile = lambda g, reps: jnp.tile(g.astype(F32), reps)[None, :]
    n_slots = lower_bounds.shape[0]

    row_spec = lambda w: pl.BlockSpec((tm, w), lambda i: (i, 0))
    t_spec = pl.BlockSpec((tm // LANES, WIDTH, LANES), lambda i: (i, 0, 0))
    out_shape = [
        jax.ShapeDtypeStruct((rows, WIDTH), BF16),
        jax.ShapeDtypeStruct((rows, WIDTH), BF16),
        jax.ShapeDtypeStruct((rows // LANES, WIDTH, LANES), BF16),
        jax.ShapeDtypeStruct((rows, WIDTH), BF16),
        jax.ShapeDtypeStruct((rows, WIDTH), BF16),
        jax.ShapeDtypeStruct((rows, LANES), BF16),
        jax.ShapeDtypeStruct((16, rows), F32),
        jax.ShapeDtypeStruct((rows, WIDTH), BF16),
        jax.ShapeDtypeStruct((rows, WIDTH), F32),
        jax.ShapeDtypeStruct((rows, WIDTH), BF16),
        jax.ShapeDtypeStruct((rows, WIDTH), BF16),
        jax.ShapeDtypeStruct((rows // LANES, WIDTH, LANES), BF16),
        jax.ShapeDtypeStruct((rows, WIDTH), BF16),
        jax.ShapeDtypeStruct((rows, WIDTH), BF16),
        jax.ShapeDtypeStruct((rows, WIDTH), BF16),
        jax.ShapeDtypeStruct((rows, N_BRANCH * d_model), BF16),
    ]
    out_specs = [row_spec(WIDTH), row_spec(WIDTH), t_spec, row_spec(WIDTH), row_spec(WIDTH), row_spec(LANES),
                 pl.BlockSpec((16, tm), lambda i: (0, i)),
                 row_spec(WIDTH), row_spec(WIDTH), row_spec(WIDTH), row_spec(WIDTH), t_spec, row_spec(WIDTH),
                 row_spec(WIDTH), row_spec(WIDTH), row_spec(N_BRANCH * d_model)]
    in_specs = [row_spec(d_model), _const_spec((1, d_model)), _const_spec(w_main.shape), _const_spec(wvt.shape),
                _const_spec(wwt.shape), _const_spec((n_slots, WIDTH)), _const_spec((1, WIDTH)),
                _const_spec((1, WIDTH)), _const_spec((1, WIDTH)), _const_spec((WIDTH, WIDTH)),
                _const_spec((WIDTH, WIDTH))]
    return pl.pallas_call(
        functools.partial(_proj_body, d_model=d_model),
        out_shape=out_shape, grid=(rows // tm,), in_specs=in_specs, out_specs=out_specs,
        compiler_params=pltpu.CompilerParams(dimension_semantics=("parallel",), vmem_limit_bytes=VMEM_LIMIT),
        name="proj",
    )(x2, norm_in.astype(F32)[None, :], w_main, wvt, wwt, lower_bounds.astype(F32),
      tile(q_norm_a, N_HEADS_A), tile(k_norm_a, N_HEADS_A), tile(q_norm_c, N_HEADS_C),
      _block_ones(WIDTH, HEAD_DIM_A), _block_ones(WIDTH, HEAD_DIM_C))


def _memkv_body(m_ref, nm_ref, w_ref, gk_ref, ones128_ref, kc_ref, vc_ref):
    x = m_ref[...]
    ms = jnp.mean(x * x, axis=-1, keepdims=True)
    hb = ((x * lax.rsqrt(ms + EPS)) * nm_ref[...]).astype(BF16)
    kc = _dot(hb, w_ref[:, 0:WIDTH])
    kc = kc * lax.rsqrt(_group_meansq(kc, ones128_ref, HEAD_DIM_C) + EPS) * gk_ref[...]
    kc_ref[...] = kc.astype(BF16)
    vc_ref[...] = _dot(hb, w_ref[:, WIDTH:2 * WIDTH]).astype(BF16)


def _memkv_call(mem2, norm_mem, w_mem_kv, k_norm_c, tm):
    rows, d_model = mem2.shape
    row_spec = lambda w: pl.BlockSpec((tm, w), lambda i: (i, 0))
    return pl.pallas_call(
        _memkv_body,
        out_shape=[jax.ShapeDtypeStruct((rows, WIDTH), BF16)] * 2,
        grid=(rows // tm,),
        in_specs=[row_spec(d_model), _const_spec((1, d_model)), _const_spec((d_model, 2 * WIDTH)),
                  _const_spec((1, WIDTH)), _const_spec((WIDTH, WIDTH))],
        out_specs=[row_spec(WIDTH), row_spec(WIDTH)],
        compiler_params=pltpu.CompilerParams(dimension_semantics=("parallel",), vmem_limit_bytes=VMEM_LIMIT),
        name="memkv",
    )(mem2, norm_mem.astype(F32)[None, :], w_mem_kv.astype(BF16),
      jnp.tile(k_norm_c.astype(F32), N_HEADS_C)[None, :], _block_ones(WIDTH, HEAD_DIM_C))


def _key_to_f32(k):
    return pltpu.bitcast(jnp.where(k < 0, k ^ jnp.int32(0x7FFFFFFF), k), F32)


def _slope_log2(h):
    return (2.0 ** (-8.0 * (h + 1) / N_HEADS_A)) * LOG2E


N_SLOPE_PARTS = 3


def _slope_rows(h, row):
    rest = np.float32(_slope_log2(h))
    out = jnp.zeros(row.shape, F32)
    for r in range(N_SLOPE_PARTS):
        part = np.float32(np.asarray(rest, dtype=jnp.bfloat16))
        out = jnp.where(row == r, float(part), out)
        rest = np.float32(rest - part)
    return out


def _dsa_body(qa_ref, qi_ref, wit_ref, sza_ref, ka_ref, vat_ref, ki2_ref, out_ref,
              sc_ref, qir_ref, qar_ref, m_ref, l_ref, alpha_ref, acc_ref, pbuf_ref, sbuf_ref, ot_ref,
              *, topk, pos_bits):
    i = pl.program_id(1)
    q0 = i * TQ
    n_att = (q0 + TQ + TKS - 1) // TKS
    n_cnt = (q0 + TQ + TKC - 1) // TKC
    n_sc = n_cnt * (TKC // TKS)

    row = lax.broadcasted_iota(I32, (LANES, TQ), 0)
    row_lo = row < HEAD_DIM_A
    pairs = range(N_HEADS_A // 2)

    def pair_operand(ref, p):
        t = ref[:, p * LANES:(p + 1) * LANES].astype(F32).T
        return jnp.concatenate([jnp.where(row_lo, t, 0.0), jnp.where(row_lo, 0.0, t)], axis=1).astype(BF16)

    for p in pairs:
        qir_ref[p] = pair_operand(qi_ref, p)

    def index_dots(jt):
        kt = ki2_ref[pl.ds(pl.multiple_of(jt * TKS, TKS), TKS), :]
        return [_dot(kt, qir_ref[p]) for p in pairs]

    first = index_dots(0)
    for p in pairs:
        qar_ref[p, 0:LANES, :] = pair_operand(qa_ref, p)
        qar_ref[p, LANES:2 * LANES, :] = jnp.concatenate(
            [_slope_rows(2 * p, row), _slope_rows(2 * p + 1, row)], axis=1).astype(BF16)
    m_ref[...] = jnp.full(m_ref.shape, NEG_BIG, F32)
    l_ref[...] = jnp.zeros(l_ref.shape, F32)
    acc_ref[...] = jnp.zeros(acc_ref.shape, F32)
    alpha_ref[...] = jnp.ones(alpha_ref.shape, F32)
    pbuf_ref[...] = jnp.zeros(pbuf_ref.shape, BF16)
    for p in pairs:
        sbuf_ref[p] = first[p]

    row_s = lax.broadcasted_iota(I32, (TKS, TQ), 0)
    tpos = q0 + lax.broadcasted_iota(I32, (TKS, TQ), 1)

    def score_tile(j, carry):
        r0 = pl.multiple_of(j * TKS, TKS)
        ahead = index_dots(jnp.minimum(j + 1, n_sc - 1))
        acc = jnp.zeros((TKS, TQ), F32)
        for p in pairs:
            rel = jnp.maximum(sbuf_ref[p], 0.0)
            acc = acc + rel[:, :TQ] * wit_ref[2 * p:2 * p + 1, :] + rel[:, TQ:] * wit_ref[2 * p + 1:2 * p + 2, :]
        sc = acc * IDX_SCALE
        sc = jnp.where(sc == 0.0, 0.0, sc)
        sc_ref[pl.ds(r0, TKS), :] = jnp.where(r0 + row_s <= tpos, sc, -jnp.inf)
        for p in pairs:
            sbuf_ref[p] = ahead[p]
        return carry

    lax.fori_loop(0, n_sc, score_tile, 0)

    row_c = lax.broadcasted_iota(I32, (TKC, TQ), 0)

    def count(pred):
        def tile(j, acc):
            r0 = pl.multiple_of(j * TKC, TKC)
            c = jnp.where(pred(sc_ref[pl.ds(r0, TKC), :], r0), jnp.int32(1), jnp.int32(0))
            return acc + jnp.sum(c.reshape(TKC // 8, 8, TQ), axis=0)
        acc = lax.fori_loop(0, n_cnt, tile, jnp.zeros((8, TQ), I32))
        return jnp.sum(acc, axis=0, keepdims=True)

    def bisect(it, p):
        cand = p ^ lax.shift_left(jnp.int32(1), 31 - it)
        cand_f = _key_to_f32(cand)
        n = count(lambda s, r0: s >= cand_f)
        return jnp.where(n >= topk, cand, p)

    kth = lax.fori_loop(0, 32, bisect, jnp.full((1, TQ), INT_MIN, I32))
    kth = jnp.maximum(kth, jnp.int32(KEY_LOWEST))
    thr = _key_to_f32(kth)
    nxt = _key_to_f32(kth + 1)
    n_gt = count(lambda s, r0: s >= nxt)
    n_ge = count(lambda s, r0: s >= thr)
    need = topk - n_gt
    has_tie = n_ge > topk
    any_tie = jnp.max(jnp.where(has_tie, 1, 0)) > 0

    lane_a = lax.broadcasted_iota(I32, (TKS, LANES), 1)
    aug = jnp.where(lane_a < N_SLOPE_PARTS, lax.broadcasted_iota(I32, (TKS, LANES), 0).astype(F32), 0.0).astype(BF16)
    half2 = lax.broadcasted_iota(I32, (1, 2 * TQ), 1) < TQ

    def value_update(jt):
        for p in pairs:
            vt = jnp.concatenate([vat_ref[jt * (TKS // LANES) + t, p * LANES:(p + 1) * LANES, :]
                                  for t in range(TKS // LANES)], axis=1)
            acc_ref[p] = alpha_ref[p] * acc_ref[p] + _dot(vt, pbuf_ref[p])

    def score_dots(jt):
        r0 = pl.multiple_of(jt * TKS, TKS)
        return [_dot(jnp.concatenate([ka_ref[pl.ds(r0, TKS), p * LANES:(p + 1) * LANES], aug], axis=1), qar_ref[p])
                for p in pairs]

    for p, s in zip(pairs, score_dots(0)):
        sbuf_ref[p] = s

    def tie_cut():
        def step(it, jp):
            cand = jp | lax.shift_left(jnp.int32(1), pos_bits - 1 - it)
            n = count(lambda s, r0: (s >= thr) & (s < nxt) & (r0 + row_c < cand))
            return jnp.where(n < need, cand, jp)
        return lax.fori_loop(0, pos_bits, step, jnp.zeros((1, TQ), I32))

    cut = lax.cond(any_tie, tie_cut, lambda: jnp.zeros((1, TQ), I32))
    cut = jnp.where(has_tie, cut, jnp.int32(2 ** 30))

    def att_tile(j, carry):
        r0 = pl.multiple_of(j * TKS, TKS)
        s_idx = sc_ref[pl.ds(r0, TKS), :]
        sel = (s_idx >= thr) & ((s_idx >= nxt) | (r0 + row_s <= cut))
        bias = jnp.where(sel, 0.0, NEG_BIG)
        bias2 = jnp.concatenate([bias, bias], axis=1)
        base = (r0 - q0).astype(F32)
        value_update(jnp.maximum(j - 1, 0))
        ahead = score_dots(jnp.minimum(j + 1, n_att - 1))
        for p in pairs:
            shift = jnp.where(half2, _slope_log2(2 * p), _slope_log2(2 * p + 1)) * base
            s = sbuf_ref[p] + bias2
            m_old = m_ref[p]
            m_new = jnp.maximum(m_old, jnp.max(s, axis=0, keepdims=True) + shift)
            alpha = jnp.exp2(m_old - m_new)
            pr = jnp.exp2(s - (m_new - shift))
            l_ref[p] = alpha * l_ref[p] + jnp.sum(pr, axis=0, keepdims=True)
            m_ref[p] = m_new
            alpha_ref[p] = alpha
            pbuf_ref[p] = pr.astype(BF16)
        for p in pairs:
            sbuf_ref[p] = ahead[p]
        return carry

    lax.fori_loop(0, n_att, att_tile, 0)
    value_update(n_att - 1)

    row_lo = lax.broadcasted_iota(I32, (LANES, TQ), 0) < HEAD_DIM_A
    for p in range(N_HEADS_A // 2):
        o = acc_ref[p] * (1.0 / l_ref[p])
        ot_ref[p * LANES:(p + 1) * LANES, :] = jnp.where(row_lo, o[:, :TQ], o[:, TQ:])
    out_ref[...] = (ot_ref[...].T * sza_ref[...].astype(F32)).astype(BF16)


def _dsa_call(qa, qi, wit, sza, ka, vat, ki2, batch, seq):
    assert TQ == LANES and TKS % LANES == 0 and TKC % TKS == 0 and seq % TKC == 0
    nq = seq // TQ
    topk = min(TOPK_MAX, seq // 4)
    pos_bits = max(1, int(np.ceil(np.log2(seq))))
    q_spec = lambda w: pl.BlockSpec((TQ, w), lambda b, i: (b * nq + i, 0))
    b_spec = lambda w: pl.BlockSpec((seq, w), lambda b, i: (b, 0))
    n_pair = N_HEADS_A // 2
    return pl.pallas_call(
        functools.partial(_dsa_body, topk=topk, pos_bits=pos_bits),
        out_shape=jax.ShapeDtypeStruct((batch * seq, WIDTH), BF16),
        grid=(batch, nq),
        in_specs=[q_spec(WIDTH), q_spec(WIDTH), pl.BlockSpec((16, TQ), lambda b, i: (0, b * nq + i)), q_spec(WIDTH),
                  b_spec(WIDTH), pl.BlockSpec((seq // LANES, WIDTH, LANES), lambda b, i: (b, 0, 0)), b_spec(LANES)],
        out_specs=q_spec(WIDTH),
        scratch_shapes=[pltpu.VMEM((seq, TQ), F32),
                        pltpu.VMEM((n_pair, LANES, 2 * TQ), BF16), pltpu.VMEM((n_pair, 2 * LANES, 2 * TQ), BF16),
                        pltpu.VMEM((n_pair, 1, 2 * TQ), F32), pltpu.VMEM((n_pair, 1, 2 * TQ), F32),
                        pltpu.VMEM((n_pair, 1, 2 * TQ), F32), pltpu.VMEM((n_pair, LANES, 2 * TQ), F32),
                        pltpu.VMEM((n_pair, TKS, 2 * TQ), BF16), pltpu.VMEM((n_pair, TKS, 2 * TQ), F32),
                        pltpu.VMEM((WIDTH, TQ), F32)],
        compiler_params=pltpu.CompilerParams(dimension_semantics=("parallel", "arbitrary"),
                                             vmem_limit_bytes=VMEM_LIMIT),
        name="dsa",
    )(qa, qi, wit, sza, ka, vat, ki2)


def _split3(v):
    hi = v.astype(BF16)
    r = v - hi.astype(F32)
    mid = r.astype(BF16)
    lo = (r - mid.astype(F32)).astype(BF16)
    return hi, mid, lo


def _hgrn_body(sqb_ref, logf_ref, kk_ref, ib_ref, ibt_ref, sgb_ref, gon_ref, out_ref, st_ref):
    @pl.when(pl.program_id(1) == 0)
    def _():
        st_ref[...] = jnp.zeros_like(st_ref)

    r_i = lax.broadcasted_iota(I32, (CHUNK, CHUNK), 0)
    c_i = lax.broadcasted_iota(I32, (CHUNK, CHUNK), 1)
    tril = c_i <= r_i
    tri = jnp.where(tril, 1.0, 0.0).astype(BF16)
    n_sub = CHUNK // SUB
    row_c = lax.broadcasted_iota(I32, (CHUNK, LANES), 0)
    row_t = lax.broadcasted_iota(I32, (T_HGRN, LANES), 0)

    for c in range(T_HGRN // CHUNK):
        rows = slice(c * CHUNK, (c + 1) * CHUNK)
        g = logf_ref[rows, :]
        b = sum(_dot(tri, part) for part in _split3(g))
        q = sqb_ref[rows, :].astype(F32)
        kk = kk_ref[rows, :].astype(F32)
        b_last = b[CHUNK - 1:CHUNK, :]
        ref_k = jnp.concatenate(
            [jnp.broadcast_to(b[(j + 1) * SUB - 1:(j + 1) * SUB, :], (SUB, WIDTH)) for j in range(n_sub)], axis=0)
        k_in = kk * jnp.exp(ref_k - b)
        q_out = (q * jnp.exp(b)).astype(BF16)
        k_st = (kk * jnp.exp(b_last - b)).astype(BF16)
        q_in = []
        for j in range(n_sub):
            lo = j * SUB
            qj = q[lo:, :] * jnp.exp(b[lo:, :] - b[lo + SUB - 1:lo + SUB, :])
            if lo:
                qj = jnp.concatenate([jnp.zeros((lo, WIDTH), F32), qj], axis=0)
            q_in.append(qj.astype(BF16))
        for hd in range(N_HEADS_B):
            cols = slice(hd * HEAD_B, (hd + 1) * HEAD_B)
            qs = jnp.concatenate([qj[:, cols] for qj in q_in], axis=1)
            kh = k_in[:, cols]
            ks = jnp.concatenate(
                [jnp.where((row_c >= j * SUB) & (row_c < (j + 1) * SUB), kh, 0.0) for j in range(n_sub)],
                axis=1).astype(BF16)
            a = jnp.where(tril, _dot_nt(qs, ks), 0.0).astype(BF16)
            st = st_ref[hd]
            o = _dot(a, ib_ref[rows, cols]) + _dot_nt(q_out[:, cols], st.astype(BF16))
            k_pad = jnp.where((row_t >= c * CHUNK) & (row_t < (c + 1) * CHUNK),
                              jnp.concatenate([k_st[:, cols]] * (T_HGRN // CHUNK), axis=0), jnp.zeros((), BF16))
            st_ref[hd] = st * jnp.exp(b_last[:, cols]) + _dot(ibt_ref[0, cols, :], k_pad)
            ms = jnp.mean(o * o, axis=-1, keepdims=True)
            o = o * lax.rsqrt(ms + EPS) * gon_ref[...]
            out_ref[rows, cols] = (o * sgb_ref[rows, cols].astype(F32)).astype(BF16)


def _hgrn_call(sqb, logf, kk, ib, ibt, sgb, o_norm_b, batch, seq):
    assert T_HGRN == LANES and seq % T_HGRN == 0
    nt = seq // T_HGRN
    spec = pl.BlockSpec((T_HGRN, WIDTH), lambda b, t: (b * nt + t, 0))
    ibt_spec = pl.BlockSpec((1, WIDTH, T_HGRN), lambda b, t: (b * nt + t, 0, 0))
    return pl.pallas_call(
        _hgrn_body,
        out_shape=jax.ShapeDtypeStruct((batch * seq, WIDTH), BF16),
        grid=(batch, nt),
        in_specs=[spec, spec, spec, spec, ibt_spec, spec, _const_spec((1, HEAD_B))],
        out_specs=spec,
        scratch_shapes=[pltpu.VMEM((N_HEADS_B, HEAD_B, HEAD_B), F32)],
        compiler_params=pltpu.CompilerParams(dimension_semantics=("parallel", "arbitrary"),
                                             vmem_limit_bytes=VMEM_LIMIT),
        name="hgrn",
    )(sqb, logf, kk, ib, ibt, sgb, o_norm_b.astype(F32)[None, :])


def _merge_body(x_ref, oa_ref, ob_ref, qc_ref, szc_ref, gates_ref, kc_ref, vc_ref, wb_ref, wo_ref, out_ref, *, d_model):
    oc = []
    for hd in range(N_HEADS_C):
        cols = slice(hd * HEAD_DIM_C, (hd + 1) * HEAD_DIM_C)
        s = _dot_nt(qc_ref[:, cols], kc_ref[:, cols])
        m = jnp.max(s, axis=-1, keepdims=True)
        p = jnp.exp(s - m)
        l = jnp.sum(p, axis=-1, keepdims=True)
        oc.append(_dot(p.astype(BF16), vc_ref[:, cols]) * (1.0 / l))
    oc = (jnp.concatenate(oc, axis=1) * szc_ref[...].astype(F32)).astype(BF16)
    merged = jnp.zeros((x_ref.shape[0], d_model), F32)
    for n, br in enumerate((oa_ref[...], ob_ref[...], oc)):
        y = _dot(br, wb_ref[n])
        merged = merged + gates_ref[:, n * d_model:(n + 1) * d_model].astype(F32) * y
    out_ref[...] = x_ref[...] + _dot(merged.astype(BF16), wo_ref[...])


def _merge_call(x2, oa, ob, qc, szc, gates, kc, vc, w_branch, w_out, batch, seq, mem_len):
    rows, d_model = x2.shape
    tm = TM_MERGE
    nt = seq // tm
    row_spec = lambda w: pl.BlockSpec((tm, w), lambda b, t: (b * nt + t, 0))
    mem_spec = pl.BlockSpec((mem_len, WIDTH), lambda b, t: (b, 0))
    return pl.pallas_call(
        functools.partial(_merge_body, d_model=d_model),
        out_shape=jax.ShapeDtypeStruct((rows, d_model), F32),
        grid=(batch, nt),
        in_specs=[row_spec(d_model), row_spec(WIDTH), row_spec(WIDTH), row_spec(WIDTH), row_spec(WIDTH),
                  row_spec(N_BRANCH * d_model), mem_spec, mem_spec,
                  _const_spec((N_BRANCH, WIDTH, d_model)), _const_spec((d_model, d_model))],
        out_specs=row_spec(d_model),
        compiler_params=pltpu.CompilerParams(dimension_semantics=("parallel", "parallel"),
                                             vmem_limit_bytes=VMEM_LIMIT),
        name="merge",
    )(x2, oa, ob, qc, szc, gates, kc, vc, w_branch.astype(BF16), w_out.astype(BF16))


def _layer(x, mem, norm_in, norm_mem, w_in, q_norm_a, k_norm_a, lower_bounds, o_norm_b, w_mem_kv, q_norm_c, k_norm_c,
           w_branch, w_out):
    batch, seq, d_model = x.shape
    mem_len = mem.shape[1]
    x2 = x.reshape(batch * seq, d_model)
    (qa, ka, vat, sza, qi, ki2, wit, sqb, logf, kk, ib, ibt, sgb, qc, szc, gates) = _proj_call(
        x2, norm_in, w_in, lower_bounds, q_norm_a, k_norm_a, q_norm_c)
    kc, vc = _memkv_call(mem.reshape(batch * mem_len, d_model), norm_mem, w_mem_kv, k_norm_c, mem_len)
    oa = _dsa_call(qa, qi, wit, sza, ka, vat, ki2, batch, seq)
    ob = _hgrn_call(sqb, logf, kk, ib, ibt, sgb, o_norm_b, batch, seq)
    out = _merge_call(x2, oa, ob, qc, szc, gates, kc, vc, w_branch, w_out, batch, seq, mem_len)
    return out.reshape(batch, seq, d_model)


def kernel(x, mem, norm_in, norm_mem, w_in, q_norm_a, k_norm_a, lower_bounds, o_norm_b, w_mem_kv, q_norm_c, k_norm_c,
           w_branch, w_out):
    assert norm_in.shape[0] == 1, "single-layer block"
    return _layer(x, mem, norm_in[0], norm_mem[0], w_in[0], q_norm_a[0], k_norm_a[0], lower_bounds, o_norm_b[0],
                  w_mem_kv[0], q_norm_c[0], k_norm_c[0], w_branch[0], w_out[0])
```

```python
import functools

import jax
import jax.numpy as jnp
import numpy as np
from jax import lax
from jax.experimental import pallas as pl
from jax.experimental.pallas import tpu as pltpu

F32 = jnp.float32
BF16 = jnp.bfloat16
I32 = jnp.int32

N_HEADS_A = 8
HEAD_DIM_A = 64
N_IDX_HEADS = 8
IDX_DIM = 64
TOPK_MAX = 256
N_HEADS_B = 4
HEAD_B = 128
N_HEADS_C = 4
HEAD_DIM_C = 128
N_BRANCH = 3
EPS = 1e-6
WIDTH = 512
IDX_SCALE = (IDX_DIM ** -0.5) * (N_IDX_HEADS ** -0.5)

LANES = 128
VMEM_LIMIT = 56 * 1024 * 1024

TM_PROJ = 256
TQ = 128
TKS = 256
TKC = 512
CHUNK = 64
SUB = 16
T_HGRN = 128
NB_HGRN = 4
TM_MERGE = 256

NEG_BIG = -1e30
LOG2E = 1.4426950408889634
INT_MIN = -2147483648
KEY_LOWEST = -2139095040

NT = (((1,), (1,)), ((), ()))


def _dot(a, b):
    return jnp.dot(a, b, preferred_element_type=F32)


def _dot_nt(a, b):
    return lax.dot_general(a, b, NT, preferred_element_type=F32)


def _group_meansq(v, ones_ref, group):
    sq = v * v
    hi = sq.astype(BF16)
    lo = (sq - hi.astype(F32)).astype(BF16)
    s = _dot(hi, ones_ref[...]) + _dot(lo, ones_ref[...])
    return s * (1.0 / group)


_C_QA, _C_KA, _C_ZA, _C_QB, _C_FB, _C_IB, _C_GB, _C_QC, _C_ZC = [WIDTH * i for i in range(9)]
_C_GATES = 9 * WIDTH
_C_QI = _C_GATES + 3 * 1024
_C_KI = _C_QI + WIDTH
_C_END = _C_KI + LANES


def _proj_body(x_ref, nin_ref, w_ref, wvt_ref, wwt_ref, lbp_ref, gqa_ref, gka_ref, gqc_ref, ones64_ref, ones128_ref,
               qa_ref, ka_ref, vat_ref, sza_ref, qi_ref, ki2_ref, wit_ref,
               sqb_ref, logf_ref, kk_ref, ib_ref, ibt_ref, sgb_ref, qc_ref, szc_ref, gates_ref, *, d_model):
    x = x_ref[...]
    ms = jnp.mean(x * x, axis=-1, keepdims=True)
    h = (x * lax.rsqrt(ms + EPS)) * nin_ref[...]
    hb = h.astype(BF16)

    def proj(c0, width):
        return lambda: _dot(hb, w_ref[:, c0:c0 + width])

    def head_norm(ones_ref, group, gain_ref, scale, out_ref):
        def epilogue(v):
            v = v * lax.rsqrt(_group_meansq(v, ones_ref, group) + EPS) * gain_ref[...]
            out_ref[...] = (v * scale).astype(BF16)
        return epilogue

    def store(out_ref, fn=lambda v: v):
        def epilogue(v):
            out_ref[...] = fn(v).astype(out_ref.dtype)
        return epilogue

    def store_time_blocks(out_ref):
        def epilogue(v):
            v = v.astype(BF16)
            for t in range(out_ref.shape[0]):
                out_ref[t] = v[:, t * LANES:(t + 1) * LANES]
        return epilogue

    def forget(v):
        lbp = lbp_ref[...]
        lbe = jnp.exp(lbp - jnp.max(lbp, axis=0, keepdims=True))
        lb = lbe[0:1, :] / jnp.sum(lbe, axis=0, keepdims=True)
        f = lb + (1.0 - lb) * jax.nn.sigmoid(v)
        logf_ref[...] = jnp.log(f)
        kk_ref[...] = (1.0 - f).astype(BF16)

    def gate(n):
        def epilogue(v):
            gates_ref[:, n * d_model:(n + 1) * d_model] = jax.nn.sigmoid(v).astype(BF16)
        return epilogue

    stages = [
        (proj(_C_QA, WIDTH), head_norm(ones64_ref, HEAD_DIM_A, gqa_ref, HEAD_DIM_A ** -0.5 * LOG2E, qa_ref)),
        (proj(_C_KA, WIDTH), head_norm(ones64_ref, HEAD_DIM_A, gka_ref, 1.0, ka_ref)),
        (lambda: _dot_nt(wvt_ref[0:WIDTH, :], hb), store_time_blocks(vat_ref)),
        (proj(_C_ZA, WIDTH), store(sza_ref, jax.nn.silu)),
        (proj(_C_QI, WIDTH), store(qi_ref)),
        (proj(_C_KI, LANES), store(ki2_ref)),
        (lambda: _dot_nt(wwt_ref[...], hb), store(wit_ref)),
        (proj(_C_QB, WIDTH), store(sqb_ref, jax.nn.silu)),
        (proj(_C_FB, WIDTH), forget),
        (proj(_C_IB, WIDTH), store(ib_ref)),
        (lambda: _dot_nt(wvt_ref[WIDTH:2 * WIDTH, :], hb), store_time_blocks(ibt_ref)),
        (proj(_C_GB, WIDTH), store(sgb_ref, jax.nn.silu)),
        (proj(_C_QC, WIDTH), head_norm(ones128_ref, HEAD_DIM_C, gqc_ref, HEAD_DIM_C ** -0.5, qc_ref)),
        (proj(_C_ZC, WIDTH), store(szc_ref, jax.nn.silu)),
    ] + [(proj(_C_GATES + n * d_model, d_model), gate(n)) for n in range(N_BRANCH)]
    pending = stages[0][0]()
    for k, (_, epilogue) in enumerate(stages):
        ahead = stages[k + 1][0]() if k + 1 < len(stages) else None
        epilogue(pending)
        pending = ahead


def _const_spec(shape):
    nd = len(shape)
    return pl.BlockSpec(shape, lambda *_: (0,) * nd, pipeline_mode=pl.Buffered(1))


def _block_ones(width, group):
    g = np.arange(width) // group
    return jnp.asarray((g[:, None] == g[None, :]).astype(np.float32), dtype=BF16)


def _proj_call(x2, norm_in, w_in, lower_bounds, q_norm_a, k_norm_a, q_norm_c):
    rows, d_model = x2.shape
    tm = TM_PROJ
    assert rows % tm == 0 and d_model == 1024
    offs = np.cumsum([0, 512, 512, 512, 512, 512, 64, 8, 512, 512, 512, 512, 512, 512, 3 * d_model])
    (o_qa, o_ka, o_va, o_za, o_qi, o_ki, o_wi, o_qb, o_fb, o_ib, o_gb, o_qc, o_zc, o_g, o_end) = [int(o) for o in offs]
    assert o_end == w_in.shape[1]
    wb = w_in.astype(BF16)
    col = lambda a, b: wb[:, a:b]
    w_main = jnp.concatenate([
        col(o_qa, o_ka), col(o_ka, o_va), col(o_za, o_qi), col(o_qb, o_fb), col(o_fb, o_ib), col(o_ib, o_gb),
        col(o_gb, o_qc), col(o_qc, o_zc), col(o_zc, o_g), col(o_g, o_end), col(o_qi, o_ki),
        col(o_ki, o_wi), col(o_ki, o_wi)], axis=1)
    assert w_main.shape[1] == _C_END
    wvt = jnp.concatenate([col(o_va, o_za).T, col(o_ib, o_gb).T], axis=0)
    wwt = jnp.concatenate([col(o_wi, o_qb).T, jnp.zeros((8, d_model), BF16)], axis=0)
    tile = lambda g, reps: jnp.tile(g.astype(F32), reps)[None, :]
    n_slots = lower_bounds.shape[0]

    row_spec = lambda w: pl.BlockSpec((tm, w), lambda i: (i, 0))
    t_spec = pl.BlockSpec((tm // LANES, WIDTH, LANES), lambda i: (i, 0, 0))
    out_shape = [
        jax.ShapeDtypeStruct((rows, WIDTH), BF16),
        jax.ShapeDtypeStruct((rows, WIDTH), BF16),
        jax.ShapeDtypeStruct((rows // LANES, WIDTH, LANES), BF16),
        jax.ShapeDtypeStruct((rows, WIDTH), BF16),
        jax.ShapeDtypeStruct((rows, WIDTH), BF16),
        jax.ShapeDtypeStruct((rows, LANES), BF16),
        jax.ShapeDtypeStruct((16, rows), F32),
        jax.ShapeDtypeStruct((rows, WIDTH), BF16),
        jax.ShapeDtypeStruct((rows, WIDTH), F32),
        jax.ShapeDtypeStruct((rows, WIDTH), BF16),
        jax.ShapeDtypeStruct((rows, WIDTH), BF16),
        jax.ShapeDtypeStruct((rows // LANES, WIDTH, LANES), BF16),
        jax.ShapeDtypeStruct((rows, WIDTH), BF16),
        jax.ShapeDtypeStruct((rows, WIDTH), BF16),
        jax.ShapeDtypeStruct((rows, WIDTH), BF16),
        jax.ShapeDtypeStruct((rows, N_BRANCH * d_model), BF16),
    ]
    out_specs = [row_spec(WIDTH), row_spec(WIDTH), t_spec, row_spec(WIDTH), row_spec(WIDTH), row_spec(LANES),
                 pl.BlockSpec((16, tm), lambda i: (0, i)),
                 row_spec(WIDTH), row_spec(WIDTH), row_spec(WIDTH), row_spec(WIDTH), t_spec, row_spec(WIDTH),
                 row_spec(WIDTH), row_spec(WIDTH), row_spec(N_BRANCH * d_model)]
    in_specs = [row_spec(d_model), _const_spec((1, d_model)), _const_spec(w_main.shape), _const_spec(wvt.shape),
                _const_spec(wwt.shape), _const_spec((n_slots, WIDTH)), _const_spec((1, WIDTH)),
                _const_spec((1, WIDTH)), _const_spec((1, WIDTH)), _const_spec((WIDTH, WIDTH)),
                _const_spec((WIDTH, WIDTH))]
    return pl.pallas_call(
        functools.partial(_proj_body, d_model=d_model),
        out_shape=out_shape, grid=(rows // tm,), in_specs=in_specs, out_specs=out_specs,
        compiler_params=pltpu.CompilerParams(dimension_semantics=("parallel",), vmem_limit_bytes=VMEM_LIMIT),
        name="proj",
    )(x2, norm_in.astype(F32)[None, :], w_main, wvt, wwt, lower_bounds.astype(F32),
      tile(q_norm_a, N_HEADS_A), tile(k_norm_a, N_HEADS_A), tile(q_norm_c, N_HEADS_C),
      _block_ones(WIDTH, HEAD_DIM_A), _block_ones(WIDTH, HEAD_DIM_C))


def _memkv_body(m_ref, nm_ref, w_ref, gk_ref, ones128_ref, kc_ref, vc_ref):
    x = m_ref[...]
    ms = jnp.mean(x * x, axis=-1, keepdims=True)
    hb = ((x * lax.rsqrt(ms + EPS)) * nm_ref[...]).astype(BF16)
    kc = _dot(hb, w_ref[:, 0:WIDTH])
    kc = kc * lax.rsqrt(_group_meansq(kc, ones128_ref, HEAD_DIM_C) + EPS) * gk_ref[...]
    kc_ref[...] = kc.astype(BF16)
    vc_ref[...] = _dot(hb, w_ref[:, WIDTH:2 * WIDTH]).astype(BF16)


def _memkv_call(mem2, norm_mem, w_mem_kv, k_norm_c, tm):
    rows, d_model = mem2.shape
    row_spec = lambda w: pl.BlockSpec((tm, w), lambda i: (i, 0))
    return pl.pallas_call(
        _memkv_body,
        out_shape=[jax.ShapeDtypeStruct((rows, WIDTH), BF16)] * 2,
        grid=(rows // tm,),
        in_specs=[row_spec(d_model), _const_spec((1, d_model)), _const_spec((d_model, 2 * WIDTH)),
                  _const_spec((1, WIDTH)), _const_spec((WIDTH, WIDTH))],
        out_specs=[row_spec(WIDTH), row_spec(WIDTH)],
        compiler_params=pltpu.CompilerParams(dimension_semantics=("parallel",), vmem_limit_bytes=VMEM_LIMIT),
        name="memkv",
    )(mem2, norm_mem.astype(F32)[None, :], w_mem_kv.astype(BF16),
      jnp.tile(k_norm_c.astype(F32), N_HEADS_C)[None, :], _block_ones(WIDTH, HEAD_DIM_C))


def _key_to_f32(k):
    return pltpu.bitcast(jnp.where(k < 0, k ^ jnp.int32(0x7FFFFFFF), k), F32)


def _slope_log2(h):
    return (2.0 ** (-8.0 * (h + 1) / N_HEADS_A)) * LOG2E


N_SLOPE_PARTS = 3


def _slope_rows(h, row):
    rest = np.float32(_slope_log2(h))
    out = jnp.zeros(row.shape, F32)
    for r in range(N_SLOPE_PARTS):
        part = np.float32(np.asarray(rest, dtype=jnp.bfloat16))
        out = jnp.where(row == r, float(part), out)
        rest = np.float32(rest - part)
    return out


def _dsa_body(qa_ref, qi_ref, wit_ref, sza_ref, ka_ref, vat_ref, ki2_ref, out_ref,
              sc_ref, qir_ref, qar_ref, m_ref, l_ref, alpha_ref, acc_ref, pbuf_ref, sbuf_ref, ot_ref,
              *, topk, pos_bits):
    i = pl.program_id(1)
    q0 = i * TQ
    n_att = (q0 + TQ + TKS - 1) // TKS
    n_cnt = (q0 + TQ + TKC - 1) // TKC
    n_sc = n_cnt * (TKC // TKS)

    row = lax.broadcasted_iota(I32, (LANES, TQ), 0)
    row_lo = row < HEAD_DIM_A
    pairs = range(N_HEADS_A // 2)

    def pair_operand(ref, p):
        t = ref[:, p * LANES:(p + 1) * LANES].astype(F32).T
        return jnp.concatenate([jnp.where(row_lo, t, 0.0), jnp.where(row_lo, 0.0, t)], axis=1).astype(BF16)

    for p in pairs:
        qir_ref[p] = pair_operand(qi_ref, p)

    def index_dots(jt):
        kt = ki2_ref[pl.ds(pl.multiple_of(jt * TKS, TKS), TKS), :]
        return [_dot(kt, qir_ref[p]) for p in pairs]

    first = index_dots(0)
    for p in pairs:
        qar_ref[p, 0:LANES, :] = pair_operand(qa_ref, p)
        qar_ref[p, LANES:2 * LANES, :] = jnp.concatenate(
            [_slope_rows(2 * p, row), _slope_rows(2 * p + 1, row)], axis=1).astype(BF16)
    m_ref[...] = jnp.full(m_ref.shape, NEG_BIG, F32)
    l_ref[...] = jnp.zeros(l_ref.shape, F32)
    acc_ref[...] = jnp.zeros(acc_ref.shape, F32)
    alpha_ref[...] = jnp.ones(alpha_ref.shape, F32)
    pbuf_ref[...] = jnp.zeros(pbuf_ref.shape, BF16)
    for p in pairs:
        sbuf_ref[p] = first[p]

    row_s = lax.broadcasted_iota(I32, (TKS, TQ), 0)
    tpos = q0 + lax.broadcasted_iota(I32, (TKS, TQ), 1)

    def score_tile(j, carry):
        r0 = pl.multiple_of(j * TKS, TKS)
        ahead = index_dots(jnp.minimum(j + 1, n_sc - 1))
        acc = jnp.zeros((TKS, TQ), F32)
        for p in pairs:
            rel = jnp.maximum(sbuf_ref[p], 0.0)
            acc = acc + rel[:, :TQ] * wit_ref[2 * p:2 * p + 1, :] + rel[:, TQ:] * wit_ref[2 * p + 1:2 * p + 2, :]
        sc = acc * IDX_SCALE
        sc = jnp.where(sc == 0.0, 0.0, sc)
        sc_ref[pl.ds(r0, TKS), :] = jnp.where(r0 + row_s <= tpos, sc, -jnp.inf)
        for p in pairs:
            sbuf_ref[p] = ahead[p]
        return carry

    lax.fori_loop(0, n_sc, score_tile, 0)

    row_c = lax.broadcasted_iota(I32, (TKC, TQ), 0)

    def count(pred):
        def tile(j, acc):
            r0 = pl.multiple_of(j * TKC, TKC)
            c = jnp.where(pred(sc_ref[pl.ds(r0, TKC), :], r0), jnp.int32(1), jnp.int32(0))
            return acc + jnp.sum(c.reshape(TKC // 8, 8, TQ), axis=0)
        acc = lax.fori_loop(0, n_cnt, tile, jnp.zeros((8, TQ), I32))
        return jnp.sum(acc, axis=0, keepdims=True)

    def bisect(it, p):
        cand = p ^ lax.shift_left(jnp.int32(1), 31 - it)
        cand_f = _key_to_f32(cand)
        n = count(lambda s, r0: s >= cand_f)
        return jnp.where(n >= topk, cand, p)

    kth = lax.fori_loop(0, 32, bisect, jnp.full((1, TQ), INT_MIN, I32))
    kth = jnp.maximum(kth, jnp.int32(KEY_LOWEST))
    thr = _key_to_f32(kth)
    nxt = _key_to_f32(kth + 1)
    n_gt = count(lambda s, r0: s >= nxt)
    n_ge = count(lambda s, r0: s >= thr)
    need = topk - n_gt
    has_tie = n_ge > topk
    any_tie = jnp.max(jnp.where(has_tie, 1, 0)) > 0

    lane_a = lax.broadcasted_iota(I32, (TKS, LANES), 1)
    aug = jnp.where(lane_a < N_SLOPE_PARTS, lax.broadcasted_iota(I32, (TKS, LANES), 0).astype(F32), 0.0).astype(BF16)
    half2 = lax.broadcasted_iota(I32, (1, 2 * TQ), 1) < TQ

    def value_update(jt):
        for p in pairs:
            vt = jnp.concatenate([vat_ref[jt * (TKS // LANES) + t, p * LANES:(p + 1) * LANES, :]
                                  for t in range(TKS // LANES)], axis=1)
            acc_ref[p] = alpha_ref[p] * acc_ref[p] + _dot(vt, pbuf_ref[p])

    def score_dots(jt):
        r0 = pl.multiple_of(jt * TKS, TKS)
        return [_dot(jnp.concatenate([ka_ref[pl.ds(r0, TKS), p * LANES:(p + 1) * LANES], aug], axis=1), qar_ref[p])
                for p in pairs]

    for p, s in zip(pairs, score_dots(0)):
        sbuf_ref[p] = s

    def tie_cut():
        def step(it, jp):
            cand = jp | lax.shift_left(jnp.int32(1), pos_bits - 1 - it)
            n = count(lambda s, r0: (s >= thr) & (s < nxt) & (r0 + row_c < cand))
            return jnp.where(n < need, cand, jp)
        return lax.fori_loop(0, pos_bits, step, jnp.zeros((1, TQ), I32))

    cut = lax.cond(any_tie, tie_cut, lambda: jnp.zeros((1, TQ), I32))
    cut = jnp.where(has_tie, cut, jnp.int32(2 ** 30))

    def att_tile(j, carry):
        r0 = pl.multiple_of(j * TKS, TKS)
        s_idx = sc_ref[pl.ds(r0, TKS), :]
        sel = (s_idx >= thr) & ((s_idx >= nxt) | (r0 + row_s <= cut))
        bias = jnp.where(sel, 0.0, NEG_BIG)
        bias2 = jnp.concatenate([bias, bias], axis=1)
        base = (r0 - q0).astype(F32)
        value_update(jnp.maximum(j - 1, 0))
        ahead = score_dots(jnp.minimum(j + 1, n_att - 1))
        for p in pairs:
            shift = jnp.where(half2, _slope_log2(2 * p), _slope_log2(2 * p + 1)) * base
            s = sbuf_ref[p] + bias2
            m_old = m_ref[p]
            m_new = jnp.maximum(m_old, jnp.max(s, axis=0, keepdims=True) + shift)
            alpha = jnp.exp2(m_old - m_new)
            pr = jnp.exp2(s - (m_new - shift))
            l_ref[p] = alpha * l_ref[p] + jnp.sum(pr, axis=0, keepdims=True)
            m_ref[p] = m_new
            alpha_ref[p] = alpha
            pbuf_ref[p] = pr.astype(BF16)
        for p in pairs:
            sbuf_ref[p] = ahead[p]
        return carry

    lax.fori_loop(0, n_att, att_tile, 0)
    value_update(n_att - 1)

    row_lo = lax.broadcasted_iota(I32, (LANES, TQ), 0) < HEAD_DIM_A
    for p in range(N_HEADS_A // 2):
        o = acc_ref[p] * (1.0 / l_ref[p])
        ot_ref[p * LANES:(p + 1) * LANES, :] = jnp.where(row_lo, o[:, :TQ], o[:, TQ:])
    out_ref[...] = (ot_ref[...].T * sza_ref[...].astype(F32)).astype(BF16)


def _dsa_call(qa, qi, wit, sza, ka, vat, ki2, batch, seq):
    assert TQ == LANES and TKS % LANES == 0 and TKC % TKS == 0 and seq % TKC == 0
    nq = seq // TQ
    topk = min(TOPK_MAX, seq // 4)
    pos_bits = max(1, int(np.ceil(np.log2(seq))))
    q_spec = lambda w: pl.BlockSpec((TQ, w), lambda b, i: (b * nq + i, 0))
    b_spec = lambda w: pl.BlockSpec((seq, w), lambda b, i: (b, 0))
    n_pair = N_HEADS_A // 2
    return pl.pallas_call(
        functools.partial(_dsa_body, topk=topk, pos_bits=pos_bits),
        out_shape=jax.ShapeDtypeStruct((batch * seq, WIDTH), BF16),
        grid=(batch, nq),
        in_specs=[q_spec(WIDTH), q_spec(WIDTH), pl.BlockSpec((16, TQ), lambda b, i: (0, b * nq + i)), q_spec(WIDTH),
                  b_spec(WIDTH), pl.BlockSpec((seq // LANES, WIDTH, LANES), lambda b, i: (b, 0, 0)), b_spec(LANES)],
        out_specs=q_spec(WIDTH),
        scratch_shapes=[pltpu.VMEM((seq, TQ), F32),
                        pltpu.VMEM((n_pair, LANES, 2 * TQ), BF16), pltpu.VMEM((n_pair, 2 * LANES, 2 * TQ), BF16),
                        pltpu.VMEM((n_pair, 1, 2 * TQ), F32), pltpu.VMEM((n_pair, 1, 2 * TQ), F32),
                        pltpu.VMEM((n_pair, 1, 2 * TQ), F32), pltpu.VMEM((n_pair, LANES, 2 * TQ), F32),
                        pltpu.VMEM((n_pair, TKS, 2 * TQ), BF16), pltpu.VMEM((n_pair, TKS, 2 * TQ), F32),
                        pltpu.VMEM((WIDTH, TQ), F32)],
        compiler_params=pltpu.CompilerParams(dimension_semantics=("parallel", "arbitrary"),
                                             vmem_limit_bytes=VMEM_LIMIT),
        name="dsa",
    )(qa, qi, wit, sza, ka, vat, ki2)


def _split3(v):
    hi = v.astype(BF16)
    r = v - hi.astype(F32)
    mid = r.astype(BF16)
    lo = (r - mid.astype(F32)).astype(BF16)
    return hi, mid, lo


def _hgrn_body(sqb_ref, logf_ref, kk_ref, ib_ref, ibt_ref, sgb_ref, gon_ref, out_ref, st_ref):
    @pl.when(pl.program_id(1) == 0)
    def _():
        st_ref[...] = jnp.zeros_like(st_ref)

    r_i = lax.broadcasted_iota(I32, (CHUNK, CHUNK), 0)
    c_i = lax.broadcasted_iota(I32, (CHUNK, CHUNK), 1)
    tril = c_i <= r_i
    tri = jnp.where(tril, 1.0, 0.0).astype(BF16)
    n_sub = CHUNK // SUB
    row_c = lax.broadcasted_iota(I32, (CHUNK, LANES), 0)
    row_t = lax.broadcasted_iota(I32, (T_HGRN, LANES), 0)

    def decays(nb, c):
        g = logf_ref[nb, c * CHUNK:(c + 1) * CHUNK, :]
        return sum(_dot(tri, part) for part in _split3(g))

    def operands(nb, c, b):
        rows = slice(c * CHUNK, (c + 1) * CHUNK)
        q = sqb_ref[nb, rows, :].astype(F32)
        kk = kk_ref[nb, rows, :].astype(F32)
        b_last = b[CHUNK - 1:CHUNK, :]
        ref_k = jnp.concatenate(
            [jnp.broadcast_to(b[(j + 1) * SUB - 1:(j + 1) * SUB, :], (SUB, WIDTH)) for j in range(n_sub)], axis=0)
        k_in = kk * jnp.exp(ref_k - b)
        q_out = (q * jnp.exp(b)).astype(BF16)
        k_st = (kk * jnp.exp(b_last - b)).astype(BF16)
        q_in = []
        for j in range(n_sub):
            lo = j * SUB
            qj = q[lo:, :] * jnp.exp(b[lo:, :] - b[lo + SUB - 1:lo + SUB, :])
            if lo:
                qj = jnp.concatenate([jnp.zeros((lo, WIDTH), F32), qj], axis=0)
            q_in.append(qj.astype(BF16))
        return q_in, k_in, q_out, k_st, jnp.exp(b_last)

    def intra_scores(hd, q_in, k_in):
        cols = slice(hd * HEAD_B, (hd + 1) * HEAD_B)
        qs = jnp.concatenate([qj[:, cols] for qj in q_in], axis=1)
        kh = k_in[:, cols]
        ks = jnp.concatenate(
            [jnp.where((row_c >= j * SUB) & (row_c < (j + 1) * SUB), kh, 0.0) for j in range(n_sub)],
            axis=1).astype(BF16)
        return _dot_nt(qs, ks)

    def outputs(nb, c, hd, a_raw, q_out, k_st, decay_last):
        rows = slice(c * CHUNK, (c + 1) * CHUNK)
        cols = slice(hd * HEAD_B, (hd + 1) * HEAD_B)
        a = jnp.where(tril, a_raw, 0.0).astype(BF16)
        st = st_ref[nb, hd]
        o = _dot(a, ib_ref[nb, rows, cols]) + _dot_nt(q_out[:, cols], st.astype(BF16))
        k_pad = jnp.where((row_t >= c * CHUNK) & (row_t < (c + 1) * CHUNK),
                          jnp.concatenate([k_st[:, cols]] * (T_HGRN // CHUNK), axis=0), jnp.zeros((), BF16))
        st_ref[nb, hd] = st * decay_last[:, cols] + _dot(ibt_ref[nb, 0, cols, :], k_pad)
        return o

    def finish(nb, c, hd, o):
        rows = slice(c * CHUNK, (c + 1) * CHUNK)
        cols = slice(hd * HEAD_B, (hd + 1) * HEAD_B)
        ms = jnp.mean(o * o, axis=-1, keepdims=True)
        o = o * lax.rsqrt(ms + EPS) * gon_ref[...]
        out_ref[nb, rows, cols] = (o * sgb_ref[nb, rows, cols].astype(F32)).astype(BF16)

    seqs, heads = range(NB_HGRN), range(N_HEADS_B)
    for c in range(T_HGRN // CHUNK):
        bs = [decays(nb, c) for nb in seqs]
        ops = [operands(nb, c, bs[nb]) for nb in seqs]
        raw = [[intra_scores(hd, ops[nb][0], ops[nb][1]) for hd in heads] for nb in seqs]
        outs = [[outputs(nb, c, hd, raw[nb][hd], ops[nb][2], ops[nb][3], ops[nb][4]) for hd in heads] for nb in seqs]
        for nb in seqs:
            for hd in heads:
                finish(nb, c, hd, outs[nb][hd])


def _hgrn_call(sqb, logf, kk, ib, ibt, sgb, o_norm_b, batch, seq):
    assert T_HGRN == LANES and seq % T_HGRN == 0 and batch % NB_HGRN == 0
    nt = seq // T_HGRN
    as3 = lambda a: a.reshape(batch, seq, WIDTH)
    spec = pl.BlockSpec((NB_HGRN, T_HGRN, WIDTH), lambda b, t: (b, t, 0))
    ibt_spec = pl.BlockSpec((NB_HGRN, 1, WIDTH, T_HGRN), lambda b, t: (b, t, 0, 0))
    out = pl.pallas_call(
        _hgrn_body,
        out_shape=jax.ShapeDtypeStruct((batch, seq, WIDTH), BF16),
        grid=(batch // NB_HGRN, nt),
        in_specs=[spec, spec, spec, spec, ibt_spec, spec, _const_spec((1, HEAD_B))],
        out_specs=spec,
        scratch_shapes=[pltpu.VMEM((NB_HGRN, N_HEADS_B, HEAD_B, HEAD_B), F32)],
        compiler_params=pltpu.CompilerParams(dimension_semantics=("parallel", "arbitrary"),
                                             vmem_limit_bytes=VMEM_LIMIT),
        name="hgrn",
    )(as3(sqb), as3(logf), as3(kk), as3(ib), ibt.reshape(batch, nt, WIDTH, T_HGRN), as3(sgb),
      o_norm_b.astype(F32)[None, :])
    return out.reshape(batch * seq, WIDTH)


def _merge_body(x_ref, oa_ref, ob_ref, qc_ref, szc_ref, gates_ref, kc_ref, vc_ref, wb_ref, wo_ref, out_ref, *, d_model):
    oc = []
    for hd in range(N_HEADS_C):
        cols = slice(hd * HEAD_DIM_C, (hd + 1) * HEAD_DIM_C)
        s = _dot_nt(qc_ref[:, cols], kc_ref[:, cols])
        m = jnp.max(s, axis=-1, keepdims=True)
        p = jnp.exp(s - m)
        l = jnp.sum(p, axis=-1, keepdims=True)
        oc.append(_dot(p.astype(BF16), vc_ref[:, cols]) * (1.0 / l))
    oc = (jnp.concatenate(oc, axis=1) * szc_ref[...].astype(F32)).astype(BF16)
    merged = jnp.zeros((x_ref.shape[0], d_model), F32)
    for n, br in enumerate((oa_ref[...], ob_ref[...], oc)):
        y = _dot(br, wb_ref[n])
        merged = merged + gates_ref[:, n * d_model:(n + 1) * d_model].astype(F32) * y
    out_ref[...] = x_ref[...] + _dot(merged.astype(BF16), wo_ref[...])


def _merge_call(x2, oa, ob, qc, szc, gates, kc, vc, w_branch, w_out, batch, seq, mem_len):
    rows, d_model = x2.shape
    tm = TM_MERGE
    nt = seq // tm
    row_spec = lambda w: pl.BlockSpec((tm, w), lambda b, t: (b * nt + t, 0))
    mem_spec = pl.BlockSpec((mem_len, WIDTH), lambda b, t: (b, 0))
    return pl.pallas_call(
        functools.partial(_merge_body, d_model=d_model),
        out_shape=jax.ShapeDtypeStruct((rows, d_model), F32),
        grid=(batch, nt),
        in_specs=[row_spec(d_model), row_spec(WIDTH), row_spec(WIDTH), row_spec(WIDTH), row_spec(WIDTH),
                  row_spec(N_BRANCH * d_model), mem_spec, mem_spec,
                  _const_spec((N_BRANCH, WIDTH, d_model)), _const_spec((d_model, d_model))],
        out_specs=row_spec(d_model),
        compiler_params=pltpu.CompilerParams(dimension_semantics=("parallel", "parallel"),
                                             vmem_limit_bytes=VMEM_LIMIT),
        name="merge",
    )(x2, oa, ob, qc, szc, gates, kc, vc, w_branch.astype(BF16), w_out.astype(BF16))


def _layer(x, mem, norm_in, norm_mem, w_in, q_norm_a, k_norm_a, lower_bounds, o_norm_b, w_mem_kv, q_norm_c, k_norm_c,
           w_branch, w_out):
    batch, seq, d_model = x.shape
    mem_len = mem.shape[1]
    x2 = x.reshape(batch * seq, d_model)
    (qa, ka, vat, sza, qi, ki2, wit, sqb, logf, kk, ib, ibt, sgb, qc, szc, gates) = _proj_call(
        x2, norm_in, w_in, lower_bounds, q_norm_a, k_norm_a, q_norm_c)
    kc, vc = _memkv_call(mem.reshape(batch * mem_len, d_model), norm_mem, w_mem_kv, k_norm_c, mem_len)
    oa = _dsa_call(qa, qi, wit, sza, ka, vat, ki2, batch, seq)
    ob = _hgrn_call(sqb, logf, kk, ib, ibt, sgb, o_norm_b, batch, seq)
    out = _merge_call(x2, oa, ob, qc, szc, gates, kc, vc, w_branch, w_out, batch, seq, mem_len)
    return out.reshape(batch, seq, d_model)


def kernel(x, mem, norm_in, norm_mem, w_in, q_norm_a, k_norm_a, lower_bounds, o_norm_b, w_mem_kv, q_norm_c, k_norm_c,
           w_branch, w_out):
    assert norm_in.shape[0] == 1, "single-layer block"
    return _layer(x, mem, norm_in[0], norm_mem[0], w_in[0], q_norm_a[0], k_norm_a[0], lower_bounds, o_norm_b[0],
                  w_mem_kv[0], q_norm_c[0], k_norm_c[0], w_branch[0], w_out[0])
```

```python
import functools

import jax
import jax.numpy as jnp
import numpy as np
from jax import lax
from jax.experimental import pallas as pl
from jax.experimental.pallas import tpu as pltpu

F32 = jnp.float32
BF16 = jnp.bfloat16
I32 = jnp.int32

N_HEADS_A = 8
HEAD_DIM_A = 64
N_IDX_HEADS = 8
IDX_DIM = 64
TOPK_MAX = 256
N_HEADS_B = 4
HEAD_B = 128
N_HEADS_C = 4
HEAD_DIM_C = 128
N_BRANCH = 3
EPS = 1e-6
WIDTH = 512
IDX_SCALE = (IDX_DIM ** -0.5) * (N_IDX_HEADS ** -0.5)

LANES = 128
VMEM_LIMIT = 56 * 1024 * 1024

TM_PROJ = 256
TQ = 128
TKS = 256
TKC = 512
SUM_ROWS = 16
CHUNK = 64
SUB = 16
T_HGRN = 128
NB_HGRN = 4
TM_MERGE = 256

NEG_BIG = -1e30
LOG2E = 1.4426950408889634
INT_MIN = -2147483648
KEY_LOWEST = -2139095040

NT = (((1,), (1,)), ((), ()))


def _dot(a, b):
    return jnp.dot(a, b, preferred_element_type=F32)


def _dot_nt(a, b):
    return lax.dot_general(a, b, NT, preferred_element_type=F32)


def _group_meansq(v, ones_ref, group):
    sq = v * v
    hi = sq.astype(BF16)
    lo = (sq - hi.astype(F32)).astype(BF16)
    s = _dot(hi, ones_ref[...]) + _dot(lo, ones_ref[...])
    return s * (1.0 / group)


_C_QA, _C_KA, _C_ZA, _C_QB, _C_FB, _C_IB, _C_GB, _C_QC, _C_ZC = [WIDTH * i for i in range(9)]
_C_GATES = 9 * WIDTH
_C_QI = _C_GATES + 3 * 1024
_C_KI = _C_QI + WIDTH
_C_END = _C_KI + LANES


def _proj_body(x_ref, nin_ref, w_ref, wvt_ref, wwt_ref, lbp_ref, gqa_ref, gka_ref, gqc_ref, ones64_ref, ones128_ref,
               qa_ref, ka_ref, vat_ref, sza_ref, qi_ref, ki2_ref, wit_ref,
               sqb_ref, logf_ref, kk_ref, ib_ref, ibt_ref, sgb_ref, qc_ref, szc_ref, gates_ref, *, d_model):
    x = x_ref[...]
    ms = jnp.mean(x * x, axis=-1, keepdims=True)
    h = (x * lax.rsqrt(ms + EPS)) * nin_ref[...]
    hb = h.astype(BF16)

    def proj(c0, width):
        return lambda: _dot(hb, w_ref[:, c0:c0 + width])

    def head_norm(ones_ref, group, gain_ref, scale, out_ref):
        def epilogue(v):
            v = v * lax.rsqrt(_group_meansq(v, ones_ref, group) + EPS) * gain_ref[...]
            out_ref[...] = (v * scale).astype(BF16)
        return epilogue

    def store(out_ref, fn=lambda v: v):
        def epilogue(v):
            out_ref[...] = fn(v).astype(out_ref.dtype)
        return epilogue

    def store_time_blocks(out_ref):
        def epilogue(v):
            v = v.astype(BF16)
            for t in range(out_ref.shape[0]):
                out_ref[t] = v[:, t * LANES:(t + 1) * LANES]
        return epilogue

    def forget(v):
        lbp = lbp_ref[...]
        lbe = jnp.exp(lbp - jnp.max(lbp, axis=0, keepdims=True))
        lb = lbe[0:1, :] / jnp.sum(lbe, axis=0, keepdims=True)
        f = lb + (1.0 - lb) * jax.nn.sigmoid(v)
        logf_ref[...] = jnp.log(f)
        kk_ref[...] = (1.0 - f).astype(BF16)

    def gate(n):
        def epilogue(v):
            gates_ref[:, n * d_model:(n + 1) * d_model] = jax.nn.sigmoid(v).astype(BF16)
        return epilogue

    stages = [
        (proj(_C_QA, WIDTH), head_norm(ones64_ref, HEAD_DIM_A, gqa_ref, HEAD_DIM_A ** -0.5 * LOG2E, qa_ref)),
        (proj(_C_KA, WIDTH), head_norm(ones64_ref, HEAD_DIM_A, gka_ref, 1.0, ka_ref)),
        (lambda: _dot_nt(wvt_ref[0:WIDTH, :], hb), store_time_blocks(vat_ref)),
        (proj(_C_ZA, WIDTH), store(sza_ref, jax.nn.silu)),
        (proj(_C_QI, WIDTH), store(qi_ref)),
        (proj(_C_KI, LANES), store(ki2_ref)),
        (lambda: _dot_nt(wwt_ref[...], hb), store(wit_ref)),
        (proj(_C_QB, WIDTH), store(sqb_ref, jax.nn.silu)),
        (proj(_C_FB, WIDTH), forget),
        (proj(_C_IB, WIDTH), store(ib_ref)),
        (lambda: _dot_nt(wvt_ref[WIDTH:2 * WIDTH, :], hb), store_time_blocks(ibt_ref)),
        (proj(_C_GB, WIDTH), store(sgb_ref, jax.nn.silu)),
        (proj(_C_QC, WIDTH), head_norm(ones128_ref, HEAD_DIM_C, gqc_ref, HEAD_DIM_C ** -0.5, qc_ref)),
        (proj(_C_ZC, WIDTH), store(szc_ref, jax.nn.silu)),
    ] + [(proj(_C_GATES + n * d_model, d_model), gate(n)) for n in range(N_BRANCH)]
    pending = stages[0][0]()
    for k, (_, epilogue) in enumerate(stages):
        ahead = stages[k + 1][0]() if k + 1 < len(stages) else None
        epilogue(pending)
        pending = ahead


def _const_spec(shape):
    nd = len(shape)
    return pl.BlockSpec(shape, lambda *_: (0,) * nd, pipeline_mode=pl.Buffered(1))


def _block_ones(width, group):
    g = np.arange(width) // group
    return jnp.asarray((g[:, None] == g[None, :]).astype(np.float32), dtype=BF16)


def _proj_call(x2, norm_in, w_in, lower_bounds, q_norm_a, k_norm_a, q_norm_c):
    rows, d_model = x2.shape
    tm = TM_PROJ
    assert rows % tm == 0 and d_model == 1024
    offs = np.cumsum([0, 512, 512, 512, 512, 512, 64, 8, 512, 512, 512, 512, 512, 512, 3 * d_model])
    (o_qa, o_ka, o_va, o_za, o_qi, o_ki, o_wi, o_qb, o_fb, o_ib, o_gb, o_qc, o_zc, o_g, o_end) = [int(o) for o in offs]
    assert o_end == w_in.shape[1]
    wb = w_in.astype(BF16)
    col = lambda a, b: wb[:, a:b]
    w_main = jnp.concatenate([
        col(o_qa, o_ka), col(o_ka, o_va), col(o_za, o_qi), col(o_qb, o_fb), col(o_fb, o_ib), col(o_ib, o_gb),
        col(o_gb, o_qc), col(o_qc, o_zc), col(o_zc, o_g), col(o_g, o_end), col(o_qi, o_ki),
        col(o_ki, o_wi), col(o_ki, o_wi)], axis=1)
    assert w_main.shape[1] == _C_END
    wvt = jnp.concatenate([col(o_va, o_za).T, col(o_ib, o_gb).T], axis=0)
    wwt = jnp.concatenate([col(o_wi, o_qb).T, jnp.zeros((8, d_model), BF16)], axis=0)
    tile = lambda g, reps: jnp.tile(g.astype(F32), reps)[None, :]
    n_slots = lower_bounds.shape[0]

    row_spec = lambda w: pl.BlockSpec((tm, w), lambda i: (i, 0))
    t_spec = pl.BlockSpec((tm // LANES, WIDTH, LANES), lambda i: (i, 0, 0))
    out_shape = [
        jax.ShapeDtypeStruct((rows, WIDTH), BF16),
        jax.ShapeDtypeStruct((rows, WIDTH), BF16),
        jax.ShapeDtypeStruct((rows // LANES, WIDTH, LANES), BF16),
        jax.ShapeDtypeStruct((rows, WIDTH), BF16),
        jax.ShapeDtypeStruct((rows, WIDTH), BF16),
        jax.ShapeDtypeStruct((rows, LANES), BF16),
        jax.ShapeDtypeStruct((16, rows), F32),
        jax.ShapeDtypeStruct((rows, WIDTH), BF16),
        jax.ShapeDtypeStruct((rows, WIDTH), F32),
        jax.ShapeDtypeStruct((rows, WIDTH), BF16),
        jax.ShapeDtypeStruct((rows, WIDTH), BF16),
        jax.ShapeDtypeStruct((rows // LANES, WIDTH, LANES), BF16),
        jax.ShapeDtypeStruct((rows, WIDTH), BF16),
        jax.ShapeDtypeStruct((rows, WIDTH), BF16),
        jax.ShapeDtypeStruct((rows, WIDTH), BF16),
        jax.ShapeDtypeStruct((rows, N_BRANCH * d_model), BF16),
    ]
    out_specs = [row_spec(WIDTH), row_spec(WIDTH), t_spec, row_spec(WIDTH), row_spec(WIDTH), row_spec(LANES),
                 pl.BlockSpec((16, tm), lambda i: (0, i)),
                 row_spec(WIDTH), row_spec(WIDTH), row_spec(WIDTH), row_spec(WIDTH), t_spec, row_spec(WIDTH),
                 row_spec(WIDTH), row_spec(WIDTH), row_spec(N_BRANCH * d_model)]
    in_specs = [row_spec(d_model), _const_spec((1, d_model)), _const_spec(w_main.shape), _const_spec(wvt.shape),
                _const_spec(wwt.shape), _const_spec((n_slots, WIDTH)), _const_spec((1, WIDTH)),
                _const_spec((1, WIDTH)), _const_spec((1, WIDTH)), _const_spec((WIDTH, WIDTH)),
                _const_spec((WIDTH, WIDTH))]
    return pl.pallas_call(
        functools.partial(_proj_body, d_model=d_model),
        out_shape=out_shape, grid=(rows // tm,), in_specs=in_specs, out_specs=out_specs,
        compiler_params=pltpu.CompilerParams(dimension_semantics=("parallel",), vmem_limit_bytes=VMEM_LIMIT),
        name="proj",
    )(x2, norm_in.astype(F32)[None, :], w_main, wvt, wwt, lower_bounds.astype(F32),
      tile(q_norm_a, N_HEADS_A), tile(k_norm_a, N_HEADS_A), tile(q_norm_c, N_HEADS_C),
      _block_ones(WIDTH, HEAD_DIM_A), _block_ones(WIDTH, HEAD_DIM_C))


def _memkv_body(m_ref, nm_ref, w_ref, gk_ref, ones128_ref, kc_ref, vc_ref):
    x = m_ref[...]
    ms = jnp.mean(x * x, axis=-1, keepdims=True)
    hb = ((x * lax.rsqrt(ms + EPS)) * nm_ref[...]).astype(BF16)
    kc = _dot(hb, w_ref[:, 0:WIDTH])
    kc = kc * lax.rsqrt(_group_meansq(kc, ones128_ref, HEAD_DIM_C) + EPS) * gk_ref[...]
    kc_ref[...] = kc.astype(BF16)
    vc_ref[...] = _dot(hb, w_ref[:, WIDTH:2 * WIDTH]).astype(BF16)


def _memkv_call(mem2, norm_mem, w_mem_kv, k_norm_c, tm):
    rows, d_model = mem2.shape
    row_spec = lambda w: pl.BlockSpec((tm, w), lambda i: (i, 0))
    return pl.pallas_call(
        _memkv_body,
        out_shape=[jax.ShapeDtypeStruct((rows, WIDTH), BF16)] * 2,
        grid=(rows // tm,),
        in_specs=[row_spec(d_model), _const_spec((1, d_model)), _const_spec((d_model, 2 * WIDTH)),
                  _const_spec((1, WIDTH)), _const_spec((WIDTH, WIDTH))],
        out_specs=[row_spec(WIDTH), row_spec(WIDTH)],
        compiler_params=pltpu.CompilerParams(dimension_semantics=("parallel",), vmem_limit_bytes=VMEM_LIMIT),
        name="memkv",
    )(mem2, norm_mem.astype(F32)[None, :], w_mem_kv.astype(BF16),
      jnp.tile(k_norm_c.astype(F32), N_HEADS_C)[None, :], _block_ones(WIDTH, HEAD_DIM_C))


def _key_to_f32(k):
    return pltpu.bitcast(jnp.where(k < 0, k ^ jnp.int32(0x7FFFFFFF), k), F32)


def _slope_log2(h):
    return (2.0 ** (-8.0 * (h + 1) / N_HEADS_A)) * LOG2E


N_SLOPE_PARTS = 3


def _slope_rows(h, row):
    rest = np.float32(_slope_log2(h))
    out = jnp.zeros(row.shape, F32)
    for r in range(N_SLOPE_PARTS):
        part = np.float32(np.asarray(rest, dtype=jnp.bfloat16))
        out = jnp.where(row == r, float(part), out)
        rest = np.float32(rest - part)
    return out


def _dsa_body(qa_ref, qi_ref, wit_ref, sza_ref, ka_ref, vat_ref, ki2_ref, out_ref,
              sc_ref, qir_ref, qar_ref, m_ref, alpha_ref, acc_ref, pbuf_ref, sbuf_ref, rbuf_ref, ot_ref,
              *, topk, pos_bits):
    i = pl.program_id(1)
    q0 = i * TQ
    n_att = (q0 + TQ + TKS - 1) // TKS
    n_cnt = (q0 + TQ + TKC - 1) // TKC

    row = lax.broadcasted_iota(I32, (LANES, TQ), 0)
    row_lo = row < HEAD_DIM_A
    pairs = range(N_HEADS_A // 2)

    def pair_operand(ref, p):
        t = ref[:, p * LANES:(p + 1) * LANES].astype(F32).T
        return jnp.concatenate([jnp.where(row_lo, t, 0.0), jnp.where(row_lo, 0.0, t)], axis=1).astype(BF16)

    for p in pairs:
        qir_ref[p] = pair_operand(qi_ref, p)

    def index_dots(jt):
        kt = ki2_ref[pl.ds(pl.multiple_of(jt * TKC, TKC), TKC), :]
        return [_dot(kt, qir_ref[p]) for p in pairs]

    first = index_dots(0)
    for p in pairs:
        qar_ref[p, 0:LANES, :] = pair_operand(qa_ref, p)
        qar_ref[p, LANES:2 * LANES, :] = jnp.concatenate(
            [_slope_rows(2 * p, row), _slope_rows(2 * p + 1, row)], axis=1).astype(BF16)
    m_ref[...] = jnp.full(m_ref.shape, NEG_BIG, F32)
    acc_ref[...] = jnp.zeros(acc_ref.shape, F32)
    alpha_ref[...] = jnp.ones(alpha_ref.shape, F32)
    pbuf_ref[...] = jnp.zeros(pbuf_ref.shape, BF16)
    for p in pairs:
        rbuf_ref[p] = first[p]

    row_s = lax.broadcasted_iota(I32, (TKS, TQ), 0)
    tpos = q0 + lax.broadcasted_iota(I32, (TKS, TQ), 1)

    def score_tile(j, carry):
        ahead = index_dots(jnp.minimum(j + 1, n_cnt - 1))
        for r in range(TKC // TKS):
            rows = slice(r * TKS, (r + 1) * TKS)
            r0 = pl.multiple_of(j * TKC + r * TKS, TKS)
            acc = jnp.zeros((TKS, TQ), F32)
            for p in pairs:
                rel = jnp.maximum(rbuf_ref[p, rows, :], 0.0)
                acc = acc + rel[:, :TQ] * wit_ref[2 * p:2 * p + 1, :] + rel[:, TQ:] * wit_ref[2 * p + 1:2 * p + 2, :]
            sc = acc * IDX_SCALE
            sc = jnp.where(sc == 0.0, 0.0, sc)
            sc_ref[pl.ds(r0, TKS), :] = jnp.where(r0 + row_s <= tpos, sc, -jnp.inf)
        for p in pairs:
            rbuf_ref[p] = ahead[p]
        return carry

    lax.fori_loop(0, n_cnt, score_tile, 0)

    row_c = lax.broadcasted_iota(I32, (TKC, TQ), 0)

    def count(pred):
        def tile(j, acc):
            r0 = pl.multiple_of(j * TKC, TKC)
            c = jnp.where(pred(sc_ref[pl.ds(r0, TKC), :], r0), jnp.int32(1), jnp.int32(0))
            return acc + jnp.sum(c.reshape(TKC // 8, 8, TQ), axis=0)
        acc = lax.fori_loop(0, n_cnt, tile, jnp.zeros((8, TQ), I32))
        return jnp.sum(acc, axis=0, keepdims=True)

    def bisect(it, p):
        cand = p ^ lax.shift_left(jnp.int32(1), 31 - it)
        cand_f = _key_to_f32(cand)
        n = count(lambda s, r0: s >= cand_f)
        return jnp.where(n >= topk, cand, p)

    kth = lax.fori_loop(0, 32, bisect, jnp.full((1, TQ), INT_MIN, I32))
    kth = jnp.maximum(kth, jnp.int32(KEY_LOWEST))
    thr = _key_to_f32(kth)
    nxt = _key_to_f32(kth + 1)
    n_gt = count(lambda s, r0: s >= nxt)
    n_ge = count(lambda s, r0: s >= thr)
    need = topk - n_gt
    has_tie = n_ge > topk
    any_tie = jnp.max(jnp.where(has_tie, 1, 0)) > 0

    lane_a = lax.broadcasted_iota(I32, (TKS, LANES), 1)
    aug = jnp.where(lane_a < N_SLOPE_PARTS, lax.broadcasted_iota(I32, (TKS, LANES), 0).astype(F32), 0.0).astype(BF16)
    half2 = lax.broadcasted_iota(I32, (1, 2 * TQ), 1) < TQ

    ones_rows = jnp.ones((SUM_ROWS, TKS), BF16)

    def value_update(jt):
        for p in pairs:
            vt = jnp.concatenate([vat_ref[jt * (TKS // LANES) + t, p * LANES:(p + 1) * LANES, :]
                                  for t in range(TKS // LANES)], axis=1)
            vt = jnp.concatenate([vt, ones_rows], axis=0)
            acc_ref[p] = alpha_ref[p] * acc_ref[p] + _dot(vt, pbuf_ref[p])

    def score_dots(jt):
        r0 = pl.multiple_of(jt * TKS, TKS)
        return [_dot(jnp.concatenate([ka_ref[pl.ds(r0, TKS), p * LANES:(p + 1) * LANES], aug], axis=1), qar_ref[p])
                for p in pairs]

    for p, s in zip(pairs, score_dots(0)):
        sbuf_ref[p] = s

    def tie_cut():
        def step(it, jp):
            cand = jp | lax.shift_left(jnp.int32(1), pos_bits - 1 - it)
            n = count(lambda s, r0: (s >= thr) & (s < nxt) & (r0 + row_c < cand))
            return jnp.where(n < need, cand, jp)
        return lax.fori_loop(0, pos_bits, step, jnp.zeros((1, TQ), I32))

    cut = lax.cond(any_tie, tie_cut, lambda: jnp.zeros((1, TQ), I32))
    cut = jnp.where(has_tie, cut, jnp.int32(2 ** 30))

    def att_tile(j, carry):
        r0 = pl.multiple_of(j * TKS, TKS)
        s_idx = sc_ref[pl.ds(r0, TKS), :]
        sel = (s_idx >= thr) & ((s_idx >= nxt) | (r0 + row_s <= cut))
        bias = jnp.where(sel, 0.0, NEG_BIG)
        bias2 = jnp.concatenate([bias, bias], axis=1)
        base = (r0 - q0).astype(F32)
        value_update(jnp.maximum(j - 1, 0))
        ahead = score_dots(jnp.minimum(j + 1, n_att - 1))
        for p in pairs:
            shift = jnp.where(half2, _slope_log2(2 * p), _slope_log2(2 * p + 1)) * base
            s = sbuf_ref[p] + bias2
            m_old = m_ref[p]
            m_new = jnp.maximum(m_old, jnp.max(s, axis=0, keepdims=True) + shift)
            alpha = jnp.exp2(m_old - m_new)
            m_ref[p] = m_new
            alpha_ref[p] = alpha
            pbuf_ref[p] = jnp.exp2(s - (m_new - shift)).astype(BF16)
        for p in pairs:
            sbuf_ref[p] = ahead[p]
        return carry

    lax.fori_loop(0, n_att, att_tile, 0)
    value_update(n_att - 1)

    row_lo = lax.broadcasted_iota(I32, (LANES, TQ), 0) < HEAD_DIM_A
    for p in range(N_HEADS_A // 2):
        o = acc_ref[p, 0:LANES, :] * (1.0 / acc_ref[p, LANES:LANES + 1, :])
        ot_ref[p * LANES:(p + 1) * LANES, :] = jnp.where(row_lo, o[:, :TQ], o[:, TQ:])
    out_ref[...] = (ot_ref[...].T * sza_ref[...].astype(F32)).astype(BF16)


def _dsa_call(qa, qi, wit, sza, ka, vat, ki2, batch, seq):
    assert TQ == LANES and TKS % LANES == 0 and TKC % TKS == 0 and seq % TKC == 0
    nq = seq // TQ
    topk = min(TOPK_MAX, seq // 4)
    pos_bits = max(1, int(np.ceil(np.log2(seq))))
    q_spec = lambda w: pl.BlockSpec((TQ, w), lambda b, i: (b * nq + i, 0))
    b_spec = lambda w: pl.BlockSpec((seq, w), lambda b, i: (b, 0))
    n_pair = N_HEADS_A // 2
    return pl.pallas_call(
        functools.partial(_dsa_body, topk=topk, pos_bits=pos_bits),
        out_shape=jax.ShapeDtypeStruct((batch * seq, WIDTH), BF16),
        grid=(batch, nq),
        in_specs=[q_spec(WIDTH), q_spec(WIDTH), pl.BlockSpec((16, TQ), lambda b, i: (0, b * nq + i)), q_spec(WIDTH),
                  b_spec(WIDTH), pl.BlockSpec((seq // LANES, WIDTH, LANES), lambda b, i: (b, 0, 0)), b_spec(LANES)],
        out_specs=q_spec(WIDTH),
        scratch_shapes=[pltpu.VMEM((seq, TQ), F32),
                        pltpu.VMEM((n_pair, LANES, 2 * TQ), BF16), pltpu.VMEM((n_pair, 2 * LANES, 2 * TQ), BF16),
                        pltpu.VMEM((n_pair, 1, 2 * TQ), F32), pltpu.VMEM((n_pair, 1, 2 * TQ), F32),
                        pltpu.VMEM((n_pair, LANES + SUM_ROWS, 2 * TQ), F32),
                        pltpu.VMEM((n_pair, TKS, 2 * TQ), BF16), pltpu.VMEM((n_pair, TKS, 2 * TQ), F32),
                        pltpu.VMEM((n_pair, TKC, 2 * TQ), F32), pltpu.VMEM((WIDTH, TQ), F32)],
        compiler_params=pltpu.CompilerParams(dimension_semantics=("parallel", "arbitrary"),
                                             vmem_limit_bytes=VMEM_LIMIT),
        name="dsa",
    )(qa, qi, wit, sza, ka, vat, ki2)


def _split3(v):
    hi = v.astype(BF16)
    r = v - hi.astype(F32)
    mid = r.astype(BF16)
    lo = (r - mid.astype(F32)).astype(BF16)
    return hi, mid, lo


def _hgrn_body(sqb_ref, logf_ref, kk_ref, ib_ref, ibt_ref, sgb_ref, gon_ref, out_ref, st_ref):
    @pl.when(pl.program_id(1) == 0)
    def _():
        st_ref[...] = jnp.zeros_like(st_ref)

    r_i = lax.broadcasted_iota(I32, (CHUNK, CHUNK), 0)
    c_i = lax.broadcasted_iota(I32, (CHUNK, CHUNK), 1)
    tril = c_i <= r_i
    tri = jnp.where(tril, 1.0, 0.0).astype(BF16)
    n_sub = CHUNK // SUB
    row_c = lax.broadcasted_iota(I32, (CHUNK, LANES), 0)
    row_t = lax.broadcasted_iota(I32, (T_HGRN, LANES), 0)

    def decays(nb, c):
        g = logf_ref[nb, c * CHUNK:(c + 1) * CHUNK, :]
        return sum(_dot(tri, part) for part in _split3(g))

    def operands(nb, c, b):
        rows = slice(c * CHUNK, (c + 1) * CHUNK)
        q = sqb_ref[nb, rows, :].astype(F32)
        kk = kk_ref[nb, rows, :].astype(F32)
        b_last = b[CHUNK - 1:CHUNK, :]
        ref_k = jnp.concatenate(
            [jnp.broadcast_to(b[(j + 1) * SUB - 1:(j + 1) * SUB, :], (SUB, WIDTH)) for j in range(n_sub)], axis=0)
        k_in = kk * jnp.exp(ref_k - b)
        q_out = (q * jnp.exp(b)).astype(BF16)
        k_st = (kk * jnp.exp(b_last - b)).astype(BF16)
        q_in = []
        for j in range(n_sub):
            lo = j * SUB
            qj = q[lo:, :] * jnp.exp(b[lo:, :] - b[lo + SUB - 1:lo + SUB, :])
            if lo:
                qj = jnp.concatenate([jnp.zeros((lo, WIDTH), F32), qj], axis=0)
            q_in.append(qj.astype(BF16))
        return q_in, k_in, q_out, k_st, jnp.exp(b_last)

    def intra_scores(hd, q_in, k_in):
        cols = slice(hd * HEAD_B, (hd + 1) * HEAD_B)
        qs = jnp.concatenate([qj[:, cols] for qj in q_in], axis=1)
        kh = k_in[:, cols]
        ks = jnp.concatenate(
            [jnp.where((row_c >= j * SUB) & (row_c < (j + 1) * SUB), kh, 0.0) for j in range(n_sub)],
            axis=1).astype(BF16)
        return _dot_nt(qs, ks)

    def outputs(nb, c, hd, a_raw, q_out, k_st, decay_last):
        rows = slice(c * CHUNK, (c + 1) * CHUNK)
        cols = slice(hd * HEAD_B, (hd + 1) * HEAD_B)
        a = jnp.where(tril, a_raw, 0.0).astype(BF16)
        st = st_ref[nb, hd]
        o = _dot(a, ib_ref[nb, rows, cols]) + _dot_nt(q_out[:, cols], st.astype(BF16))
        k_pad = jnp.where((row_t >= c * CHUNK) & (row_t < (c + 1) * CHUNK),
                          jnp.concatenate([k_st[:, cols]] * (T_HGRN // CHUNK), axis=0), jnp.zeros((), BF16))
        st_ref[nb, hd] = st * decay_last[:, cols] + _dot(ibt_ref[nb, 0, cols, :], k_pad)
        return o

    def finish(nb, c, hd, o):
        rows = slice(c * CHUNK, (c + 1) * CHUNK)
        cols = slice(hd * HEAD_B, (hd + 1) * HEAD_B)
        ms = jnp.mean(o * o, axis=-1, keepdims=True)
        o = o * lax.rsqrt(ms + EPS) * gon_ref[...]
        out_ref[nb, rows, cols] = (o * sgb_ref[nb, rows, cols].astype(F32)).astype(BF16)

    seqs, heads = range(NB_HGRN), range(N_HEADS_B)
    for c in range(T_HGRN // CHUNK):
        bs = [decays(nb, c) for nb in seqs]
        ops = [operands(nb, c, bs[nb]) for nb in seqs]
        raw = [[intra_scores(hd, ops[nb][0], ops[nb][1]) for hd in heads] for nb in seqs]
        outs = [[outputs(nb, c, hd, raw[nb][hd], ops[nb][2], ops[nb][3], ops[nb][4]) for hd in heads] for nb in seqs]
        for nb in seqs:
            for hd in heads:
                finish(nb, c, hd, outs[nb][hd])


def _hgrn_call(sqb, logf, kk, ib, ibt, sgb, o_norm_b, batch, seq):
    assert T_HGRN == LANES and seq % T_HGRN == 0 and batch % NB_HGRN == 0
    nt = seq // T_HGRN
    as3 = lambda a: a.reshape(batch, seq, WIDTH)
    spec = pl.BlockSpec((NB_HGRN, T_HGRN, WIDTH), lambda b, t: (b, t, 0))
    ibt_spec = pl.BlockSpec((NB_HGRN, 1, WIDTH, T_HGRN), lambda b, t: (b, t, 0, 0))
    out = pl.pallas_call(
        _hgrn_body,
        out_shape=jax.ShapeDtypeStruct((batch, seq, WIDTH), BF16),
        grid=(batch // NB_HGRN, nt),
        in_specs=[spec, spec, spec, spec, ibt_spec, spec, _const_spec((1, HEAD_B))],
        out_specs=spec,
        scratch_shapes=[pltpu.VMEM((NB_HGRN, N_HEADS_B, HEAD_B, HEAD_B), F32)],
        compiler_params=pltpu.CompilerParams(dimension_semantics=("parallel", "arbitrary"),
                                             vmem_limit_bytes=VMEM_LIMIT),
        name="hgrn",
    )(as3(sqb), as3(logf), as3(kk), as3(ib), ibt.reshape(batch, nt, WIDTH, T_HGRN), as3(sgb),
      o_norm_b.astype(F32)[None, :])
    return out.reshape(batch * seq, WIDTH)


def _merge_body(x_ref, oa_ref, ob_ref, qc_ref, szc_ref, gates_ref, kc_ref, vc_ref, wb_ref, wo_ref, out_ref, *, d_model):
    oc = []
    for hd in range(N_HEADS_C):
        cols = slice(hd * HEAD_DIM_C, (hd + 1) * HEAD_DIM_C)
        s = _dot_nt(qc_ref[:, cols], kc_ref[:, cols])
        m = jnp.max(s, axis=-1, keepdims=True)
        p = jnp.exp(s - m)
        l = jnp.sum(p, axis=-1, keepdims=True)
        oc.append(_dot(p.astype(BF16), vc_ref[:, cols]) * (1.0 / l))
    oc = (jnp.concatenate(oc, axis=1) * szc_ref[...].astype(F32)).astype(BF16)
    merged = jnp.zeros((x_ref.shape[0], d_model), F32)
    for n, br in enumerate((oa_ref[...], ob_ref[...], oc)):
        y = _dot(br, wb_ref[n])
        merged = merged + gates_ref[:, n * d_model:(n + 1) * d_model].astype(F32) * y
    out_ref[...] = x_ref[...] + _dot(merged.astype(BF16), wo_ref[...])


def _merge_call(x2, oa, ob, qc, szc, gates, kc, vc, w_branch, w_out, batch, seq, mem_len):
    rows, d_model = x2.shape
    tm = TM_MERGE
    nt = seq // tm
    row_spec = lambda w: pl.BlockSpec((tm, w), lambda b, t: (b * nt + t, 0))
    mem_spec = pl.BlockSpec((mem_len, WIDTH), lambda b, t: (b, 0))
    return pl.pallas_call(
        functools.partial(_merge_body, d_model=d_model),
        out_shape=jax.ShapeDtypeStruct((rows, d_model), F32),
        grid=(batch, nt),
        in_specs=[row_spec(d_model), row_spec(WIDTH), row_spec(WIDTH), row_spec(WIDTH), row_spec(WIDTH),
                  row_spec(N_BRANCH * d_model), mem_spec, mem_spec,
                  _const_spec((N_BRANCH, WIDTH, d_model)), _const_spec((d_model, d_model))],
        out_specs=row_spec(d_model),
        compiler_params=pltpu.CompilerParams(dimension_semantics=("parallel", "parallel"),
                                             vmem_limit_bytes=VMEM_LIMIT),
        name="merge",
    )(x2, oa, ob, qc, szc, gates, kc, vc, w_branch.astype(BF16), w_out.astype(BF16))


def _layer(x, mem, norm_in, norm_mem, w_in, q_norm_a, k_norm_a, lower_bounds, o_norm_b, w_mem_kv, q_norm_c, k_norm_c,
           w_branch, w_out):
    batch, seq, d_model = x.shape
    mem_len = mem.shape[1]
    x2 = x.reshape(batch * seq, d_model)
    (qa, ka, vat, sza, qi, ki2, wit, sqb, logf, kk, ib, ibt, sgb, qc, szc, gates) = _proj_call(
        x2, norm_in, w_in, lower_bounds, q_norm_a, k_norm_a, q_norm_c)
    kc, vc = _memkv_call(mem.reshape(batch * mem_len, d_model), norm_mem, w_mem_kv, k_norm_c, mem_len)
    oa = _dsa_call(qa, qi, wit, sza, ka, vat, ki2, batch, seq)
    ob = _hgrn_call(sqb, logf, kk, ib, ibt, sgb, o_norm_b, batch, seq)
    out = _merge_call(x2, oa, ob, qc, szc, gates, kc, vc, w_branch, w_out, batch, seq, mem_len)
    return out.reshape(batch, seq, d_model)


def kernel(x, mem, norm_in, norm_mem, w_in, q_norm_a, k_norm_a, lower_bounds, o_norm_b, w_mem_kv, q_norm_c, k_norm_c,
           w_branch, w_out):
    assert norm_in.shape[0] == 1, "single-layer block"
    return _layer(x, mem, norm_in[0], norm_mem[0], w_in[0], q_norm_a[0], k_norm_a[0], lower_bounds, o_norm_b[0],
                  w_mem_kv[0], q_norm_c[0], k_norm_c[0], w_branch[0], w_out[0])
```

```python
import functools

import jax
import jax.numpy as jnp
import numpy as np
from jax import lax
from jax.experimental import pallas as pl
from jax.experimental.pallas import tpu as pltpu

F32 = jnp.float32
BF16 = jnp.bfloat16
I32 = jnp.int32

N_HEADS_A = 8
HEAD_DIM_A = 64
N_IDX_HEADS = 8
IDX_DIM = 64
TOPK_MAX = 256
N_HEADS_B = 4
HEAD_B = 128
N_HEADS_C = 4
HEAD_DIM_C = 128
N_BRANCH = 3
EPS = 1e-6
WIDTH = 512
IDX_SCALE = (IDX_DIM ** -0.5) * (N_IDX_HEADS ** -0.5)

LANES = 128
VMEM_LIMIT = 56 * 1024 * 1024

TM_PROJ = 256
TQ = 128
TKS = 256
TKA = 256
AUG_RADIX = 256
TKC = 512
SUM_ROWS = 16
CHUNK = 64
SUB = 16
T_HGRN = 128
NB_HGRN = 4
TM_MERGE = 256

NEG_BIG = -1e30
LOG2E = 1.4426950408889634
INT_MIN = -2147483648
KEY_LOWEST = -2139095040
BF16_ROWS = 16
STAGE2_BITS = 17

NT = (((1,), (1,)), ((), ()))


def _dot(a, b):
    return jnp.dot(a, b, preferred_element_type=F32)


def _dot_nt(a, b):
    return lax.dot_general(a, b, NT, preferred_element_type=F32)


def _group_meansq(v, ones_ref, group):
    sq = v * v
    hi = sq.astype(BF16)
    lo = (sq - hi.astype(F32)).astype(BF16)
    s = _dot(hi, ones_ref[...]) + _dot(lo, ones_ref[...])
    return s * (1.0 / group)


_C_QA, _C_KA, _C_ZA, _C_QB, _C_FB, _C_IB, _C_GB, _C_QC, _C_ZC = [WIDTH * i for i in range(9)]
_C_GATES = 9 * WIDTH
_C_QI = _C_GATES + 3 * 1024
_C_KI = _C_QI + WIDTH
_C_END = _C_KI + LANES


def _proj_body(x_ref, nin_ref, w_ref, wvt_ref, wwt_ref, lbp_ref, gqa_ref, gka_ref, gqc_ref, ones64_ref, ones128_ref,
               qa_ref, ka_ref, vat_ref, sza_ref, qi_ref, ki2_ref, wit_ref,
               sqb_ref, logf_ref, kk_ref, ib_ref, ibt_ref, sgb_ref, qc_ref, szc_ref, gates_ref, *, d_model):
    x = x_ref[...]
    ms = jnp.mean(x * x, axis=-1, keepdims=True)
    h = (x * lax.rsqrt(ms + EPS)) * nin_ref[...]
    hb = h.astype(BF16)

    def proj(c0, width):
        return lambda: _dot(hb, w_ref[:, c0:c0 + width])

    def head_norm(ones_ref, group, gain_ref, scale, out_ref):
        def epilogue(v):
            v = v * lax.rsqrt(_group_meansq(v, ones_ref, group) + EPS) * gain_ref[...]
            out_ref[...] = (v * scale).astype(BF16)
        return epilogue

    def store(out_ref, fn=lambda v: v):
        def epilogue(v):
            out_ref[...] = fn(v).astype(out_ref.dtype)
        return epilogue

    def store_time_blocks(out_ref):
        def epilogue(v):
            v = v.astype(BF16)
            for t in range(out_ref.shape[0]):
                out_ref[t] = v[:, t * LANES:(t + 1) * LANES]
        return epilogue

    def forget(v):
        lbp = lbp_ref[...]
        lbe = jnp.exp(lbp - jnp.max(lbp, axis=0, keepdims=True))
        lb = lbe[0:1, :] / jnp.sum(lbe, axis=0, keepdims=True)
        f = lb + (1.0 - lb) * jax.nn.sigmoid(v)
        logf_ref[...] = jnp.log(f)
        kk_ref[...] = (1.0 - f).astype(BF16)

    def gate(n):
        def epilogue(v):
            gates_ref[:, n * d_model:(n + 1) * d_model] = jax.nn.sigmoid(v).astype(BF16)
        return epilogue

    stages = [
        (proj(_C_QA, WIDTH), head_norm(ones64_ref, HEAD_DIM_A, gqa_ref, HEAD_DIM_A ** -0.5 * LOG2E, qa_ref)),
        (proj(_C_KA, WIDTH), head_norm(ones64_ref, HEAD_DIM_A, gka_ref, 1.0, ka_ref)),
        (lambda: _dot_nt(wvt_ref[0:WIDTH, :], hb), store_time_blocks(vat_ref)),
        (proj(_C_ZA, WIDTH), store(sza_ref, jax.nn.silu)),
        (proj(_C_QI, WIDTH), store(qi_ref)),
        (proj(_C_KI, LANES), store(ki2_ref)),
        (lambda: _dot_nt(wwt_ref[...], hb), store(wit_ref)),
        (proj(_C_QB, WIDTH), store(sqb_ref, jax.nn.silu)),
        (proj(_C_FB, WIDTH), forget),
        (proj(_C_IB, WIDTH), store(ib_ref)),
        (lambda: _dot_nt(wvt_ref[WIDTH:2 * WIDTH, :], hb), store_time_blocks(ibt_ref)),
        (proj(_C_GB, WIDTH), store(sgb_ref, jax.nn.silu)),
        (proj(_C_QC, WIDTH), head_norm(ones128_ref, HEAD_DIM_C, gqc_ref, HEAD_DIM_C ** -0.5, qc_ref)),
        (proj(_C_ZC, WIDTH), store(szc_ref, jax.nn.silu)),
    ] + [(proj(_C_GATES + n * d_model, d_model), gate(n)) for n in range(N_BRANCH)]
    pending = stages[0][0]()
    for k, (_, epilogue) in enumerate(stages):
        ahead = stages[k + 1][0]() if k + 1 < len(stages) else None
        epilogue(pending)
        pending = ahead


def _const_spec(shape):
    nd = len(shape)
    return pl.BlockSpec(shape, lambda *_: (0,) * nd, pipeline_mode=pl.Buffered(1))


def _block_ones(width, group):
    g = np.arange(width) // group
    return jnp.asarray((g[:, None] == g[None, :]).astype(np.float32), dtype=BF16)


def _proj_call(x2, norm_in, w_in, lower_bounds, q_norm_a, k_norm_a, q_norm_c):
    rows, d_model = x2.shape
    tm = TM_PROJ
    assert rows % tm == 0 and d_model == 1024
    offs = np.cumsum([0, 512, 512, 512, 512, 512, 64, 8, 512, 512, 512, 512, 512, 512, 3 * d_model])
    (o_qa, o_ka, o_va, o_za, o_qi, o_ki, o_wi, o_qb, o_fb, o_ib, o_gb, o_qc, o_zc, o_g, o_end) = [int(o) for o in offs]
    assert o_end == w_in.shape[1]
    wb = w_in.astype(BF16)
    col = lambda a, b: wb[:, a:b]
    w_main = jnp.concatenate([
        col(o_qa, o_ka), col(o_ka, o_va), col(o_za, o_qi), col(o_qb, o_fb), col(o_fb, o_ib), col(o_ib, o_gb),
        col(o_gb, o_qc), col(o_qc, o_zc), col(o_zc, o_g), col(o_g, o_end), col(o_qi, o_ki),
        col(o_ki, o_wi), col(o_ki, o_wi)], axis=1)
    assert w_main.shape[1] == _C_END
    wvt = jnp.concatenate([col(o_va, o_za).T, col(o_ib, o_gb).T], axis=0)
    wwt = jnp.concatenate([col(o_wi, o_qb).T, jnp.zeros((8, d_model), BF16)], axis=0)
    tile = lambda g, reps: jnp.tile(g.astype(F32), reps)[None, :]
    n_slots = lower_bounds.shape[0]

    row_spec = lambda w: pl.BlockSpec((tm, w), lambda i: (i, 0))
    t_spec = pl.BlockSpec((tm // LANES, WIDTH, LANES), lambda i: (i, 0, 0))
    out_shape = [
        jax.ShapeDtypeStruct((rows, WIDTH), BF16),
        jax.ShapeDtypeStruct((rows, WIDTH), BF16),
        jax.ShapeDtypeStruct((rows // LANES, WIDTH, LANES), BF16),
        jax.ShapeDtypeStruct((rows, WIDTH), BF16),
        jax.ShapeDtypeStruct((rows, WIDTH), BF16),
        jax.ShapeDtypeStruct((rows, LANES), BF16),
        jax.ShapeDtypeStruct((16, rows), F32),
        jax.ShapeDtypeStruct((rows, WIDTH), BF16),
        jax.ShapeDtypeStruct((rows, WIDTH), F32),
        jax.ShapeDtypeStruct((rows, WIDTH), BF16),
        jax.ShapeDtypeStruct((rows, WIDTH), BF16),
        jax.ShapeDtypeStruct((rows // LANES, WIDTH, LANES), BF16),
        jax.ShapeDtypeStruct((rows, WIDTH), BF16),
        jax.ShapeDtypeStruct((rows, WIDTH), BF16),
        jax.ShapeDtypeStruct((rows, WIDTH), BF16),
        jax.ShapeDtypeStruct((rows, N_BRANCH * d_model), BF16),
    ]
    out_specs = [row_spec(WIDTH), row_spec(WIDTH), t_spec, row_spec(WIDTH), row_spec(WIDTH), row_spec(LANES),
                 pl.BlockSpec((16, tm), lambda i: (0, i)),
                 row_spec(WIDTH), row_spec(WIDTH), row_spec(WIDTH), row_spec(WIDTH), t_spec, row_spec(WIDTH),
                 row_spec(WIDTH), row_spec(WIDTH), row_spec(N_BRANCH * d_model)]
    in_specs = [row_spec(d_model), _const_spec((1, d_model)), _const_spec(w_main.shape), _const_spec(wvt.shape),
                _const_spec(wwt.shape), _const_spec((n_slots, WIDTH)), _const_spec((1, WIDTH)),
                _const_spec((1, WIDTH)), _const_spec((1, WIDTH)), _const_spec((WIDTH, WIDTH)),
                _const_spec((WIDTH, WIDTH))]
    return pl.pallas_call(
        functools.partial(_proj_body, d_model=d_model),
        out_shape=out_shape, grid=(rows // tm,), in_specs=in_specs, out_specs=out_specs,
        compiler_params=pltpu.CompilerParams(dimension_semantics=("parallel",), vmem_limit_bytes=VMEM_LIMIT),
        name="proj",
    )(x2, norm_in.astype(F32)[None, :], w_main, wvt, wwt, lower_bounds.astype(F32),
      tile(q_norm_a, N_HEADS_A), tile(k_norm_a, N_HEADS_A), tile(q_norm_c, N_HEADS_C),
      _block_ones(WIDTH, HEAD_DIM_A), _block_ones(WIDTH, HEAD_DIM_C))


def _memkv_body(m_ref, nm_ref, w_ref, gk_ref, ones128_ref, kc_ref, vc_ref):
    x = m_ref[...]
    ms = jnp.mean(x * x, axis=-1, keepdims=True)
    hb = ((x * lax.rsqrt(ms + EPS)) * nm_ref[...]).astype(BF16)
    kc = _dot(hb, w_ref[:, 0:WIDTH])
    kc = kc * lax.rsqrt(_group_meansq(kc, ones128_ref, HEAD_DIM_C) + EPS) * gk_ref[...]
    kc_ref[...] = kc.astype(BF16)
    vc_ref[...] = _dot(hb, w_ref[:, WIDTH:2 * WIDTH]).astype(BF16)


def _memkv_call(mem2, norm_mem, w_mem_kv, k_norm_c, tm):
    rows, d_model = mem2.shape
    row_spec = lambda w: pl.BlockSpec((tm, w), lambda i: (i, 0))
    return pl.pallas_call(
        _memkv_body,
        out_shape=[jax.ShapeDtypeStruct((rows, WIDTH), BF16)] * 2,
        grid=(rows // tm,),
        in_specs=[row_spec(d_model), _const_spec((1, d_model)), _const_spec((d_model, 2 * WIDTH)),
                  _const_spec((1, WIDTH)), _const_spec((WIDTH, WIDTH))],
        out_specs=[row_spec(WIDTH), row_spec(WIDTH)],
        compiler_params=pltpu.CompilerParams(dimension_semantics=("parallel",), vmem_limit_bytes=VMEM_LIMIT),
        name="memkv",
    )(mem2, norm_mem.astype(F32)[None, :], w_mem_kv.astype(BF16),
      jnp.tile(k_norm_c.astype(F32), N_HEADS_C)[None, :], _block_ones(WIDTH, HEAD_DIM_C))


def _key_to_f32(k):
    return pltpu.bitcast(jnp.where(k < 0, k ^ jnp.int32(0x7FFFFFFF), k), F32)


def _slope_log2(h):
    return (2.0 ** (-8.0 * (h + 1) / N_HEADS_A)) * LOG2E


N_SLOPE_PARTS = 3


def _slope_rows(h, row):
    rest = np.float32(_slope_log2(h))
    out = jnp.zeros(row.shape, F32)
    for r in range(N_SLOPE_PARTS):
        part = np.float32(np.asarray(rest, dtype=jnp.bfloat16))
        out = jnp.where(row == r, float(part), out)
        out = jnp.where(row == N_SLOPE_PARTS + r, float(part) * AUG_RADIX, out)
        rest = np.float32(rest - part)
    return out


def _dsa_body(qa_ref, qi_ref, wit_ref, sza_ref, ka_ref, vat_ref, ki2_ref, out_ref,
              sc_ref, hi_ref, qir_ref, qar_ref, m_ref, alpha_ref, acc_ref, pbuf_ref, sbuf_ref, rbuf_ref, ot_ref,
              *, topk, pos_bits):
    i = pl.program_id(1)
    q0 = i * TQ
    n_att = (q0 + TQ + TKA - 1) // TKA
    n_cnt = (q0 + TQ + TKC - 1) // TKC

    row = lax.broadcasted_iota(I32, (LANES, TQ), 0)
    row_lo = row < HEAD_DIM_A
    pairs = range(N_HEADS_A // 2)

    def pair_operand(ref, p):
        t = ref[:, p * LANES:(p + 1) * LANES].astype(F32).T
        return jnp.concatenate([jnp.where(row_lo, t, 0.0), jnp.where(row_lo, 0.0, t)], axis=1).astype(BF16)

    for p in pairs:
        qir_ref[p] = pair_operand(qi_ref, p)

    def index_dots(jt):
        kt = ki2_ref[pl.ds(pl.multiple_of(jt * TKC, TKC), TKC), :]
        return [_dot(kt, qir_ref[p]) for p in pairs]

    first = index_dots(0)
    for p in pairs:
        qar_ref[p, 0:LANES, :] = pair_operand(qa_ref, p)
        qar_ref[p, LANES:2 * LANES, :] = jnp.concatenate(
            [_slope_rows(2 * p, row), _slope_rows(2 * p + 1, row)], axis=1).astype(BF16)
    m_ref[...] = jnp.full(m_ref.shape, NEG_BIG, F32)
    acc_ref[...] = jnp.zeros(acc_ref.shape, F32)
    alpha_ref[...] = jnp.ones(alpha_ref.shape, F32)
    pbuf_ref[...] = jnp.zeros(pbuf_ref.shape, BF16)
    for p in pairs:
        rbuf_ref[p] = first[p]

    row_s = lax.broadcasted_iota(I32, (TKS, TQ), 0)
    tpos = q0 + lax.broadcasted_iota(I32, (TKS, TQ), 1)

    def score_tile(j, carry):
        ahead = index_dots(jnp.minimum(j + 1, n_cnt - 1))
        for r in range(TKC // TKS):
            rows = slice(r * TKS, (r + 1) * TKS)
            r0 = pl.multiple_of(j * TKC + r * TKS, TKS)
            acc = jnp.zeros((TKS, TQ), F32)
            for p in pairs:
                rel = jnp.maximum(rbuf_ref[p, rows, :], 0.0)
                acc = acc + rel[:, :TQ] * wit_ref[2 * p:2 * p + 1, :] + rel[:, TQ:] * wit_ref[2 * p + 1:2 * p + 2, :]
            sc = acc * IDX_SCALE
            sc = jnp.where(sc == 0.0, 0.0, sc)
            sc = jnp.where(r0 + row_s <= tpos, sc, -jnp.inf)
            sc_ref[pl.ds(r0, TKS), :] = sc
            hi_ref[pl.ds(r0, TKS), :] = sc.astype(BF16)
        for p in pairs:
            rbuf_ref[p] = ahead[p]
        return carry

    lax.fori_loop(0, n_cnt, score_tile, 0)

    row_c = lax.broadcasted_iota(I32, (TKC, TQ), 0)

    def count(pred):
        def tile(j, acc):
            r0 = pl.multiple_of(j * TKC, TKC)
            c = jnp.where(pred(sc_ref[pl.ds(r0, TKC), :], r0), jnp.int32(1), jnp.int32(0))
            return acc + jnp.sum(c.reshape(TKC // 8, 8, TQ), axis=0)
        acc = lax.fori_loop(0, n_cnt, tile, jnp.zeros((8, TQ), I32))
        return jnp.sum(acc, axis=0, keepdims=True)

    def count16(cand_f):
        cand = jnp.broadcast_to(cand_f, (BF16_ROWS, TQ)).astype(BF16)
        one, zero = jnp.ones((), BF16), jnp.zeros((), BF16)

        def tile(j, acc):
            r0 = pl.multiple_of(j * TKC, TKC)
            t = hi_ref[pl.ds(r0, TKC), :].reshape(TKC // BF16_ROWS, BF16_ROWS, TQ)
            c = jnp.where(t >= cand[None], one, zero)
            parts = [c[g] for g in range(TKC // BF16_ROWS)]
            while len(parts) > 1:
                parts = [a + b for a, b in zip(parts[::2], parts[1::2])]
            return acc + parts[0].astype(F32)
        acc = lax.fori_loop(0, n_cnt, tile, jnp.zeros((BF16_ROWS, TQ), F32))
        return jnp.sum(acc, axis=0, keepdims=True)

    def bisect16(it, p):
        cand = p + lax.shift_left(jnp.int32(1), 15 - it)
        bits = lax.shift_left(jnp.where(cand < 0, cand ^ jnp.int32(0x7FFF), cand), 16)
        n = count16(pltpu.bitcast(bits, F32))
        return jnp.where(n >= topk, cand, p)

    k16 = lax.fori_loop(0, 16, bisect16, jnp.full((1, TQ), -32768, I32))
    key_v = lax.shift_left(k16, 16) + jnp.where(k16 < 0, jnp.int32(0xFFFF), jnp.int32(0))
    low = jnp.maximum(key_v, jnp.int32(INT_MIN + 0x8000)) - jnp.int32(0x8000)

    def bisect(it, p):
        bit = lax.shift_left(jnp.int32(1), STAGE2_BITS - 1 - it)
        top = jnp.int32(2 ** 31 - 1)
        cand = jnp.where(p > top - bit, top, p + bit)
        cand_f = _key_to_f32(cand)
        n = count(lambda s, r0: s >= cand_f)
        return jnp.where(n >= topk, cand, p)

    kth = lax.fori_loop(0, STAGE2_BITS, bisect, low)
    kth = jnp.maximum(kth, jnp.int32(KEY_LOWEST))
    thr = _key_to_f32(kth)
    nxt = _key_to_f32(kth + 1)
    n_gt = count(lambda s, r0: s >= nxt)
    n_ge = count(lambda s, r0: s >= thr)
    need = topk - n_gt
    has_tie = n_ge > topk
    any_tie = jnp.max(jnp.where(has_tie, 1, 0)) > 0

    lane_a = lax.broadcasted_iota(I32, (TKA, LANES), 1)
    row_a = lax.broadcasted_iota(I32, (TKA, LANES), 0)
    aug = jnp.where(lane_a < N_SLOPE_PARTS, row_a % AUG_RADIX,
                    jnp.where(lane_a < 2 * N_SLOPE_PARTS, row_a // AUG_RADIX, 0)).astype(F32).astype(BF16)
    row_q = lax.broadcasted_iota(I32, (TKA, TQ), 0)
    half2 = lax.broadcasted_iota(I32, (1, 2 * TQ), 1) < TQ

    ones_rows = jnp.ones((SUM_ROWS, TKA), BF16)

    def value_update(jt):
        for p in pairs:
            vt = jnp.concatenate([vat_ref[jt * (TKA // LANES) + t, p * LANES:(p + 1) * LANES, :]
                                  for t in range(TKA // LANES)], axis=1)
            vt = jnp.concatenate([vt, ones_rows], axis=0)
            acc_ref[p] = alpha_ref[p] * acc_ref[p] + _dot(vt, pbuf_ref[p])

    def score_dots(jt):
        r0 = pl.multiple_of(jt * TKA, TKA)
        return [_dot(jnp.concatenate([ka_ref[pl.ds(r0, TKA), p * LANES:(p + 1) * LANES], aug], axis=1), qar_ref[p])
                for p in pairs]

    for p, s in zip(pairs, score_dots(0)):
        sbuf_ref[p] = s

    def tie_cut():
        def step(it, jp):
            cand = jp | lax.shift_left(jnp.int32(1), pos_bits - 1 - it)
            n = count(lambda s, r0: (s >= thr) & (s < nxt) & (r0 + row_c < cand))
            return jnp.where(n < need, cand, jp)
        return lax.fori_loop(0, pos_bits, step, jnp.zeros((1, TQ), I32))

    cut = lax.cond(any_tie, tie_cut, lambda: jnp.zeros((1, TQ), I32))
    cut = jnp.where(has_tie, cut, jnp.int32(2 ** 30))

    def att_tile(j, carry):
        r0 = pl.multiple_of(j * TKA, TKA)
        s_idx = sc_ref[pl.ds(r0, TKA), :]
        sel = (s_idx >= thr) & ((s_idx >= nxt) | (r0 + row_q <= cut))
        bias = jnp.where(sel, 0.0, NEG_BIG)
        bias2 = jnp.concatenate([bias, bias], axis=1)
        base = (r0 - q0).astype(F32)
        value_update(jnp.maximum(j - 1, 0))
        ahead = score_dots(jnp.minimum(j + 1, n_att - 1))
        for p in pairs:
            shift = jnp.where(half2, _slope_log2(2 * p), _slope_log2(2 * p + 1)) * base
            s = sbuf_ref[p] + bias2
            m_old = m_ref[p]
            m_new = jnp.maximum(m_old, jnp.max(s, axis=0, keepdims=True) + shift)
            alpha = jnp.exp2(m_old - m_new)
            m_ref[p] = m_new
            alpha_ref[p] = alpha
            pbuf_ref[p] = jnp.exp2(s - (m_new - shift)).astype(BF16)
        for p in pairs:
            sbuf_ref[p] = ahead[p]
        return carry

    lax.fori_loop(0, n_att, att_tile, 0)
    value_update(n_att - 1)

    row_lo = lax.broadcasted_iota(I32, (LANES, TQ), 0) < HEAD_DIM_A
    for p in range(N_HEADS_A // 2):
        o = acc_ref[p, 0:LANES, :] * (1.0 / acc_ref[p, LANES:LANES + 1, :])
        ot_ref[p * LANES:(p + 1) * LANES, :] = jnp.where(row_lo, o[:, :TQ], o[:, TQ:])
    out_ref[...] = (ot_ref[...].T * sza_ref[...].astype(F32)).astype(BF16)


def _dsa_call(qa, qi, wit, sza, ka, vat, ki2, batch, seq):
    assert TQ == LANES and TKS % LANES == 0 and TKC % TKS == 0 and seq % TKC == 0
    assert TKA % LANES == 0 and TKC % TKA == 0 and TKA <= AUG_RADIX * AUG_RADIX
    nq = seq // TQ
    topk = min(TOPK_MAX, seq // 4)
    pos_bits = max(1, int(np.ceil(np.log2(seq))))
    q_spec = lambda w: pl.BlockSpec((TQ, w), lambda b, i: (b * nq + i, 0))
    b_spec = lambda w: pl.BlockSpec((seq, w), lambda b, i: (b, 0))
    n_pair = N_HEADS_A // 2
    return pl.pallas_call(
        functools.partial(_dsa_body, topk=topk, pos_bits=pos_bits),
        out_shape=jax.ShapeDtypeStruct((batch * seq, WIDTH), BF16),
        grid=(batch, nq),
        in_specs=[q_spec(WIDTH), q_spec(WIDTH), pl.BlockSpec((16, TQ), lambda b, i: (0, b * nq + i)), q_spec(WIDTH),
                  b_spec(WIDTH), pl.BlockSpec((seq // LANES, WIDTH, LANES), lambda b, i: (b, 0, 0)), b_spec(LANES)],
        out_specs=q_spec(WIDTH),
        scratch_shapes=[pltpu.VMEM((seq, TQ), F32), pltpu.VMEM((seq, TQ), BF16),
                        pltpu.VMEM((n_pair, LANES, 2 * TQ), BF16), pltpu.VMEM((n_pair, 2 * LANES, 2 * TQ), BF16),
                        pltpu.VMEM((n_pair, 1, 2 * TQ), F32), pltpu.VMEM((n_pair, 1, 2 * TQ), F32),
                        pltpu.VMEM((n_pair, LANES + SUM_ROWS, 2 * TQ), F32),
                        pltpu.VMEM((n_pair, TKA, 2 * TQ), BF16), pltpu.VMEM((n_pair, TKA, 2 * TQ), F32),
                        pltpu.VMEM((n_pair, TKC, 2 * TQ), F32), pltpu.VMEM((WIDTH, TQ), F32)],
        compiler_params=pltpu.CompilerParams(dimension_semantics=("parallel", "arbitrary"),
                                             vmem_limit_bytes=VMEM_LIMIT),
        name="dsa",
    )(qa, qi, wit, sza, ka, vat, ki2)


def _split3(v):
    hi = v.astype(BF16)
    r = v - hi.astype(F32)
    mid = r.astype(BF16)
    lo = (r - mid.astype(F32)).astype(BF16)
    return hi, mid, lo


def _hgrn_body(sqb_ref, logf_ref, kk_ref, ib_ref, ibt_ref, sgb_ref, gon_ref, out_ref, st_ref):
    @pl.when(pl.program_id(1) == 0)
    def _():
        st_ref[...] = jnp.zeros_like(st_ref)

    r_i = lax.broadcasted_iota(I32, (CHUNK, CHUNK), 0)
    c_i = lax.broadcasted_iota(I32, (CHUNK, CHUNK), 1)
    tril = c_i <= r_i
    tri = jnp.where(tril, 1.0, 0.0).astype(BF16)
    n_sub = CHUNK // SUB
    row_c = lax.broadcasted_iota(I32, (CHUNK, LANES), 0)
    row_t = lax.broadcasted_iota(I32, (T_HGRN, LANES), 0)

    def decays(nb, c):
        g = logf_ref[nb, c * CHUNK:(c + 1) * CHUNK, :]
        return sum(_dot(tri, part) for part in _split3(g))

    def operands(nb, c, b):
        rows = slice(c * CHUNK, (c + 1) * CHUNK)
        q = sqb_ref[nb, rows, :].astype(F32)
        kk = kk_ref[nb, rows, :].astype(F32)
        b_last = b[CHUNK - 1:CHUNK, :]
        ref_k = jnp.concatenate(
            [jnp.broadcast_to(b[(j + 1) * SUB - 1:(j + 1) * SUB, :], (SUB, WIDTH)) for j in range(n_sub)], axis=0)
        k_in = kk * jnp.exp(ref_k - b)
        q_out = (q * jnp.exp(b)).astype(BF16)
        k_st = (kk * jnp.exp(b_last - b)).astype(BF16)
        q_in = []
        for j in range(n_sub):
            lo = j * SUB
            qj = q[lo:, :] * jnp.exp(b[lo:, :] - b[lo + SUB - 1:lo + SUB, :])
            if lo:
                qj = jnp.concatenate([jnp.zeros((lo, WIDTH), F32), qj], axis=0)
            q_in.append(qj.astype(BF16))
        return q_in, k_in, q_out, k_st, jnp.exp(b_last)

    def intra_scores(hd, q_in, k_in):
        cols = slice(hd * HEAD_B, (hd + 1) * HEAD_B)
        qs = jnp.concatenate([qj[:, cols] for qj in q_in], axis=1)
        kh = k_in[:, cols]
        ks = jnp.concatenate(
            [jnp.where((row_c >= j * SUB) & (row_c < (j + 1) * SUB), kh, 0.0) for j in range(n_sub)],
            axis=1).astype(BF16)
        return _dot_nt(qs, ks)

    def outputs(nb, c, hd, a_raw, q_out, k_st, decay_last):
        rows = slice(c * CHUNK, (c + 1) * CHUNK)
        cols = slice(hd * HEAD_B, (hd + 1) * HEAD_B)
        a = jnp.where(tril, a_raw, 0.0).astype(BF16)
        st = st_ref[nb, hd]
        o = _dot(a, ib_ref[nb, rows, cols]) + _dot_nt(q_out[:, cols], st.astype(BF16))
        k_pad = jnp.where((row_t >= c * CHUNK) & (row_t < (c + 1) * CHUNK),
                          jnp.concatenate([k_st[:, cols]] * (T_HGRN // CHUNK), axis=0), jnp.zeros((), BF16))
        st_ref[nb, hd] = st * decay_last[:, cols] + _dot(ibt_ref[nb, 0, cols, :], k_pad)
        return o

    def finish(nb, c, hd, o):
        rows = slice(c * CHUNK, (c + 1) * CHUNK)
        cols = slice(hd * HEAD_B, (hd + 1) * HEAD_B)
        ms = jnp.mean(o * o, axis=-1, keepdims=True)
        o = o * lax.rsqrt(ms + EPS) * gon_ref[...]
        out_ref[nb, rows, cols] = (o * sgb_ref[nb, rows, cols].astype(F32)).astype(BF16)

    seqs, heads = range(NB_HGRN), range(N_HEADS_B)
    for c in range(T_HGRN // CHUNK):
        bs = [decays(nb, c) for nb in seqs]
        ops = [operands(nb, c, bs[nb]) for nb in seqs]
        raw = [[intra_scores(hd, ops[nb][0], ops[nb][1]) for hd in heads] for nb in seqs]
        outs = [[outputs(nb, c, hd, raw[nb][hd], ops[nb][2], ops[nb][3], ops[nb][4]) for hd in heads] for nb in seqs]
        for nb in seqs:
            for hd in heads:
                finish(nb, c, hd, outs[nb][hd])


def _hgrn_call(sqb, logf, kk, ib, ibt, sgb, o_norm_b, batch, seq):
    assert T_HGRN == LANES and seq % T_HGRN == 0 and batch % NB_HGRN == 0
    nt = seq // T_HGRN
    as3 = lambda a: a.reshape(batch, seq, WIDTH)
    spec = pl.BlockSpec((NB_HGRN, T_HGRN, WIDTH), lambda b, t: (b, t, 0))
    ibt_spec = pl.BlockSpec((NB_HGRN, 1, WIDTH, T_HGRN), lambda b, t: (b, t, 0, 0))
    out = pl.pallas_call(
        _hgrn_body,
        out_shape=jax.ShapeDtypeStruct((batch, seq, WIDTH), BF16),
        grid=(batch // NB_HGRN, nt),
        in_specs=[spec, spec, spec, spec, ibt_spec, spec, _const_spec((1, HEAD_B))],
        out_specs=spec,
        scratch_shapes=[pltpu.VMEM((NB_HGRN, N_HEADS_B, HEAD_B, HEAD_B), F32)],
        compiler_params=pltpu.CompilerParams(dimension_semantics=("parallel", "arbitrary"),
                                             vmem_limit_bytes=VMEM_LIMIT),
        name="hgrn",
    )(as3(sqb), as3(logf), as3(kk), as3(ib), ibt.reshape(batch, nt, WIDTH, T_HGRN), as3(sgb),
      o_norm_b.astype(F32)[None, :])
    return out.reshape(batch * seq, WIDTH)


def _merge_body(x_ref, oa_ref, ob_ref, qc_ref, szc_ref, gates_ref, kc_ref, vc_ref, wb_ref, wo_ref, out_ref, *, d_model):
    def gated(n, br):
        return gates_ref[:, n * d_model:(n + 1) * d_model].astype(F32) * _dot(br, wb_ref[n])

    heads = [slice(hd * HEAD_DIM_C, (hd + 1) * HEAD_DIM_C) for hd in range(N_HEADS_C)]
    logits = [_dot_nt(qc_ref[:, cols], kc_ref[:, cols]) for cols in heads]
    merged = gated(0, oa_ref[...])
    probs = [jnp.exp(s - jnp.max(s, axis=-1, keepdims=True)) for s in logits]
    oc = [_dot(p.astype(BF16), vc_ref[:, cols]) * (1.0 / jnp.sum(p, axis=-1, keepdims=True))
          for p, cols in zip(probs, heads)]
    merged = merged + gated(1, ob_ref[...])
    oc = (jnp.concatenate(oc, axis=1) * szc_ref[...].astype(F32)).astype(BF16)
    merged = merged + gated(2, oc)
    out_ref[...] = x_ref[...] + _dot(merged.astype(BF16), wo_ref[...])


def _merge_call(x2, oa, ob, qc, szc, gates, kc, vc, w_branch, w_out, batch, seq, mem_len):
    rows, d_model = x2.shape
    tm = TM_MERGE
    nt = seq // tm
    row_spec = lambda w: pl.BlockSpec((tm, w), lambda b, t: (b * nt + t, 0))
    mem_spec = pl.BlockSpec((mem_len, WIDTH), lambda b, t: (b, 0))
    return pl.pallas_call(
        functools.partial(_merge_body, d_model=d_model),
        out_shape=jax.ShapeDtypeStruct((rows, d_model), F32),
        grid=(batch, nt),
        in_specs=[row_spec(d_model), row_spec(WIDTH), row_spec(WIDTH), row_spec(WIDTH), row_spec(WIDTH),
                  row_spec(N_BRANCH * d_model), mem_spec, mem_spec,
                  _const_spec((N_BRANCH, WIDTH, d_model)), _const_spec((d_model, d_model))],
        out_specs=row_spec(d_model),
        compiler_params=pltpu.CompilerParams(dimension_semantics=("parallel", "parallel"),
                                             vmem_limit_bytes=VMEM_LIMIT),
        name="merge",
    )(x2, oa, ob, qc, szc, gates, kc, vc, w_branch.astype(BF16), w_out.astype(BF16))


def _layer(x, mem, norm_in, norm_mem, w_in, q_norm_a, k_norm_a, lower_bounds, o_norm_b, w_mem_kv, q_norm_c, k_norm_c,
           w_branch, w_out):
    batch, seq, d_model = x.shape
    mem_len = mem.shape[1]
    x2 = x.reshape(batch * seq, d_model)
    (qa, ka, vat, sza, qi, ki2, wit, sqb, logf, kk, ib, ibt, sgb, qc, szc, gates) = _proj_call(
        x2, norm_in, w_in, lower_bounds, q_norm_a, k_norm_a, q_norm_c)
    kc, vc = _memkv_call(mem.reshape(batch * mem_len, d_model), norm_mem, w_mem_kv, k_norm_c, mem_len)
    oa = _dsa_call(qa, qi, wit, sza, ka, vat, ki2, batch, seq)
    ob = _hgrn_call(sqb, logf, kk, ib, ibt, sgb, o_norm_b, batch, seq)
    out = _merge_call(x2, oa, ob, qc, szc, gates, kc, vc, w_branch, w_out, batch, seq, mem_len)
    return out.reshape(batch, seq, d_model)


def kernel(x, mem, norm_in, norm_mem, w_in, q_norm_a, k_norm_a, lower_bounds, o_norm_b, w_mem_kv, q_norm_c, k_norm_c,
           w_branch, w_out):
    assert norm_in.shape[0] == 1, "single-layer block"
    return _layer(x, mem, norm_in[0], norm_mem[0], w_in[0], q_norm_a[0], k_norm_a[0], lower_bounds, o_norm_b[0],
                  w_mem_kv[0], q_norm_c[0], k_norm_c[0], w_branch[0], w_out[0])
```

```python
import functools

import jax
import jax.numpy as jnp
import numpy as np
from jax import lax
from jax.experimental import pallas as pl
from jax.experimental.pallas import tpu as pltpu

F32 = jnp.float32
BF16 = jnp.bfloat16
I32 = jnp.int32

N_HEADS_A = 8
HEAD_DIM_A = 64
N_IDX_HEADS = 8
IDX_DIM = 64
TOPK_MAX = 256
N_HEADS_B = 4
HEAD_B = 128
N_HEADS_C = 4
HEAD_DIM_C = 128
N_BRANCH = 3
EPS = 1e-6
WIDTH = 512
IDX_SCALE = (IDX_DIM ** -0.5) * (N_IDX_HEADS ** -0.5)

LANES = 128
VMEM_LIMIT = 56 * 1024 * 1024

TM_PROJ = 256
TQ = 128
TKS = 256
TKA = 256
AUG_RADIX = 256
TKC = 512
SUM_ROWS = 16
CHUNK = 64
SUB = 16
T_HGRN = 128
NB_HGRN = 4
TM_MERGE = 256

NEG_BIG = -1e30
LOG2E = 1.4426950408889634
INT_MIN = -2147483648
KEY_LOWEST = -2139095040

NT = (((1,), (1,)), ((), ()))


def _dot(a, b):
    return jnp.dot(a, b, preferred_element_type=F32)


def _dot_nt(a, b):
    return lax.dot_general(a, b, NT, preferred_element_type=F32)


def _group_meansq(v, group):
    sq = v * v
    lane = lax.broadcasted_iota(I32, (v.shape[0], LANES), 1) // group
    blocks = []
    for b in range(v.shape[1] // LANES):
        blk = sq[:, b * LANES:(b + 1) * LANES]
        ms = jnp.zeros(blk.shape, F32)
        for g in range(LANES // group):
            part = blk if group == LANES else jnp.where(lane == g, blk, 0.0)
            total = jnp.sum(part, axis=-1, keepdims=True)
            ms = total + ms if group == LANES else jnp.where(lane == g, total, ms)
        blocks.append(ms)
    return jnp.concatenate(blocks, axis=1) * (1.0 / group)


_C_QA, _C_KA, _C_ZA, _C_QB, _C_FB, _C_IB, _C_GB, _C_QC, _C_ZC = [WIDTH * i for i in range(9)]
_C_GATES = 9 * WIDTH
_C_QI = _C_GATES + 3 * 1024
_C_KI = _C_QI + WIDTH
_C_END = _C_KI + LANES


def _proj_body(x_ref, nin_ref, w_ref, wvt_ref, wwt_ref, lbp_ref, gqa_ref, gka_ref, gqc_ref,
               qa_ref, ka_ref, vat_ref, sza_ref, qi_ref, ki2_ref, wit_ref,
               sqb_ref, logf_ref, kk_ref, ib_ref, ibt_ref, sgb_ref, qc_ref, szc_ref, gates_ref, *, d_model):
    x = x_ref[...]
    ms = jnp.mean(x * x, axis=-1, keepdims=True)
    h = (x * lax.rsqrt(ms + EPS)) * nin_ref[...]
    hb = h.astype(BF16)

    def proj(c0, width):
        return lambda: _dot(hb, w_ref[:, c0:c0 + width])

    def head_norm(group, gain_ref, scale, out_ref):
        def epilogue(v):
            v = v * lax.rsqrt(_group_meansq(v, group) + EPS) * gain_ref[...]
            out_ref[...] = (v * scale).astype(BF16)
        return epilogue

    def store(out_ref, fn=lambda v: v):
        def epilogue(v):
            out_ref[...] = fn(v).astype(out_ref.dtype)
        return epilogue

    def store_time_blocks(out_ref):
        def epilogue(v):
            v = v.astype(BF16)
            for t in range(out_ref.shape[0]):
                out_ref[t] = v[:, t * LANES:(t + 1) * LANES]
        return epilogue

    def forget(v):
        lbp = lbp_ref[...]
        lbe = jnp.exp(lbp - jnp.max(lbp, axis=0, keepdims=True))
        lb = lbe[0:1, :] / jnp.sum(lbe, axis=0, keepdims=True)
        f = lb + (1.0 - lb) * jax.nn.sigmoid(v)
        logf_ref[...] = jnp.log(f)
        kk_ref[...] = (1.0 - f).astype(BF16)

    def gate(n):
        def epilogue(v):
            gates_ref[:, n * d_model:(n + 1) * d_model] = jax.nn.sigmoid(v).astype(BF16)
        return epilogue

    stages = [
        (proj(_C_QA, WIDTH), head_norm(HEAD_DIM_A, gqa_ref, HEAD_DIM_A ** -0.5 * LOG2E, qa_ref)),
        (proj(_C_KA, WIDTH), head_norm(HEAD_DIM_A, gka_ref, 1.0, ka_ref)),
        (lambda: _dot_nt(wvt_ref[0:WIDTH, :], hb), store_time_blocks(vat_ref)),
        (proj(_C_ZA, WIDTH), store(sza_ref, jax.nn.silu)),
        (proj(_C_QI, WIDTH), store(qi_ref)),
        (proj(_C_KI, LANES), store(ki2_ref)),
        (lambda: _dot_nt(wwt_ref[...], hb), store(wit_ref)),
        (proj(_C_QB, WIDTH), store(sqb_ref, jax.nn.silu)),
        (proj(_C_FB, WIDTH), forget),
        (proj(_C_IB, WIDTH), store(ib_ref)),
        (lambda: _dot_nt(wvt_ref[WIDTH:2 * WIDTH, :], hb), store_time_blocks(ibt_ref)),
        (proj(_C_GB, WIDTH), store(sgb_ref, jax.nn.silu)),
        (proj(_C_QC, WIDTH), head_norm(HEAD_DIM_C, gqc_ref, HEAD_DIM_C ** -0.5, qc_ref)),
        (proj(_C_ZC, WIDTH), store(szc_ref, jax.nn.silu)),
    ] + [(proj(_C_GATES + n * d_model, d_model), gate(n)) for n in range(N_BRANCH)]
    pending = stages[0][0]()
    for k, (_, epilogue) in enumerate(stages):
        ahead = stages[k + 1][0]() if k + 1 < len(stages) else None
        epilogue(pending)
        pending = ahead


def _const_spec(shape):
    nd = len(shape)
    return pl.BlockSpec(shape, lambda *_: (0,) * nd, pipeline_mode=pl.Buffered(1))


def _proj_call(x2, norm_in, w_in, lower_bounds, q_norm_a, k_norm_a, q_norm_c):
    rows, d_model = x2.shape
    tm = TM_PROJ
    assert rows % tm == 0 and d_model == 1024
    offs = np.cumsum([0, 512, 512, 512, 512, 512, 64, 8, 512, 512, 512, 512, 512, 512, 3 * d_model])
    (o_qa, o_ka, o_va, o_za, o_qi, o_ki, o_wi, o_qb, o_fb, o_ib, o_gb, o_qc, o_zc, o_g, o_end) = [int(o) for o in offs]
    assert o_end == w_in.shape[1]
    wb = w_in.astype(BF16)
    col = lambda a, b: wb[:, a:b]
    w_main = jnp.concatenate([
        col(o_qa, o_ka), col(o_ka, o_va), col(o_za, o_qi), col(o_qb, o_fb), col(o_fb, o_ib), col(o_ib, o_gb),
        col(o_gb, o_qc), col(o_qc, o_zc), col(o_zc, o_g), col(o_g, o_end), col(o_qi, o_ki),
        col(o_ki, o_wi), col(o_ki, o_wi)], axis=1)
    assert w_main.shape[1] == _C_END
    wvt = jnp.concatenate([col(o_va, o_za).T, col(o_ib, o_gb).T], axis=0)
    wwt = jnp.concatenate([col(o_wi, o_qb).T, jnp.zeros((8, d_model), BF16)], axis=0)
    tile = lambda g, reps: jnp.tile(g.astype(F32), reps)[None, :]
    n_slots = lower_bounds.shape[0]

    row_spec = lambda w: pl.BlockSpec((tm, w), lambda i: (i, 0))
    t_spec = pl.BlockSpec((tm // LANES, WIDTH, LANES), lambda i: (i, 0, 0))
    out_shape = [
        jax.ShapeDtypeStruct((rows, WIDTH), BF16),
        jax.ShapeDtypeStruct((rows, WIDTH), BF16),
        jax.ShapeDtypeStruct((rows // LANES, WIDTH, LANES), BF16),
        jax.ShapeDtypeStruct((rows, WIDTH), BF16),
        jax.ShapeDtypeStruct((rows, WIDTH), BF16),
        jax.ShapeDtypeStruct((rows, LANES), BF16),
        jax.ShapeDtypeStruct((16, rows), F32),
        jax.ShapeDtypeStruct((rows, WIDTH), BF16),
        jax.ShapeDtypeStruct((rows, WIDTH), F32),
        jax.ShapeDtypeStruct((rows, WIDTH), BF16),
        jax.ShapeDtypeStruct((rows, WIDTH), BF16),
        jax.ShapeDtypeStruct((rows // LANES, WIDTH, LANES), BF16),
        jax.ShapeDtypeStruct((rows, WIDTH), BF16),
        jax.ShapeDtypeStruct((rows, WIDTH), BF16),
        jax.ShapeDtypeStruct((rows, WIDTH), BF16),
        jax.ShapeDtypeStruct((rows, N_BRANCH * d_model), BF16),
    ]
    out_specs = [row_spec(WIDTH), row_spec(WIDTH), t_spec, row_spec(WIDTH), row_spec(WIDTH), row_spec(LANES),
                 pl.BlockSpec((16, tm), lambda i: (0, i)),
                 row_spec(WIDTH), row_spec(WIDTH), row_spec(WIDTH), row_spec(WIDTH), t_spec, row_spec(WIDTH),
                 row_spec(WIDTH), row_spec(WIDTH), row_spec(N_BRANCH * d_model)]
    in_specs = [row_spec(d_model), _const_spec((1, d_model)), _const_spec(w_main.shape), _const_spec(wvt.shape),
                _const_spec(wwt.shape), _const_spec((n_slots, WIDTH)), _const_spec((1, WIDTH)),
                _const_spec((1, WIDTH)), _const_spec((1, WIDTH))]
    return pl.pallas_call(
        functools.partial(_proj_body, d_model=d_model),
        out_shape=out_shape, grid=(rows // tm,), in_specs=in_specs, out_specs=out_specs,
        compiler_params=pltpu.CompilerParams(dimension_semantics=("parallel",), vmem_limit_bytes=VMEM_LIMIT),
        name="proj",
    )(x2, norm_in.astype(F32)[None, :], w_main, wvt, wwt, lower_bounds.astype(F32),
      tile(q_norm_a, N_HEADS_A), tile(k_norm_a, N_HEADS_A), tile(q_norm_c, N_HEADS_C))


def _memkv_body(m_ref, nm_ref, w_ref, gk_ref, kc_ref, vc_ref):
    x = m_ref[...]
    ms = jnp.mean(x * x, axis=-1, keepdims=True)
    hb = ((x * lax.rsqrt(ms + EPS)) * nm_ref[...]).astype(BF16)
    kc = _dot(hb, w_ref[:, 0:WIDTH])
    kc = kc * lax.rsqrt(_group_meansq(kc, HEAD_DIM_C) + EPS) * gk_ref[...]
    kc_ref[...] = kc.astype(BF16)
    vc_ref[...] = _dot(hb, w_ref[:, WIDTH:2 * WIDTH]).astype(BF16)


def _memkv_call(mem2, norm_mem, w_mem_kv, k_norm_c, tm):
    rows, d_model = mem2.shape
    row_spec = lambda w: pl.BlockSpec((tm, w), lambda i: (i, 0))
    return pl.pallas_call(
        _memkv_body,
        out_shape=[jax.ShapeDtypeStruct((rows, WIDTH), BF16)] * 2,
        grid=(rows // tm,),
        in_specs=[row_spec(d_model), _const_spec((1, d_model)), _const_spec((d_model, 2 * WIDTH)),
                  _const_spec((1, WIDTH))],
        out_specs=[row_spec(WIDTH), row_spec(WIDTH)],
        compiler_params=pltpu.CompilerParams(dimension_semantics=("parallel",), vmem_limit_bytes=VMEM_LIMIT),
        name="memkv",
    )(mem2, norm_mem.astype(F32)[None, :], w_mem_kv.astype(BF16),
      jnp.tile(k_norm_c.astype(F32), N_HEADS_C)[None, :])


def _key_to_f32(k):
    return pltpu.bitcast(jnp.where(k < 0, k ^ jnp.int32(0x7FFFFFFF), k), F32)


def _slope_log2(h):
    return (2.0 ** (-8.0 * (h + 1) / N_HEADS_A)) * LOG2E


N_SLOPE_PARTS = 3


def _slope_rows(h, row):
    rest = np.float32(_slope_log2(h))
    out = jnp.zeros(row.shape, F32)
    for r in range(N_SLOPE_PARTS):
        part = np.float32(np.asarray(rest, dtype=jnp.bfloat16))
        out = jnp.where(row == r, float(part), out)
        out = jnp.where(row == N_SLOPE_PARTS + r, float(part) * AUG_RADIX, out)
        rest = np.float32(rest - part)
    return out


def _dsa_body(qa_ref, qi_ref, wit_ref, sza_ref, ka_ref, vat_ref, ki2_ref, out_ref,
              sc_ref, qir_ref, qar_ref, m_ref, alpha_ref, acc_ref, pbuf_ref, sbuf_ref, rbuf_ref, ot_ref,
              *, topk, pos_bits):
    i = pl.program_id(1)
    q0 = i * TQ
    n_att = (q0 + TQ + TKA - 1) // TKA
    n_cnt = (q0 + TQ + TKC - 1) // TKC

    row = lax.broadcasted_iota(I32, (LANES, TQ), 0)
    row_lo = row < HEAD_DIM_A
    pairs = range(N_HEADS_A // 2)

    def pair_operand(ref, p):
        t = ref[:, p * LANES:(p + 1) * LANES].astype(F32).T
        return jnp.concatenate([jnp.where(row_lo, t, 0.0), jnp.where(row_lo, 0.0, t)], axis=1).astype(BF16)

    for p in pairs:
        qir_ref[p] = pair_operand(qi_ref, p)

    def index_dots(jt):
        kt = ki2_ref[pl.ds(pl.multiple_of(jt * TKC, TKC), TKC), :]
        return [_dot(kt, qir_ref[p]) for p in pairs]

    first = index_dots(0)
    for p in pairs:
        qar_ref[p, 0:LANES, :] = pair_operand(qa_ref, p)
        qar_ref[p, LANES:2 * LANES, :] = jnp.concatenate(
            [_slope_rows(2 * p, row), _slope_rows(2 * p + 1, row)], axis=1).astype(BF16)
    m_ref[...] = jnp.full(m_ref.shape, NEG_BIG, F32)
    acc_ref[...] = jnp.zeros(acc_ref.shape, F32)
    alpha_ref[...] = jnp.ones(alpha_ref.shape, F32)
    pbuf_ref[...] = jnp.zeros(pbuf_ref.shape, BF16)
    for p in pairs:
        rbuf_ref[p] = first[p]

    row_s = lax.broadcasted_iota(I32, (TKS, TQ), 0)
    tpos = q0 + lax.broadcasted_iota(I32, (TKS, TQ), 1)

    def score_tile(j, carry):
        ahead = index_dots(jnp.minimum(j + 1, n_cnt - 1))
        for r in range(TKC // TKS):
            rows = slice(r * TKS, (r + 1) * TKS)
            r0 = pl.multiple_of(j * TKC + r * TKS, TKS)
            acc = jnp.zeros((TKS, TQ), F32)
            for p in pairs:
                rel = jnp.maximum(rbuf_ref[p, rows, :], 0.0)
                acc = acc + rel[:, :TQ] * wit_ref[2 * p:2 * p + 1, :] + rel[:, TQ:] * wit_ref[2 * p + 1:2 * p + 2, :]
            sc = acc * IDX_SCALE
            sc = jnp.where(sc == 0.0, 0.0, sc)
            sc_ref[pl.ds(r0, TKS), :] = jnp.where(r0 + row_s <= tpos, sc, -jnp.inf)
        for p in pairs:
            rbuf_ref[p] = ahead[p]
        return carry

    lax.fori_loop(0, n_cnt, score_tile, 0)

    row_c = lax.broadcasted_iota(I32, (TKC, TQ), 0)

    def count(pred):
        def tile(j, acc):
            r0 = pl.multiple_of(j * TKC, TKC)
            c = jnp.where(pred(sc_ref[pl.ds(r0, TKC), :], r0), jnp.int32(1), jnp.int32(0))
            return acc + jnp.sum(c.reshape(TKC // 8, 8, TQ), axis=0)
        acc = lax.fori_loop(0, n_cnt, tile, jnp.zeros((8, TQ), I32))
        return jnp.sum(acc, axis=0, keepdims=True)

    def bisect(it, p):
        cand = p ^ lax.shift_left(jnp.int32(1), 31 - it)
        cand_f = _key_to_f32(cand)
        n = count(lambda s, r0: s >= cand_f)
        return jnp.where(n >= topk, cand, p)

    kth = lax.fori_loop(0, 32, bisect, jnp.full((1, TQ), INT_MIN, I32))
    kth = jnp.maximum(kth, jnp.int32(KEY_LOWEST))
    thr = _key_to_f32(kth)
    nxt = _key_to_f32(kth + 1)
    n_gt = count(lambda s, r0: s >= nxt)
    n_ge = count(lambda s, r0: s >= thr)
    need = topk - n_gt
    has_tie = n_ge > topk
    any_tie = jnp.max(jnp.where(has_tie, 1, 0)) > 0

    lane_a = lax.broadcasted_iota(I32, (TKA, LANES), 1)
    row_a = lax.broadcasted_iota(I32, (TKA, LANES), 0)
    aug = jnp.where(lane_a < N_SLOPE_PARTS, row_a % AUG_RADIX,
                    jnp.where(lane_a < 2 * N_SLOPE_PARTS, row_a // AUG_RADIX, 0)).astype(F32).astype(BF16)
    row_q = lax.broadcasted_iota(I32, (TKA, TQ), 0)
    half2 = lax.broadcasted_iota(I32, (1, 2 * TQ), 1) < TQ

    ones_rows = jnp.ones((SUM_ROWS, TKA), BF16)

    def value_update(jt):
        for p in pairs:
            vt = jnp.concatenate([vat_ref[jt * (TKA // LANES) + t, p * LANES:(p + 1) * LANES, :]
                                  for t in range(TKA // LANES)], axis=1)
            vt = jnp.concatenate([vt, ones_rows], axis=0)
            acc_ref[p] = alpha_ref[p] * acc_ref[p] + _dot(vt, pbuf_ref[p])

    def score_dots(jt):
        r0 = pl.multiple_of(jt * TKA, TKA)
        return [_dot(jnp.concatenate([ka_ref[pl.ds(r0, TKA), p * LANES:(p + 1) * LANES], aug], axis=1), qar_ref[p])
                for p in pairs]

    for p, s in zip(pairs, score_dots(0)):
        sbuf_ref[p] = s

    def tie_cut():
        def step(it, jp):
            cand = jp | lax.shift_left(jnp.int32(1), pos_bits - 1 - it)
            n = count(lambda s, r0: (s >= thr) & (s < nxt) & (r0 + row_c < cand))
            return jnp.where(n < need, cand, jp)
        return lax.fori_loop(0, pos_bits, step, jnp.zeros((1, TQ), I32))

    cut = lax.cond(any_tie, tie_cut, lambda: jnp.zeros((1, TQ), I32))
    cut = jnp.where(has_tie, cut, jnp.int32(2 ** 30))

    def att_tile(j, carry):
        r0 = pl.multiple_of(j * TKA, TKA)
        s_idx = sc_ref[pl.ds(r0, TKA), :]
        sel = (s_idx >= thr) & ((s_idx >= nxt) | (r0 + row_q <= cut))
        bias = jnp.where(sel, 0.0, NEG_BIG)
        bias2 = jnp.concatenate([bias, bias], axis=1)
        base = (r0 - q0).astype(F32)
        value_update(jnp.maximum(j - 1, 0))
        ahead = score_dots(jnp.minimum(j + 1, n_att - 1))
        for p in pairs:
            shift = jnp.where(half2, _slope_log2(2 * p), _slope_log2(2 * p + 1)) * base
            s = sbuf_ref[p] + bias2
            m_old = m_ref[p]
            m_new = jnp.maximum(m_old, jnp.max(s, axis=0, keepdims=True) + shift)
            alpha = jnp.exp2(m_old - m_new)
            m_ref[p] = m_new
            alpha_ref[p] = alpha
            pbuf_ref[p] = jnp.exp2(s - (m_new - shift)).astype(BF16)
        for p in pairs:
            sbuf_ref[p] = ahead[p]
        return carry

    lax.fori_loop(0, n_att, att_tile, 0)
    value_update(n_att - 1)

    row_lo = lax.broadcasted_iota(I32, (LANES, TQ), 0) < HEAD_DIM_A
    for p in range(N_HEADS_A // 2):
        o = acc_ref[p, 0:LANES, :] * (1.0 / acc_ref[p, LANES:LANES + 1, :])
        ot_ref[p * LANES:(p + 1) * LANES, :] = jnp.where(row_lo, o[:, :TQ], o[:, TQ:])
    out_ref[...] = (ot_ref[...].T * sza_ref[...].astype(F32)).astype(BF16)


def _dsa_call(qa, qi, wit, sza, ka, vat, ki2, batch, seq):
    assert TQ == LANES and TKS % LANES == 0 and TKC % TKS == 0 and seq % TKC == 0
    assert TKA % LANES == 0 and TKC % TKA == 0 and TKA <= AUG_RADIX * AUG_RADIX
    nq = seq // TQ
    topk = min(TOPK_MAX, seq // 4)
    pos_bits = max(1, int(np.ceil(np.log2(seq))))
    q_spec = lambda w: pl.BlockSpec((TQ, w), lambda b, i: (b * nq + i, 0))
    b_spec = lambda w: pl.BlockSpec((seq, w), lambda b, i: (b, 0))
    n_pair = N_HEADS_A // 2
    return pl.pallas_call(
        functools.partial(_dsa_body, topk=topk, pos_bits=pos_bits),
        out_shape=jax.ShapeDtypeStruct((batch * seq, WIDTH), BF16),
        grid=(batch, nq),
        in_specs=[q_spec(WIDTH), q_spec(WIDTH), pl.BlockSpec((16, TQ), lambda b, i: (0, b * nq + i)), q_spec(WIDTH),
                  b_spec(WIDTH), pl.BlockSpec((seq // LANES, WIDTH, LANES), lambda b, i: (b, 0, 0)), b_spec(LANES)],
        out_specs=q_spec(WIDTH),
        scratch_shapes=[pltpu.VMEM((seq, TQ), F32),
                        pltpu.VMEM((n_pair, LANES, 2 * TQ), BF16), pltpu.VMEM((n_pair, 2 * LANES, 2 * TQ), BF16),
                        pltpu.VMEM((n_pair, 1, 2 * TQ), F32), pltpu.VMEM((n_pair, 1, 2 * TQ), F32),
                        pltpu.VMEM((n_pair, LANES + SUM_ROWS, 2 * TQ), F32),
                        pltpu.VMEM((n_pair, TKA, 2 * TQ), BF16), pltpu.VMEM((n_pair, TKA, 2 * TQ), F32),
                        pltpu.VMEM((n_pair, TKC, 2 * TQ), F32), pltpu.VMEM((WIDTH, TQ), F32)],
        compiler_params=pltpu.CompilerParams(dimension_semantics=("parallel", "arbitrary"),
                                             vmem_limit_bytes=VMEM_LIMIT),
        name="dsa",
    )(qa, qi, wit, sza, ka, vat, ki2)


def _split3(v):
    hi = v.astype(BF16)
    r = v - hi.astype(F32)
    mid = r.astype(BF16)
    lo = (r - mid.astype(F32)).astype(BF16)
    return hi, mid, lo


def _hgrn_body(sqb_ref, logf_ref, kk_ref, ib_ref, ibt_ref, sgb_ref, gon_ref, out_ref, st_ref):
    @pl.when(pl.program_id(1) == 0)
    def _():
        st_ref[...] = jnp.zeros_like(st_ref)

    r_i = lax.broadcasted_iota(I32, (CHUNK, CHUNK), 0)
    c_i = lax.broadcasted_iota(I32, (CHUNK, CHUNK), 1)
    tril = c_i <= r_i
    tri = jnp.where(tril, 1.0, 0.0).astype(BF16)
    n_sub = CHUNK // SUB
    row_c = lax.broadcasted_iota(I32, (CHUNK, LANES), 0)
    row_t = lax.broadcasted_iota(I32, (T_HGRN, LANES), 0)

    def decays(nb, c):
        g = logf_ref[nb, c * CHUNK:(c + 1) * CHUNK, :]
        return sum(_dot(tri, part) for part in _split3(g))

    def operands(nb, c, b):
        rows = slice(c * CHUNK, (c + 1) * CHUNK)
        q = sqb_ref[nb, rows, :].astype(F32)
        kk = kk_ref[nb, rows, :].astype(F32)
        b_last = b[CHUNK - 1:CHUNK, :]
        ref_k = jnp.concatenate(
            [jnp.broadcast_to(b[(j + 1) * SUB - 1:(j + 1) * SUB, :], (SUB, WIDTH)) for j in range(n_sub)], axis=0)
        k_in = kk * jnp.exp(ref_k - b)
        q_out = (q * jnp.exp(b)).astype(BF16)
        k_st = (kk * jnp.exp(b_last - b)).astype(BF16)
        q_in = []
        for j in range(n_sub):
            lo = j * SUB
            qj = q[lo:, :] * jnp.exp(b[lo:, :] - b[lo + SUB - 1:lo + SUB, :])
            if lo:
                qj = jnp.concatenate([jnp.zeros((lo, WIDTH), F32), qj], axis=0)
            q_in.append(qj.astype(BF16))
        return q_in, k_in, q_out, k_st, jnp.exp(b_last)

    def intra_scores(hd, q_in, k_in):
        cols = slice(hd * HEAD_B, (hd + 1) * HEAD_B)
        qs = jnp.concatenate([qj[:, cols] for qj in q_in], axis=1)
        kh = k_in[:, cols]
        ks = jnp.concatenate(
            [jnp.where((row_c >= j * SUB) & (row_c < (j + 1) * SUB), kh, 0.0) for j in range(n_sub)],
            axis=1).astype(BF16)
        return _dot_nt(qs, ks)

    def outputs(nb, c, hd, a_raw, q_out, k_st, decay_last):
        rows = slice(c * CHUNK, (c + 1) * CHUNK)
        cols = slice(hd * HEAD_B, (hd + 1) * HEAD_B)
        a = jnp.where(tril, a_raw, 0.0).astype(BF16)
        st = st_ref[nb, hd]
        o = _dot(a, ib_ref[nb, rows, cols]) + _dot_nt(q_out[:, cols], st.astype(BF16))
        k_pad = jnp.where((row_t >= c * CHUNK) & (row_t < (c + 1) * CHUNK),
                          jnp.concatenate([k_st[:, cols]] * (T_HGRN // CHUNK), axis=0), jnp.zeros((), BF16))
        st_ref[nb, hd] = st * decay_last[:, cols] + _dot(ibt_ref[nb, 0, cols, :], k_pad)
        return o

    def finish(nb, c, hd, o):
        rows = slice(c * CHUNK, (c + 1) * CHUNK)
        cols = slice(hd * HEAD_B, (hd + 1) * HEAD_B)
        ms = jnp.mean(o * o, axis=-1, keepdims=True)
        o = o * lax.rsqrt(ms + EPS) * gon_ref[...]
        out_ref[nb, rows, cols] = (o * sgb_ref[nb, rows, cols].astype(F32)).astype(BF16)

    seqs, heads = range(NB_HGRN), range(N_HEADS_B)
    for c in range(T_HGRN // CHUNK):
        bs = [decays(nb, c) for nb in seqs]
        ops = [operands(nb, c, bs[nb]) for nb in seqs]
        raw = [[intra_scores(hd, ops[nb][0], ops[nb][1]) for hd in heads] for nb in seqs]
        outs = [[outputs(nb, c, hd, raw[nb][hd], ops[nb][2], ops[nb][3], ops[nb][4]) for hd in heads] for nb in seqs]
        for nb in seqs:
            for hd in heads:
                finish(nb, c, hd, outs[nb][hd])


def _hgrn_call(sqb, logf, kk, ib, ibt, sgb, o_norm_b, batch, seq):
    assert T_HGRN == LANES and seq % T_HGRN == 0 and batch % NB_HGRN == 0
    nt = seq // T_HGRN
    as3 = lambda a: a.reshape(batch, seq, WIDTH)
    spec = pl.BlockSpec((NB_HGRN, T_HGRN, WIDTH), lambda b, t: (b, t, 0))
    ibt_spec = pl.BlockSpec((NB_HGRN, 1, WIDTH, T_HGRN), lambda b, t: (b, t, 0, 0))
    out = pl.pallas_call(
        _hgrn_body,
        out_shape=jax.ShapeDtypeStruct((batch, seq, WIDTH), BF16),
        grid=(batch // NB_HGRN, nt),
        in_specs=[spec, spec, spec, spec, ibt_spec, spec, _const_spec((1, HEAD_B))],
        out_specs=spec,
        scratch_shapes=[pltpu.VMEM((NB_HGRN, N_HEADS_B, HEAD_B, HEAD_B), F32)],
        compiler_params=pltpu.CompilerParams(dimension_semantics=("parallel", "arbitrary"),
                                             vmem_limit_bytes=VMEM_LIMIT),
        name="hgrn",
    )(as3(sqb), as3(logf), as3(kk), as3(ib), ibt.reshape(batch, nt, WIDTH, T_HGRN), as3(sgb),
      o_norm_b.astype(F32)[None, :])
    return out.reshape(batch * seq, WIDTH)


def _merge_body(x_ref, oa_ref, ob_ref, qc_ref, szc_ref, gates_ref, kc_ref, vc_ref, wb_ref, wo_ref, out_ref, *, d_model):
    def gated(n, br):
        return gates_ref[:, n * d_model:(n + 1) * d_model].astype(F32) * _dot(br, wb_ref[n])

    heads = [slice(hd * HEAD_DIM_C, (hd + 1) * HEAD_DIM_C) for hd in range(N_HEADS_C)]
    logits = [_dot_nt(qc_ref[:, cols], kc_ref[:, cols]) for cols in heads]
    merged = gated(0, oa_ref[...])
    probs = [jnp.exp(s - jnp.max(s, axis=-1, keepdims=True)) for s in logits]
    oc = [_dot(p.astype(BF16), vc_ref[:, cols]) * (1.0 / jnp.sum(p, axis=-1, keepdims=True))
          for p, cols in zip(probs, heads)]
    merged = merged + gated(1, ob_ref[...])
    oc = (jnp.concatenate(oc, axis=1) * szc_ref[...].astype(F32)).astype(BF16)
    merged = merged + gated(2, oc)
    out_ref[...] = x_ref[...] + _dot(merged.astype(BF16), wo_ref[...])


def _merge_call(x2, oa, ob, qc, szc, gates, kc, vc, w_branch, w_out, batch, seq, mem_len):
    rows, d_model = x2.shape
    tm = TM_MERGE
    nt = seq // tm
    row_spec = lambda w: pl.BlockSpec((tm, w), lambda b, t: (b * nt + t, 0))
    mem_spec = pl.BlockSpec((mem_len, WIDTH), lambda b, t: (b, 0))
    return pl.pallas_call(
        functools.partial(_merge_body, d_model=d_model),
        out_shape=jax.ShapeDtypeStruct((rows, d_model), F32),
        grid=(batch, nt),
        in_specs=[row_spec(d_model), row_spec(WIDTH), row_spec(WIDTH), row_spec(WIDTH), row_spec(WIDTH),
                  row_spec(N_BRANCH * d_model), mem_spec, mem_spec,
                  _const_spec((N_BRANCH, WIDTH, d_model)), _const_spec((d_model, d_model))],
        out_specs=row_spec(d_model),
        compiler_params=pltpu.CompilerParams(dimension_semantics=("parallel", "parallel"),
                                             vmem_limit_bytes=VMEM_LIMIT),
        name="merge",
    )(x2, oa, ob, qc, szc, gates, kc, vc, w_branch.astype(BF16), w_out.astype(BF16))


def _layer(x, mem, norm_in, norm_mem, w_in, q_norm_a, k_norm_a, lower_bounds, o_norm_b, w_mem_kv, q_norm_c, k_norm_c,
           w_branch, w_out):
    batch, seq, d_model = x.shape
    mem_len = mem.shape[1]
    x2 = x.reshape(batch * seq, d_model)
    (qa, ka, vat, sza, qi, ki2, wit, sqb, logf, kk, ib, ibt, sgb, qc, szc, gates) = _proj_call(
        x2, norm_in, w_in, lower_bounds, q_norm_a, k_norm_a, q_norm_c)
    kc, vc = _memkv_call(mem.reshape(batch * mem_len, d_model), norm_mem, w_mem_kv, k_norm_c, mem_len)
    oa = _dsa_call(qa, qi, wit, sza, ka, vat, ki2, batch, seq)
    ob = _hgrn_call(sqb, logf, kk, ib, ibt, sgb, o_norm_b, batch, seq)
    out = _merge_call(x2, oa, ob, qc, szc, gates, kc, vc, w_branch, w_out, batch, seq, mem_len)
    return out.reshape(batch, seq, d_model)


def kernel(x, mem, norm_in, norm_mem, w_in, q_norm_a, k_norm_a, lower_bounds, o_norm_b, w_mem_kv, q_norm_c, k_norm_c,
           w_branch, w_out):
    assert norm_in.shape[0] == 1, "single-layer block"
    return _layer(x, mem, norm_in[0], norm_mem[0], w_in[0], q_norm_a[0], k_norm_a[0], lower_bounds, o_norm_b[0],
                  w_mem_kv[0], q_norm_c[0], k_norm_c[0], w_branch[0], w_out[0])
```

```python
import functools

import jax
import jax.numpy as jnp
import numpy as np
from jax import lax
from jax.experimental import pallas as pl
from jax.experimental.pallas import tpu as pltpu

F32 = jnp.float32
BF16 = jnp.bfloat16
I32 = jnp.int32

N_HEADS_A = 8
HEAD_DIM_A = 64
N_IDX_HEADS = 8
IDX_DIM = 64
TOPK_MAX = 256
N_HEADS_B = 4
HEAD_B = 128
N_HEADS_C = 4
HEAD_DIM_C = 128
N_BRANCH = 3
EPS = 1e-6
WIDTH = 512
IDX_SCALE = (IDX_DIM ** -0.5) * (N_IDX_HEADS ** -0.5)

LANES = 128
VMEM_LIMIT = 56 * 1024 * 1024

TM_PROJ = 256
TQ = 128
TKS = 256
TKA = 256
AUG_RADIX = 256
TKC = 512
SUM_ROWS = 16
CHUNK = 64
SUB = 16
T_HGRN = 128
NB_HGRN = 4
TM_MERGE = 256

NEG_BIG = -1e30
LOG2E = 1.4426950408889634
INT_MIN = -2147483648
KEY_LOWEST = -2139095040

NT = (((1,), (1,)), ((), ()))


def _dot(a, b):
    return jnp.dot(a, b, preferred_element_type=F32)


def _dot_nt(a, b):
    return lax.dot_general(a, b, NT, preferred_element_type=F32)


def _group_meansq(v, group):
    sq = v * v
    lane = lax.broadcasted_iota(I32, (v.shape[0], LANES), 1) // group
    blocks = []
    for b in range(v.shape[1] // LANES):
        blk = sq[:, b * LANES:(b + 1) * LANES]
        ms = jnp.zeros(blk.shape, F32)
        for g in range(LANES // group):
            part = blk if group == LANES else jnp.where(lane == g, blk, 0.0)
            total = jnp.sum(part, axis=-1, keepdims=True)
            ms = total + ms if group == LANES else jnp.where(lane == g, total, ms)
        blocks.append(ms)
    return jnp.concatenate(blocks, axis=1) * (1.0 / group)


_C_QA, _C_KA, _C_ZA, _C_QB, _C_FB, _C_IB, _C_GB, _C_QC, _C_ZC = [WIDTH * i for i in range(9)]
_C_GATES = 9 * WIDTH
_C_QI = _C_GATES + 3 * 1024
_C_KI = _C_QI + WIDTH
_C_END = _C_KI + LANES


def _proj_body(x_ref, nin_ref, w_ref, wvt_ref, wwt_ref, lbp_ref, gqa_ref, gka_ref, gqc_ref,
               qa_ref, ka_ref, vat_ref, sza_ref, qi_ref, ki2_ref, wit_ref,
               sqb_ref, logf_ref, kk_ref, ib_ref, ibt_ref, sgb_ref, qc_ref, szc_ref, gates_ref, *, d_model):
    x = x_ref[...]
    ms = jnp.mean(x * x, axis=-1, keepdims=True)
    h = (x * lax.rsqrt(ms + EPS)) * nin_ref[...]
    hb = h.astype(BF16)

    def proj(c0, width):
        return lambda: _dot(hb, w_ref[:, c0:c0 + width])

    def head_norm(group, gain_ref, scale, out_ref):
        def epilogue(v):
            v = v * lax.rsqrt(_group_meansq(v, group) + EPS) * gain_ref[...]
            out_ref[...] = (v * scale).astype(BF16)
        return epilogue

    def store(out_ref, fn=lambda v: v):
        def epilogue(v):
            out_ref[...] = fn(v).astype(out_ref.dtype)
        return epilogue

    def store_time_blocks(out_ref):
        def epilogue(v):
            v = v.astype(BF16)
            for t in range(out_ref.shape[0]):
                out_ref[t] = v[:, t * LANES:(t + 1) * LANES]
        return epilogue

    def store_both_layouts(out_ref, out_t_ref):
        def epilogue(v):
            out_ref[...] = v.astype(BF16)
            store_time_blocks(out_t_ref)(v.T)
        return epilogue

    def forget(v):
        lbp = lbp_ref[...]
        lbe = jnp.exp(lbp - jnp.max(lbp, axis=0, keepdims=True))
        lb = lbe[0:1, :] / jnp.sum(lbe, axis=0, keepdims=True)
        f = lb + (1.0 - lb) * jax.nn.sigmoid(v)
        logf_ref[...] = jnp.log(f)
        kk_ref[...] = (1.0 - f).astype(BF16)

    def gate(n):
        def epilogue(v):
            gates_ref[:, n * d_model:(n + 1) * d_model] = jax.nn.sigmoid(v).astype(BF16)
        return epilogue

    stages = [
        (proj(_C_QA, WIDTH), head_norm(HEAD_DIM_A, gqa_ref, HEAD_DIM_A ** -0.5 * LOG2E, qa_ref)),
        (proj(_C_KA, WIDTH), head_norm(HEAD_DIM_A, gka_ref, 1.0, ka_ref)),
        (lambda: _dot_nt(wvt_ref[...], hb), store_time_blocks(vat_ref)),
        (proj(_C_ZA, WIDTH), store(sza_ref, jax.nn.silu)),
        (proj(_C_QI, WIDTH), store(qi_ref)),
        (proj(_C_KI, LANES), store(ki2_ref)),
        (lambda: _dot_nt(wwt_ref[...], hb), store(wit_ref)),
        (proj(_C_QB, WIDTH), store(sqb_ref, jax.nn.silu)),
        (proj(_C_FB, WIDTH), forget),
        (proj(_C_IB, WIDTH), store_both_layouts(ib_ref, ibt_ref)),
        (proj(_C_GB, WIDTH), store(sgb_ref, jax.nn.silu)),
        (proj(_C_QC, WIDTH), head_norm(HEAD_DIM_C, gqc_ref, HEAD_DIM_C ** -0.5, qc_ref)),
        (proj(_C_ZC, WIDTH), store(szc_ref, jax.nn.silu)),
    ] + [(proj(_C_GATES + n * d_model, d_model), gate(n)) for n in range(N_BRANCH)]
    pending = stages[0][0]()
    for k, (_, epilogue) in enumerate(stages):
        ahead = stages[k + 1][0]() if k + 1 < len(stages) else None
        epilogue(pending)
        pending = ahead


def _const_spec(shape):
    nd = len(shape)
    return pl.BlockSpec(shape, lambda *_: (0,) * nd, pipeline_mode=pl.Buffered(1))


def _proj_call(x2, norm_in, w_in, lower_bounds, q_norm_a, k_norm_a, q_norm_c):
    rows, d_model = x2.shape
    tm = TM_PROJ
    assert rows % tm == 0 and d_model == 1024
    offs = np.cumsum([0, 512, 512, 512, 512, 512, 64, 8, 512, 512, 512, 512, 512, 512, 3 * d_model])
    (o_qa, o_ka, o_va, o_za, o_qi, o_ki, o_wi, o_qb, o_fb, o_ib, o_gb, o_qc, o_zc, o_g, o_end) = [int(o) for o in offs]
    assert o_end == w_in.shape[1]
    wb = w_in.astype(BF16)
    col = lambda a, b: wb[:, a:b]
    w_main = jnp.concatenate([
        col(o_qa, o_ka), col(o_ka, o_va), col(o_za, o_qi), col(o_qb, o_fb), col(o_fb, o_ib), col(o_ib, o_gb),
        col(o_gb, o_qc), col(o_qc, o_zc), col(o_zc, o_g), col(o_g, o_end), col(o_qi, o_ki),
        col(o_ki, o_wi), col(o_ki, o_wi)], axis=1)
    assert w_main.shape[1] == _C_END
    wvt = col(o_va, o_za).T
    wwt = jnp.concatenate([col(o_wi, o_qb).T, jnp.zeros((8, d_model), BF16)], axis=0)
    tile = lambda g, reps: jnp.tile(g.astype(F32), reps)[None, :]
    n_slots = lower_bounds.shape[0]

    row_spec = lambda w: pl.BlockSpec((tm, w), lambda i: (i, 0))
    t_spec = pl.BlockSpec((tm // LANES, WIDTH, LANES), lambda i: (i, 0, 0))
    out_shape = [
        jax.ShapeDtypeStruct((rows, WIDTH), BF16),
        jax.ShapeDtypeStruct((rows, WIDTH), BF16),
        jax.ShapeDtypeStruct((rows // LANES, WIDTH, LANES), BF16),
        jax.ShapeDtypeStruct((rows, WIDTH), BF16),
        jax.ShapeDtypeStruct((rows, WIDTH), BF16),
        jax.ShapeDtypeStruct((rows, LANES), BF16),
        jax.ShapeDtypeStruct((16, rows), F32),
        jax.ShapeDtypeStruct((rows, WIDTH), BF16),
        jax.ShapeDtypeStruct((rows, WIDTH), F32),
        jax.ShapeDtypeStruct((rows, WIDTH), BF16),
        jax.ShapeDtypeStruct((rows, WIDTH), BF16),
        jax.ShapeDtypeStruct((rows // LANES, WIDTH, LANES), BF16),
        jax.ShapeDtypeStruct((rows, WIDTH), BF16),
        jax.ShapeDtypeStruct((rows, WIDTH), BF16),
        jax.ShapeDtypeStruct((rows, WIDTH), BF16),
        jax.ShapeDtypeStruct((rows, N_BRANCH * d_model), BF16),
    ]
    out_specs = [row_spec(WIDTH), row_spec(WIDTH), t_spec, row_spec(WIDTH), row_spec(WIDTH), row_spec(LANES),
                 pl.BlockSpec((16, tm), lambda i: (0, i)),
                 row_spec(WIDTH), row_spec(WIDTH), row_spec(WIDTH), row_spec(WIDTH), t_spec, row_spec(WIDTH),
                 row_spec(WIDTH), row_spec(WIDTH), row_spec(N_BRANCH * d_model)]
    in_specs = [row_spec(d_model), _const_spec((1, d_model)), _const_spec(w_main.shape), _const_spec(wvt.shape),
                _const_spec(wwt.shape), _const_spec((n_slots, WIDTH)), _const_spec((1, WIDTH)),
                _const_spec((1, WIDTH)), _const_spec((1, WIDTH))]
    return pl.pallas_call(
        functools.partial(_proj_body, d_model=d_model),
        out_shape=out_shape, grid=(rows // tm,), in_specs=in_specs, out_specs=out_specs,
        compiler_params=pltpu.CompilerParams(dimension_semantics=("parallel",), vmem_limit_bytes=VMEM_LIMIT),
        name="proj",
    )(x2, norm_in.astype(F32)[None, :], w_main, wvt, wwt, lower_bounds.astype(F32),
      tile(q_norm_a, N_HEADS_A), tile(k_norm_a, N_HEADS_A), tile(q_norm_c, N_HEADS_C))


def _memkv_body(m_ref, nm_ref, w_ref, gk_ref, kc_ref, vc_ref):
    x = m_ref[...]
    ms = jnp.mean(x * x, axis=-1, keepdims=True)
    hb = ((x * lax.rsqrt(ms + EPS)) * nm_ref[...]).astype(BF16)
    kc = _dot(hb, w_ref[:, 0:WIDTH])
    kc = kc * lax.rsqrt(_group_meansq(kc, HEAD_DIM_C) + EPS) * gk_ref[...]
    kc_ref[...] = kc.astype(BF16)
    vc_ref[...] = _dot(hb, w_ref[:, WIDTH:2 * WIDTH]).astype(BF16)


def _memkv_call(mem2, norm_mem, w_mem_kv, k_norm_c, tm):
    rows, d_model = mem2.shape
    row_spec = lambda w: pl.BlockSpec((tm, w), lambda i: (i, 0))
    return pl.pallas_call(
        _memkv_body,
        out_shape=[jax.ShapeDtypeStruct((rows, WIDTH), BF16)] * 2,
        grid=(rows // tm,),
        in_specs=[row_spec(d_model), _const_spec((1, d_model)), _const_spec((d_model, 2 * WIDTH)),
                  _const_spec((1, WIDTH))],
        out_specs=[row_spec(WIDTH), row_spec(WIDTH)],
        compiler_params=pltpu.CompilerParams(dimension_semantics=("parallel",), vmem_limit_bytes=VMEM_LIMIT),
        name="memkv",
    )(mem2, norm_mem.astype(F32)[None, :], w_mem_kv.astype(BF16),
      jnp.tile(k_norm_c.astype(F32), N_HEADS_C)[None, :])


def _key_to_f32(k):
    return pltpu.bitcast(jnp.where(k < 0, k ^ jnp.int32(0x7FFFFFFF), k), F32)


def _slope_log2(h):
    return (2.0 ** (-8.0 * (h + 1) / N_HEADS_A)) * LOG2E


N_SLOPE_PARTS = 3


def _slope_rows(h, row):
    rest = np.float32(_slope_log2(h))
    out = jnp.zeros(row.shape, F32)
    for r in range(N_SLOPE_PARTS):
        part = np.float32(np.asarray(rest, dtype=jnp.bfloat16))
        out = jnp.where(row == r, float(part), out)
        out = jnp.where(row == N_SLOPE_PARTS + r, float(part) * AUG_RADIX, out)
        rest = np.float32(rest - part)
    return out


def _dsa_body(qa_ref, qi_ref, wit_ref, sza_ref, ka_ref, vat_ref, ki2_ref, out_ref,
              sc_ref, qir_ref, qar_ref, m_ref, alpha_ref, acc_ref, pbuf_ref, sbuf_ref, rbuf_ref, ot_ref,
              *, topk, pos_bits):
    i = pl.program_id(1)
    q0 = i * TQ
    n_att = (q0 + TQ + TKA - 1) // TKA
    n_cnt = (q0 + TQ + TKC - 1) // TKC

    row = lax.broadcasted_iota(I32, (LANES, TQ), 0)
    row_lo = row < HEAD_DIM_A
    pairs = range(N_HEADS_A // 2)

    def pair_operand(ref, p):
        t = ref[:, p * LANES:(p + 1) * LANES].astype(F32).T
        return jnp.concatenate([jnp.where(row_lo, t, 0.0), jnp.where(row_lo, 0.0, t)], axis=1).astype(BF16)

    for p in pairs:
        qir_ref[p] = pair_operand(qi_ref, p)

    def index_dots(jt):
        kt = ki2_ref[pl.ds(pl.multiple_of(jt * TKC, TKC), TKC), :]
        return [_dot(kt, qir_ref[p]) for p in pairs]

    first = index_dots(0)
    for p in pairs:
        qar_ref[p, 0:LANES, :] = pair_operand(qa_ref, p)
        qar_ref[p, LANES:2 * LANES, :] = jnp.concatenate(
            [_slope_rows(2 * p, row), _slope_rows(2 * p + 1, row)], axis=1).astype(BF16)
    m_ref[...] = jnp.full(m_ref.shape, NEG_BIG, F32)
    acc_ref[...] = jnp.zeros(acc_ref.shape, F32)
    alpha_ref[...] = jnp.ones(alpha_ref.shape, F32)
    pbuf_ref[...] = jnp.zeros(pbuf_ref.shape, BF16)
    for p in pairs:
        rbuf_ref[p] = first[p]

    row_s = lax.broadcasted_iota(I32, (TKS, TQ), 0)
    tpos = q0 + lax.broadcasted_iota(I32, (TKS, TQ), 1)

    def score_tile(j, carry, look_ahead=True):
        ahead = index_dots(j + 1) if look_ahead else None
        for r in range(TKC // TKS):
            rows = slice(r * TKS, (r + 1) * TKS)
            r0 = pl.multiple_of(j * TKC + r * TKS, TKS)
            acc = jnp.zeros((TKS, TQ), F32)
            for p in pairs:
                rel = jnp.maximum(rbuf_ref[p, rows, :], 0.0)
                acc = acc + rel[:, :TQ] * wit_ref[2 * p:2 * p + 1, :] + rel[:, TQ:] * wit_ref[2 * p + 1:2 * p + 2, :]
            sc = acc * IDX_SCALE
            sc = jnp.where(sc == 0.0, 0.0, sc)
            sc_ref[pl.ds(r0, TKS), :] = jnp.where(r0 + row_s <= tpos, sc, -jnp.inf)
        if look_ahead:
            for p in pairs:
                rbuf_ref[p] = ahead[p]
        return carry

    lax.fori_loop(0, n_cnt - 1, score_tile, 0)
    score_tile(n_cnt - 1, 0, look_ahead=False)

    row_c = lax.broadcasted_iota(I32, (TKC, TQ), 0)

    def count(pred):
        def tile(j, acc):
            r0 = pl.multiple_of(j * TKC, TKC)
            c = jnp.where(pred(sc_ref[pl.ds(r0, TKC), :], r0), jnp.int32(1), jnp.int32(0))
            return acc + jnp.sum(c.reshape(TKC // 8, 8, TQ), axis=0)
        acc = lax.fori_loop(0, n_cnt, tile, jnp.zeros((8, TQ), I32))
        return jnp.sum(acc, axis=0, keepdims=True)

    def bisect(it, p):
        cand = p ^ lax.shift_left(jnp.int32(1), 31 - it)
        cand_f = _key_to_f32(cand)
        n = count(lambda s, r0: s >= cand_f)
        return jnp.where(n >= topk, cand, p)

    kth = lax.fori_loop(0, 32, bisect, jnp.full((1, TQ), INT_MIN, I32))
    kth = jnp.maximum(kth, jnp.int32(KEY_LOWEST))
    thr = _key_to_f32(kth)
    nxt = _key_to_f32(kth + 1)
    n_gt = count(lambda s, r0: s >= nxt)
    n_ge = count(lambda s, r0: s >= thr)
    need = topk - n_gt
    has_tie = n_ge > topk
    any_tie = jnp.max(jnp.where(has_tie, 1, 0)) > 0

    lane_a = lax.broadcasted_iota(I32, (TKA, LANES), 1)
    row_a = lax.broadcasted_iota(I32, (TKA, LANES), 0)
    aug = jnp.where(lane_a < N_SLOPE_PARTS, row_a % AUG_RADIX,
                    jnp.where(lane_a < 2 * N_SLOPE_PARTS, row_a // AUG_RADIX, 0)).astype(F32).astype(BF16)
    row_q = lax.broadcasted_iota(I32, (TKA, TQ), 0)
    half2 = lax.broadcasted_iota(I32, (1, 2 * TQ), 1) < TQ

    ones_rows = jnp.ones((SUM_ROWS, TKA), BF16)

    def value_update(jt):
        for p in pairs:
            vt = jnp.concatenate([vat_ref[jt * (TKA // LANES) + t, p * LANES:(p + 1) * LANES, :]
                                  for t in range(TKA // LANES)], axis=1)
            vt = jnp.concatenate([vt, ones_rows], axis=0)
            acc_ref[p] = alpha_ref[p] * acc_ref[p] + _dot(vt, pbuf_ref[p])

    def score_dots(jt):
        r0 = pl.multiple_of(jt * TKA, TKA)
        return [_dot(jnp.concatenate([ka_ref[pl.ds(r0, TKA), p * LANES:(p + 1) * LANES], aug], axis=1), qar_ref[p])
                for p in pairs]

    for p, s in zip(pairs, score_dots(0)):
        sbuf_ref[p] = s

    def tie_cut():
        def step(it, jp):
            cand = jp | lax.shift_left(jnp.int32(1), pos_bits - 1 - it)
            n = count(lambda s, r0: (s >= thr) & (s < nxt) & (r0 + row_c < cand))
            return jnp.where(n < need, cand, jp)
        return lax.fori_loop(0, pos_bits, step, jnp.zeros((1, TQ), I32))

    cut = lax.cond(any_tie, tie_cut, lambda: jnp.zeros((1, TQ), I32))
    cut = jnp.where(has_tie, cut, jnp.int32(2 ** 30))

    def att_tile(j, carry, look_ahead=True):
        r0 = pl.multiple_of(j * TKA, TKA)
        s_idx = sc_ref[pl.ds(r0, TKA), :]
        sel = (s_idx >= thr) & ((s_idx >= nxt) | (r0 + row_q <= cut))
        bias = jnp.where(sel, 0.0, NEG_BIG)
        bias2 = jnp.concatenate([bias, bias], axis=1)
        base = (r0 - q0).astype(F32)
        value_update(jnp.maximum(j - 1, 0))
        ahead = score_dots(j + 1) if look_ahead else None
        for p in pairs:
            shift = jnp.where(half2, _slope_log2(2 * p), _slope_log2(2 * p + 1)) * base
            s = sbuf_ref[p] + bias2
            m_old = m_ref[p]
            m_new = jnp.maximum(m_old, jnp.max(s, axis=0, keepdims=True) + shift)
            alpha = jnp.exp2(m_old - m_new)
            m_ref[p] = m_new
            alpha_ref[p] = alpha
            pbuf_ref[p] = jnp.exp2(s - (m_new - shift)).astype(BF16)
        if look_ahead:
            for p in pairs:
                sbuf_ref[p] = ahead[p]
        return carry

    lax.fori_loop(0, n_att - 1, att_tile, 0)
    att_tile(n_att - 1, 0, look_ahead=False)
    value_update(n_att - 1)

    row_lo = lax.broadcasted_iota(I32, (LANES, TQ), 0) < HEAD_DIM_A
    for p in range(N_HEADS_A // 2):
        o = acc_ref[p, 0:LANES, :] * (1.0 / acc_ref[p, LANES:LANES + 1, :])
        ot_ref[p * LANES:(p + 1) * LANES, :] = jnp.where(row_lo, o[:, :TQ], o[:, TQ:])
    out_ref[...] = (ot_ref[...].T * sza_ref[...].astype(F32)).astype(BF16)


def _dsa_call(qa, qi, wit, sza, ka, vat, ki2, batch, seq):
    assert TQ == LANES and TKS % LANES == 0 and TKC % TKS == 0 and seq % TKC == 0
    assert TKA % LANES == 0 and TKC % TKA == 0 and TKA <= AUG_RADIX * AUG_RADIX
    nq = seq // TQ
    topk = min(TOPK_MAX, seq // 4)
    pos_bits = max(1, int(np.ceil(np.log2(seq))))
    q_spec = lambda w: pl.BlockSpec((TQ, w), lambda b, i: (b * nq + i, 0))
    b_spec = lambda w: pl.BlockSpec((seq, w), lambda b, i: (b, 0))
    n_pair = N_HEADS_A // 2
    return pl.pallas_call(
        functools.partial(_dsa_body, topk=topk, pos_bits=pos_bits),
        out_shape=jax.ShapeDtypeStruct((batch * seq, WIDTH), BF16),
        grid=(batch, nq),
        in_specs=[q_spec(WIDTH), q_spec(WIDTH), pl.BlockSpec((16, TQ), lambda b, i: (0, b * nq + i)), q_spec(WIDTH),
                  b_spec(WIDTH), pl.BlockSpec((seq // LANES, WIDTH, LANES), lambda b, i: (b, 0, 0)), b_spec(LANES)],
        out_specs=q_spec(WIDTH),
        scratch_shapes=[pltpu.VMEM((seq, TQ), F32),
                        pltpu.VMEM((n_pair, LANES, 2 * TQ), BF16), pltpu.VMEM((n_pair, 2 * LANES, 2 * TQ), BF16),
                        pltpu.VMEM((n_pair, 1, 2 * TQ), F32), pltpu.VMEM((n_pair, 1, 2 * TQ), F32),
                        pltpu.VMEM((n_pair, LANES + SUM_ROWS, 2 * TQ), F32),
                        pltpu.VMEM((n_pair, TKA, 2 * TQ), BF16), pltpu.VMEM((n_pair, TKA, 2 * TQ), F32),
                        pltpu.VMEM((n_pair, TKC, 2 * TQ), F32), pltpu.VMEM((WIDTH, TQ), F32)],
        compiler_params=pltpu.CompilerParams(dimension_semantics=("parallel", "arbitrary"),
                                             vmem_limit_bytes=VMEM_LIMIT),
        name="dsa",
    )(qa, qi, wit, sza, ka, vat, ki2)


def _split3(v):
    hi = v.astype(BF16)
    r = v - hi.astype(F32)
    mid = r.astype(BF16)
    lo = (r - mid.astype(F32)).astype(BF16)
    return hi, mid, lo


def _hgrn_body(sqb_ref, logf_ref, kk_ref, ib_ref, ibt_ref, sgb_ref, gon_ref, out_ref, st_ref):
    @pl.when(pl.program_id(1) == 0)
    def _():
        st_ref[...] = jnp.zeros_like(st_ref)

    r_i = lax.broadcasted_iota(I32, (CHUNK, CHUNK), 0)
    c_i = lax.broadcasted_iota(I32, (CHUNK, CHUNK), 1)
    tril = c_i <= r_i
    tri = jnp.where(tril, 1.0, 0.0).astype(BF16)
    n_sub = CHUNK // SUB
    row_c = lax.broadcasted_iota(I32, (CHUNK, LANES), 0)
    row_t = lax.broadcasted_iota(I32, (T_HGRN, LANES), 0)

    def decays(nb, c):
        g = logf_ref[nb, c * CHUNK:(c + 1) * CHUNK, :]
        return sum(_dot(tri, part) for part in _split3(g))

    def operands(nb, c, b):
        rows = slice(c * CHUNK, (c + 1) * CHUNK)
        q = sqb_ref[nb, rows, :].astype(F32)
        kk = kk_ref[nb, rows, :].astype(F32)
        b_last = b[CHUNK - 1:CHUNK, :]
        ref_k = jnp.concatenate(
            [jnp.broadcast_to(b[(j + 1) * SUB - 1:(j + 1) * SUB, :], (SUB, WIDTH)) for j in range(n_sub)], axis=0)
        k_in = kk * jnp.exp(ref_k - b)
        q_out = (q * jnp.exp(b)).astype(BF16)
        k_st = (kk * jnp.exp(b_last - b)).astype(BF16)
        q_in = []
        for j in range(n_sub):
            lo = j * SUB
            qj = q[lo:, :] * jnp.exp(b[lo:, :] - b[lo + SUB - 1:lo + SUB, :])
            if lo:
                qj = jnp.concatenate([jnp.zeros((lo, WIDTH), F32), qj], axis=0)
            q_in.append(qj.astype(BF16))
        return q_in, k_in, q_out, k_st, jnp.exp(b_last)

    def intra_scores(hd, q_in, k_in):
        cols = slice(hd * HEAD_B, (hd + 1) * HEAD_B)
        qs = jnp.concatenate([qj[:, cols] for qj in q_in], axis=1)
        kh = k_in[:, cols]
        ks = jnp.concatenate(
            [jnp.where((row_c >= j * SUB) & (row_c < (j + 1) * SUB), kh, 0.0) for j in range(n_sub)],
            axis=1).astype(BF16)
        return _dot_nt(qs, ks)

    def outputs(nb, c, hd, a_raw, q_out, k_st, decay_last):
        rows = slice(c * CHUNK, (c + 1) * CHUNK)
        cols = slice(hd * HEAD_B, (hd + 1) * HEAD_B)
        a = jnp.where(tril, a_raw, 0.0).astype(BF16)
        st = st_ref[nb, hd]
        o = _dot(a, ib_ref[nb, rows, cols]) + _dot_nt(q_out[:, cols], st.astype(BF16))
        k_pad = jnp.where((row_t >= c * CHUNK) & (row_t < (c + 1) * CHUNK),
                          jnp.concatenate([k_st[:, cols]] * (T_HGRN // CHUNK), axis=0), jnp.zeros((), BF16))
        st_ref[nb, hd] = st * decay_last[:, cols] + _dot(ibt_ref[nb, 0, cols, :], k_pad)
        return o

    def finish(nb, c, hd, o):
        rows = slice(c * CHUNK, (c + 1) * CHUNK)
        cols = slice(hd * HEAD_B, (hd + 1) * HEAD_B)
        ms = jnp.mean(o * o, axis=-1, keepdims=True)
        o = o * lax.rsqrt(ms + EPS) * gon_ref[...]
        out_ref[nb, rows, cols] = (o * sgb_ref[nb, rows, cols].astype(F32)).astype(BF16)

    seqs, heads = range(NB_HGRN), range(N_HEADS_B)
    for c in range(T_HGRN // CHUNK):
        bs = [decays(nb, c) for nb in seqs]
        ops = [operands(nb, c, bs[nb]) for nb in seqs]
        raw = [[intra_scores(hd, ops[nb][0], ops[nb][1]) for hd in heads] for nb in seqs]
        outs = [[outputs(nb, c, hd, raw[nb][hd], ops[nb][2], ops[nb][3], ops[nb][4]) for hd in heads] for nb in seqs]
        for nb in seqs:
            for hd in heads:
                finish(nb, c, hd, outs[nb][hd])


def _hgrn_call(sqb, logf, kk, ib, ibt, sgb, o_norm_b, batch, seq):
    assert T_HGRN == LANES and seq % T_HGRN == 0 and batch % NB_HGRN == 0
    nt = seq // T_HGRN
    as3 = lambda a: a.reshape(batch, seq, WIDTH)
    spec = pl.BlockSpec((NB_HGRN, T_HGRN, WIDTH), lambda b, t: (b, t, 0))
    ibt_spec = pl.BlockSpec((NB_HGRN, 1, WIDTH, T_HGRN), lambda b, t: (b, t, 0, 0))
    out = pl.pallas_call(
        _hgrn_body,
        out_shape=jax.ShapeDtypeStruct((batch, seq, WIDTH), BF16),
        grid=(batch // NB_HGRN, nt),
        in_specs=[spec, spec, spec, spec, ibt_spec, spec, _const_spec((1, HEAD_B))],
        out_specs=spec,
        scratch_shapes=[pltpu.VMEM((NB_HGRN, N_HEADS_B, HEAD_B, HEAD_B), F32)],
        compiler_params=pltpu.CompilerParams(dimension_semantics=("parallel", "arbitrary"),
                                             vmem_limit_bytes=VMEM_LIMIT),
        name="hgrn",
    )(as3(sqb), as3(logf), as3(kk), as3(ib), ibt.reshape(batch, nt, WIDTH, T_HGRN), as3(sgb),
      o_norm_b.astype(F32)[None, :])
    return out.reshape(batch * seq, WIDTH)


def _merge_body(x_ref, oa_ref, ob_ref, qc_ref, szc_ref, gates_ref, kc_ref, vc_ref, wb_ref, wo_ref, out_ref, *, d_model):
    def gated(n, br):
        return gates_ref[:, n * d_model:(n + 1) * d_model].astype(F32) * _dot(br, wb_ref[n])

    heads = [slice(hd * HEAD_DIM_C, (hd + 1) * HEAD_DIM_C) for hd in range(N_HEADS_C)]
    logits = [_dot_nt(qc_ref[:, cols], kc_ref[:, cols]) for cols in heads]
    merged = gated(0, oa_ref[...])
    probs = [jnp.exp(s - jnp.max(s, axis=-1, keepdims=True)) for s in logits]
    oc = [_dot(p.astype(BF16), vc_ref[:, cols]) * (1.0 / jnp.sum(p, axis=-1, keepdims=True))
          for p, cols in zip(probs, heads)]
    merged = merged + gated(1, ob_ref[...])
    oc = (jnp.concatenate(oc, axis=1) * szc_ref[...].astype(F32)).astype(BF16)
    merged = merged + gated(2, oc)
    out_ref[...] = x_ref[...] + _dot(merged.astype(BF16), wo_ref[...])


def _merge_call(x2, oa, ob, qc, szc, gates, kc, vc, w_branch, w_out, batch, seq, mem_len):
    rows, d_model = x2.shape
    tm = TM_MERGE
    nt = seq // tm
    row_spec = lambda w: pl.BlockSpec((tm, w), lambda b, t: (b * nt + t, 0))
    mem_spec = pl.BlockSpec((mem_len, WIDTH), lambda b, t: (b, 0))
    return pl.pallas_call(
        functools.partial(_merge_body, d_model=d_model),
        out_shape=jax.ShapeDtypeStruct((rows, d_model), F32),
        grid=(batch, nt),
        in_specs=[row_spec(d_model), row_spec(WIDTH), row_spec(WIDTH), row_spec(WIDTH), row_spec(WIDTH),
                  row_spec(N_BRANCH * d_model), mem_spec, mem_spec,
                  _const_spec((N_BRANCH, WIDTH, d_model)), _const_spec((d_model, d_model))],
        out_specs=row_spec(d_model),
        compiler_params=pltpu.CompilerParams(dimension_semantics=("parallel", "parallel"),
                                             vmem_limit_bytes=VMEM_LIMIT),
        name="merge",
    )(x2, oa, ob, qc, szc, gates, kc, vc, w_branch.astype(BF16), w_out.astype(BF16))


def _layer(x, mem, norm_in, norm_mem, w_in, q_norm_a, k_norm_a, lower_bounds, o_norm_b, w_mem_kv, q_norm_c, k_norm_c,
           w_branch, w_out):
    batch, seq, d_model = x.shape
    mem_len = mem.shape[1]
    x2 = x.reshape(batch * seq, d_model)
    (qa, ka, vat, sza, qi, ki2, wit, sqb, logf, kk, ib, ibt, sgb, qc, szc, gates) = _proj_call(
        x2, norm_in, w_in, lower_bounds, q_norm_a, k_norm_a, q_norm_c)
    kc, vc = _memkv_call(mem.reshape(batch * mem_len, d_model), norm_mem, w_mem_kv, k_norm_c, mem_len)
    oa = _dsa_call(qa, qi, wit, sza, ka, vat, ki2, batch, seq)
    ob = _hgrn_call(sqb, logf, kk, ib, ibt, sgb, o_norm_b, batch, seq)
    out = _merge_call(x2, oa, ob, qc, szc, gates, kc, vc, w_branch, w_out, batch, seq, mem_len)
    return out.reshape(batch, seq, d_model)


def kernel(x, mem, norm_in, norm_mem, w_in, q_norm_a, k_norm_a, lower_bounds, o_norm_b, w_mem_kv, q_norm_c, k_norm_c,
           w_branch, w_out):
    assert norm_in.shape[0] == 1, "single-layer block"
    return _layer(x, mem, norm_in[0], norm_mem[0], w_in[0], q_norm_a[0], k_norm_a[0], lower_bounds, o_norm_b[0],
                  w_mem_kv[0], q_norm_c[0], k_norm_c[0], w_branch[0], w_out[0])
```

```python
import functools

import jax
import jax.numpy as jnp
import numpy as np
from jax import lax
from jax.experimental import pallas as pl
from jax.experimental.pallas import tpu as pltpu

F32 = jnp.float32
BF16 = jnp.bfloat16
I32 = jnp.int32

N_HEADS_A = 8
HEAD_DIM_A = 64
N_IDX_HEADS = 8
IDX_DIM = 64
TOPK_MAX = 256
N_HEADS_B = 4
HEAD_B = 128
N_HEADS_C = 4
HEAD_DIM_C = 128
N_BRANCH = 3
EPS = 1e-6
WIDTH = 512
IDX_SCALE = (IDX_DIM ** -0.5) * (N_IDX_HEADS ** -0.5)

LANES = 128
VMEM_LIMIT = 56 * 1024 * 1024

TM_PROJ = 256
TQ = 128
TKS = 256
TKA = 256
AUG_RADIX = 256
TKC = 512
SUM_ROWS = 16
CHUNK = 64
SUB = 16
T_HGRN = 128
NB_HGRN = 4
TM_MERGE = 512

NEG_BIG = -1e30
LOG2E = 1.4426950408889634
INT_MIN = -2147483648
KEY_LOWEST = -2139095040

NT = (((1,), (1,)), ((), ()))


def _dot(a, b):
    return jnp.dot(a, b, preferred_element_type=F32)


def _dot_nt(a, b):
    return lax.dot_general(a, b, NT, preferred_element_type=F32)


def _group_meansq(v, group):
    sq = v * v
    lane = lax.broadcasted_iota(I32, (v.shape[0], LANES), 1) // group
    blocks = []
    for b in range(v.shape[1] // LANES):
        blk = sq[:, b * LANES:(b + 1) * LANES]
        ms = jnp.zeros(blk.shape, F32)
        for g in range(LANES // group):
            part = blk if group == LANES else jnp.where(lane == g, blk, 0.0)
            total = jnp.sum(part, axis=-1, keepdims=True)
            ms = total + ms if group == LANES else jnp.where(lane == g, total, ms)
        blocks.append(ms)
    return jnp.concatenate(blocks, axis=1) * (1.0 / group)


_C_QA, _C_KA, _C_ZA, _C_QB, _C_FB, _C_IB, _C_GB, _C_QC, _C_ZC = [WIDTH * i for i in range(9)]
_C_GATES = 9 * WIDTH
_C_QI = _C_GATES + 3 * 1024
_C_KI = _C_QI + WIDTH
_C_END = _C_KI + LANES


def _proj_body(x_ref, nin_ref, w_ref, wvt_ref, wwt_ref, lbp_ref, gqa_ref, gka_ref, gqc_ref,
               qa_ref, ka_ref, vat_ref, sza_ref, qi_ref, ki2_ref, wit_ref,
               sqb_ref, logf_ref, kk_ref, ib_ref, ibt_ref, sgb_ref, qc_ref, szc_ref, gates_ref, *, d_model):
    x = x_ref[...]
    ms = jnp.mean(x * x, axis=-1, keepdims=True)
    h = (x * lax.rsqrt(ms + EPS)) * nin_ref[...]
    hb = h.astype(BF16)

    def proj(c0, width):
        return lambda: _dot(hb, w_ref[:, c0:c0 + width])

    def head_norm(group, gain_ref, scale, out_ref):
        def epilogue(v):
            v = v * lax.rsqrt(_group_meansq(v, group) + EPS) * gain_ref[...]
            out_ref[...] = (v * scale).astype(BF16)
        return epilogue

    def store(out_ref, fn=lambda v: v):
        def epilogue(v):
            out_ref[...] = fn(v).astype(out_ref.dtype)
        return epilogue

    def store_time_blocks(out_ref):
        def epilogue(v):
            v = v.astype(BF16)
            for t in range(out_ref.shape[0]):
                out_ref[t] = v[:, t * LANES:(t + 1) * LANES]
        return epilogue

    def store_both_layouts(out_ref, out_t_ref):
        def epilogue(v):
            out_ref[...] = v.astype(BF16)
            store_time_blocks(out_t_ref)(v.T)
        return epilogue

    def forget(v):
        lbp = lbp_ref[...]
        lbe = jnp.exp(lbp - jnp.max(lbp, axis=0, keepdims=True))
        lb = lbe[0:1, :] / jnp.sum(lbe, axis=0, keepdims=True)
        f = lb + (1.0 - lb) * jax.nn.sigmoid(v)
        logf_ref[...] = jnp.log(f)
        kk_ref[...] = (1.0 - f).astype(BF16)

    def gate(n):
        def epilogue(v):
            gates_ref[:, n * d_model:(n + 1) * d_model] = jax.nn.sigmoid(v).astype(BF16)
        return epilogue

    stages = [
        (proj(_C_QA, WIDTH), head_norm(HEAD_DIM_A, gqa_ref, HEAD_DIM_A ** -0.5 * LOG2E, qa_ref)),
        (proj(_C_KA, WIDTH), head_norm(HEAD_DIM_A, gka_ref, 1.0, ka_ref)),
        (lambda: _dot_nt(wvt_ref[...], hb), store_time_blocks(vat_ref)),
        (proj(_C_ZA, WIDTH), store(sza_ref, jax.nn.silu)),
        (proj(_C_QI, WIDTH), store(qi_ref)),
        (proj(_C_KI, LANES), store(ki2_ref)),
        (lambda: _dot_nt(wwt_ref[...], hb), store(wit_ref)),
        (proj(_C_QB, WIDTH), store(sqb_ref, jax.nn.silu)),
        (proj(_C_FB, WIDTH), forget),
        (proj(_C_IB, WIDTH), store_both_layouts(ib_ref, ibt_ref)),
        (proj(_C_GB, WIDTH), store(sgb_ref, jax.nn.silu)),
        (proj(_C_QC, WIDTH), head_norm(HEAD_DIM_C, gqc_ref, HEAD_DIM_C ** -0.5, qc_ref)),
        (proj(_C_ZC, WIDTH), store(szc_ref, jax.nn.silu)),
    ] + [(proj(_C_GATES + n * d_model, d_model), gate(n)) for n in range(N_BRANCH)]
    pending = stages[0][0]()
    for k, (_, epilogue) in enumerate(stages):
        ahead = stages[k + 1][0]() if k + 1 < len(stages) else None
        epilogue(pending)
        pending = ahead


def _const_spec(shape):
    nd = len(shape)
    return pl.BlockSpec(shape, lambda *_: (0,) * nd, pipeline_mode=pl.Buffered(1))


def _proj_call(x2, norm_in, w_in, lower_bounds, q_norm_a, k_norm_a, q_norm_c):
    rows, d_model = x2.shape
    tm = TM_PROJ
    assert rows % tm == 0 and d_model == 1024
    offs = np.cumsum([0, 512, 512, 512, 512, 512, 64, 8, 512, 512, 512, 512, 512, 512, 3 * d_model])
    (o_qa, o_ka, o_va, o_za, o_qi, o_ki, o_wi, o_qb, o_fb, o_ib, o_gb, o_qc, o_zc, o_g, o_end) = [int(o) for o in offs]
    assert o_end == w_in.shape[1]
    wb = w_in.astype(BF16)
    col = lambda a, b: wb[:, a:b]
    w_main = jnp.concatenate([
        col(o_qa, o_ka), col(o_ka, o_va), col(o_za, o_qi), col(o_qb, o_fb), col(o_fb, o_ib), col(o_ib, o_gb),
        col(o_gb, o_qc), col(o_qc, o_zc), col(o_zc, o_g), col(o_g, o_end), col(o_qi, o_ki),
        col(o_ki, o_wi), col(o_ki, o_wi)], axis=1)
    assert w_main.shape[1] == _C_END
    wvt = col(o_va, o_za).T
    wwt = jnp.concatenate([col(o_wi, o_qb).T, jnp.zeros((8, d_model), BF16)], axis=0)
    tile = lambda g, reps: jnp.tile(g.astype(F32), reps)[None, :]
    n_slots = lower_bounds.shape[0]

    row_spec = lambda w: pl.BlockSpec((tm, w), lambda i: (i, 0))
    t_spec = pl.BlockSpec((tm // LANES, WIDTH, LANES), lambda i: (i, 0, 0))
    out_shape = [
        jax.ShapeDtypeStruct((rows, WIDTH), BF16),
        jax.ShapeDtypeStruct((rows, WIDTH), BF16),
        jax.ShapeDtypeStruct((rows // LANES, WIDTH, LANES), BF16),
        jax.ShapeDtypeStruct((rows, WIDTH), BF16),
        jax.ShapeDtypeStruct((rows, WIDTH), BF16),
        jax.ShapeDtypeStruct((rows, LANES), BF16),
        jax.ShapeDtypeStruct((16, rows), F32),
        jax.ShapeDtypeStruct((rows, WIDTH), BF16),
        jax.ShapeDtypeStruct((rows, WIDTH), F32),
        jax.ShapeDtypeStruct((rows, WIDTH), BF16),
        jax.ShapeDtypeStruct((rows, WIDTH), BF16),
        jax.ShapeDtypeStruct((rows // LANES, WIDTH, LANES), BF16),
        jax.ShapeDtypeStruct((rows, WIDTH), BF16),
        jax.ShapeDtypeStruct((rows, WIDTH), BF16),
        jax.ShapeDtypeStruct((rows, WIDTH), BF16),
        jax.ShapeDtypeStruct((rows, N_BRANCH * d_model), BF16),
    ]
    out_specs = [row_spec(WIDTH), row_spec(WIDTH), t_spec, row_spec(WIDTH), row_spec(WIDTH), row_spec(LANES),
                 pl.BlockSpec((16, tm), lambda i: (0, i)),
                 row_spec(WIDTH), row_spec(WIDTH), row_spec(WIDTH), row_spec(WIDTH), t_spec, row_spec(WIDTH),
                 row_spec(WIDTH), row_spec(WIDTH), row_spec(N_BRANCH * d_model)]
    in_specs = [row_spec(d_model), _const_spec((1, d_model)), _const_spec(w_main.shape), _const_spec(wvt.shape),
                _const_spec(wwt.shape), _const_spec((n_slots, WIDTH)), _const_spec((1, WIDTH)),
                _const_spec((1, WIDTH)), _const_spec((1, WIDTH))]
    return pl.pallas_call(
        functools.partial(_proj_body, d_model=d_model),
        out_shape=out_shape, grid=(rows // tm,), in_specs=in_specs, out_specs=out_specs,
        compiler_params=pltpu.CompilerParams(dimension_semantics=("parallel",), vmem_limit_bytes=VMEM_LIMIT),
        name="proj",
    )(x2, norm_in.astype(F32)[None, :], w_main, wvt, wwt, lower_bounds.astype(F32),
      tile(q_norm_a, N_HEADS_A), tile(k_norm_a, N_HEADS_A), tile(q_norm_c, N_HEADS_C))


def _memkv_body(m_ref, nm_ref, w_ref, gk_ref, kc_ref, vc_ref):
    x = m_ref[...]
    ms = jnp.mean(x * x, axis=-1, keepdims=True)
    hb = ((x * lax.rsqrt(ms + EPS)) * nm_ref[...]).astype(BF16)
    kc = _dot(hb, w_ref[:, 0:WIDTH])
    kc = kc * lax.rsqrt(_group_meansq(kc, HEAD_DIM_C) + EPS) * gk_ref[...]
    kc_ref[...] = kc.astype(BF16)
    vc_ref[...] = _dot(hb, w_ref[:, WIDTH:2 * WIDTH]).astype(BF16)


def _memkv_call(mem2, norm_mem, w_mem_kv, k_norm_c, tm):
    rows, d_model = mem2.shape
    row_spec = lambda w: pl.BlockSpec((tm, w), lambda i: (i, 0))
    return pl.pallas_call(
        _memkv_body,
        out_shape=[jax.ShapeDtypeStruct((rows, WIDTH), BF16)] * 2,
        grid=(rows // tm,),
        in_specs=[row_spec(d_model), _const_spec((1, d_model)), _const_spec((d_model, 2 * WIDTH)),
                  _const_spec((1, WIDTH))],
        out_specs=[row_spec(WIDTH), row_spec(WIDTH)],
        compiler_params=pltpu.CompilerParams(dimension_semantics=("parallel",), vmem_limit_bytes=VMEM_LIMIT),
        name="memkv",
    )(mem2, norm_mem.astype(F32)[None, :], w_mem_kv.astype(BF16),
      jnp.tile(k_norm_c.astype(F32), N_HEADS_C)[None, :])


def _key_to_f32(k):
    return pltpu.bitcast(jnp.where(k < 0, k ^ jnp.int32(0x7FFFFFFF), k), F32)


def _slope_log2(h):
    return (2.0 ** (-8.0 * (h + 1) / N_HEADS_A)) * LOG2E


N_SLOPE_PARTS = 3


def _slope_rows(h, row):
    rest = np.float32(_slope_log2(h))
    out = jnp.zeros(row.shape, F32)
    for r in range(N_SLOPE_PARTS):
        part = np.float32(np.asarray(rest, dtype=jnp.bfloat16))
        out = jnp.where(row == r, float(part), out)
        out = jnp.where(row == N_SLOPE_PARTS + r, float(part) * AUG_RADIX, out)
        rest = np.float32(rest - part)
    return out


def _dsa_body(qa_ref, qi_ref, wit_ref, sza_ref, ka_ref, vat_ref, ki2_ref, out_ref,
              sc_ref, qir_ref, qar_ref, m_ref, alpha_ref, acc_ref, pbuf_ref, sbuf_ref, rbuf_ref, ot_ref,
              *, topk, pos_bits):
    i = pl.program_id(1)
    q0 = i * TQ
    n_att = (q0 + TQ + TKA - 1) // TKA
    n_cnt = (q0 + TQ + TKC - 1) // TKC

    row = lax.broadcasted_iota(I32, (LANES, TQ), 0)
    row_lo = row < HEAD_DIM_A
    pairs = range(N_HEADS_A // 2)

    def pair_operand(ref, p):
        t = ref[:, p * LANES:(p + 1) * LANES].astype(F32).T
        return jnp.concatenate([jnp.where(row_lo, t, 0.0), jnp.where(row_lo, 0.0, t)], axis=1).astype(BF16)

    for p in pairs:
        qir_ref[p] = pair_operand(qi_ref, p)

    def index_dots(jt):
        kt = ki2_ref[pl.ds(pl.multiple_of(jt * TKC, TKC), TKC), :]
        return [_dot(kt, qir_ref[p]) for p in pairs]

    first = index_dots(0)
    for p in pairs:
        qar_ref[p, 0:LANES, :] = pair_operand(qa_ref, p)
        qar_ref[p, LANES:2 * LANES, :] = jnp.concatenate(
            [_slope_rows(2 * p, row), _slope_rows(2 * p + 1, row)], axis=1).astype(BF16)
    m_ref[...] = jnp.full(m_ref.shape, NEG_BIG, F32)
    acc_ref[...] = jnp.zeros(acc_ref.shape, F32)
    alpha_ref[...] = jnp.ones(alpha_ref.shape, F32)
    pbuf_ref[...] = jnp.zeros(pbuf_ref.shape, BF16)
    for p in pairs:
        rbuf_ref[p] = first[p]

    row_s = lax.broadcasted_iota(I32, (TKS, TQ), 0)
    tpos = q0 + lax.broadcasted_iota(I32, (TKS, TQ), 1)

    def score_tile(j, carry, look_ahead=True):
        ahead = index_dots(j + 1) if look_ahead else None
        for r in range(TKC // TKS):
            rows = slice(r * TKS, (r + 1) * TKS)
            r0 = pl.multiple_of(j * TKC + r * TKS, TKS)
            acc = jnp.zeros((TKS, TQ), F32)
            for p in pairs:
                rel = jnp.maximum(rbuf_ref[p, rows, :], 0.0)
                acc = acc + rel[:, :TQ] * wit_ref[2 * p:2 * p + 1, :] + rel[:, TQ:] * wit_ref[2 * p + 1:2 * p + 2, :]
            sc = acc * IDX_SCALE
            sc = jnp.where(sc == 0.0, 0.0, sc)
            sc_ref[pl.ds(r0, TKS), :] = jnp.where(r0 + row_s <= tpos, sc, -jnp.inf)
        if look_ahead:
            for p in pairs:
                rbuf_ref[p] = ahead[p]
        return carry

    lax.fori_loop(0, n_cnt - 1, score_tile, 0)
    score_tile(n_cnt - 1, 0, look_ahead=False)

    row_c = lax.broadcasted_iota(I32, (TKC, TQ), 0)

    def count(pred):
        def tile(j, acc):
            r0 = pl.multiple_of(j * TKC, TKC)
            c = jnp.where(pred(sc_ref[pl.ds(r0, TKC), :], r0), jnp.int32(1), jnp.int32(0))
            return acc + jnp.sum(c.reshape(TKC // 8, 8, TQ), axis=0)
        acc = lax.fori_loop(0, n_cnt, tile, jnp.zeros((8, TQ), I32))
        return jnp.sum(acc, axis=0, keepdims=True)

    def bisect(it, p):
        cand = p ^ lax.shift_left(jnp.int32(1), 31 - it)
        cand_f = _key_to_f32(cand)
        n = count(lambda s, r0: s >= cand_f)
        return jnp.where(n >= topk, cand, p)

    kth = lax.fori_loop(0, 32, bisect, jnp.full((1, TQ), INT_MIN, I32))
    kth = jnp.maximum(kth, jnp.int32(KEY_LOWEST))
    thr = _key_to_f32(kth)
    nxt = _key_to_f32(kth + 1)
    n_gt = count(lambda s, r0: s >= nxt)
    n_ge = count(lambda s, r0: s >= thr)
    need = topk - n_gt
    has_tie = n_ge > topk
    any_tie = jnp.max(jnp.where(has_tie, 1, 0)) > 0

    lane_a = lax.broadcasted_iota(I32, (TKA, LANES), 1)
    row_a = lax.broadcasted_iota(I32, (TKA, LANES), 0)
    aug = jnp.where(lane_a < N_SLOPE_PARTS, row_a % AUG_RADIX,
                    jnp.where(lane_a < 2 * N_SLOPE_PARTS, row_a // AUG_RADIX, 0)).astype(F32).astype(BF16)
    row_q = lax.broadcasted_iota(I32, (TKA, TQ), 0)
    half2 = lax.broadcasted_iota(I32, (1, 2 * TQ), 1) < TQ

    ones_rows = jnp.ones((SUM_ROWS, TKA), BF16)

    def value_update(jt):
        for p in pairs:
            vt = jnp.concatenate([vat_ref[jt * (TKA // LANES) + t, p * LANES:(p + 1) * LANES, :]
                                  for t in range(TKA // LANES)], axis=1)
            vt = jnp.concatenate([vt, ones_rows], axis=0)
            acc_ref[p] = alpha_ref[p] * acc_ref[p] + _dot(vt, pbuf_ref[p])

    def score_dots(jt):
        r0 = pl.multiple_of(jt * TKA, TKA)
        return [_dot(jnp.concatenate([ka_ref[pl.ds(r0, TKA), p * LANES:(p + 1) * LANES], aug], axis=1), qar_ref[p])
                for p in pairs]

    for p, s in zip(pairs, score_dots(0)):
        sbuf_ref[p] = s

    def tie_cut():
        def step(it, jp):
            cand = jp | lax.shift_left(jnp.int32(1), pos_bits - 1 - it)
            n = count(lambda s, r0: (s >= thr) & (s < nxt) & (r0 + row_c < cand))
            return jnp.where(n < need, cand, jp)
        return lax.fori_loop(0, pos_bits, step, jnp.zeros((1, TQ), I32))

    cut = lax.cond(any_tie, tie_cut, lambda: jnp.zeros((1, TQ), I32))
    cut = jnp.where(has_tie, cut, jnp.int32(2 ** 30))

    def att_tile(j, carry, look_ahead=True):
        r0 = pl.multiple_of(j * TKA, TKA)
        s_idx = sc_ref[pl.ds(r0, TKA), :]
        sel = (s_idx >= thr) & ((s_idx >= nxt) | (r0 + row_q <= cut))
        bias = jnp.where(sel, 0.0, NEG_BIG)
        bias2 = jnp.concatenate([bias, bias], axis=1)
        base = (r0 - q0).astype(F32)
        value_update(jnp.maximum(j - 1, 0))
        ahead = score_dots(j + 1) if look_ahead else None
        for p in pairs:
            shift = jnp.where(half2, _slope_log2(2 * p), _slope_log2(2 * p + 1)) * base
            s = sbuf_ref[p] + bias2
            m_old = m_ref[p]
            m_new = jnp.maximum(m_old, jnp.max(s, axis=0, keepdims=True) + shift)
            alpha = jnp.exp2(m_old - m_new)
            m_ref[p] = m_new
            alpha_ref[p] = alpha
            pbuf_ref[p] = jnp.exp2(s - (m_new - shift)).astype(BF16)
        if look_ahead:
            for p in pairs:
                sbuf_ref[p] = ahead[p]
        return carry

    lax.fori_loop(0, n_att - 1, att_tile, 0)
    att_tile(n_att - 1, 0, look_ahead=False)
    value_update(n_att - 1)

    row_lo = lax.broadcasted_iota(I32, (LANES, TQ), 0) < HEAD_DIM_A
    for p in range(N_HEADS_A // 2):
        o = acc_ref[p, 0:LANES, :] * (1.0 / acc_ref[p, LANES:LANES + 1, :])
        ot_ref[p * LANES:(p + 1) * LANES, :] = jnp.where(row_lo, o[:, :TQ], o[:, TQ:])
    out_ref[...] = (ot_ref[...].T * sza_ref[...].astype(F32)).astype(BF16)


def _dsa_call(qa, qi, wit, sza, ka, vat, ki2, batch, seq):
    assert TQ == LANES and TKS % LANES == 0 and TKC % TKS == 0 and seq % TKC == 0
    assert TKA % LANES == 0 and TKC % TKA == 0 and TKA <= AUG_RADIX * AUG_RADIX
    nq = seq // TQ
    topk = min(TOPK_MAX, seq // 4)
    pos_bits = max(1, int(np.ceil(np.log2(seq))))
    q_spec = lambda w: pl.BlockSpec((TQ, w), lambda b, i: (b * nq + i, 0))
    b_spec = lambda w: pl.BlockSpec((seq, w), lambda b, i: (b, 0))
    n_pair = N_HEADS_A // 2
    return pl.pallas_call(
        functools.partial(_dsa_body, topk=topk, pos_bits=pos_bits),
        out_shape=jax.ShapeDtypeStruct((batch * seq, WIDTH), BF16),
        grid=(batch, nq),
        in_specs=[q_spec(WIDTH), q_spec(WIDTH), pl.BlockSpec((16, TQ), lambda b, i: (0, b * nq + i)), q_spec(WIDTH),
                  b_spec(WIDTH), pl.BlockSpec((seq // LANES, WIDTH, LANES), lambda b, i: (b, 0, 0)), b_spec(LANES)],
        out_specs=q_spec(WIDTH),
        scratch_shapes=[pltpu.VMEM((seq, TQ), F32),
                        pltpu.VMEM((n_pair, LANES, 2 * TQ), BF16), pltpu.VMEM((n_pair, 2 * LANES, 2 * TQ), BF16),
                        pltpu.VMEM((n_pair, 1, 2 * TQ), F32), pltpu.VMEM((n_pair, 1, 2 * TQ), F32),
                        pltpu.VMEM((n_pair, LANES + SUM_ROWS, 2 * TQ), F32),
                        pltpu.VMEM((n_pair, TKA, 2 * TQ), BF16), pltpu.VMEM((n_pair, TKA, 2 * TQ), F32),
                        pltpu.VMEM((n_pair, TKC, 2 * TQ), F32), pltpu.VMEM((WIDTH, TQ), F32)],
        compiler_params=pltpu.CompilerParams(dimension_semantics=("parallel", "arbitrary"),
                                             vmem_limit_bytes=VMEM_LIMIT),
        name="dsa",
    )(qa, qi, wit, sza, ka, vat, ki2)


def _split3(v):
    hi = v.astype(BF16)
    r = v - hi.astype(F32)
    mid = r.astype(BF16)
    lo = (r - mid.astype(F32)).astype(BF16)
    return hi, mid, lo


def _hgrn_body(sqb_ref, logf_ref, kk_ref, ib_ref, ibt_ref, sgb_ref, gon_ref, out_ref, st_ref):
    @pl.when(pl.program_id(1) == 0)
    def _():
        st_ref[...] = jnp.zeros_like(st_ref)

    r_i = lax.broadcasted_iota(I32, (CHUNK, CHUNK), 0)
    c_i = lax.broadcasted_iota(I32, (CHUNK, CHUNK), 1)
    tril = c_i <= r_i
    tri = jnp.where(tril, 1.0, 0.0).astype(BF16)
    n_sub = CHUNK // SUB
    row_c = lax.broadcasted_iota(I32, (CHUNK, LANES), 0)
    row_t = lax.broadcasted_iota(I32, (T_HGRN, LANES), 0)

    def decays(nb, c):
        g = logf_ref[nb, c * CHUNK:(c + 1) * CHUNK, :]
        return sum(_dot(tri, part) for part in _split3(g))

    def operands(nb, c, b):
        rows = slice(c * CHUNK, (c + 1) * CHUNK)
        q = sqb_ref[nb, rows, :].astype(F32)
        kk = kk_ref[nb, rows, :].astype(F32)
        b_last = b[CHUNK - 1:CHUNK, :]
        ref_k = jnp.concatenate(
            [jnp.broadcast_to(b[(j + 1) * SUB - 1:(j + 1) * SUB, :], (SUB, WIDTH)) for j in range(n_sub)], axis=0)
        k_in = kk * jnp.exp(ref_k - b)
        q_out = (q * jnp.exp(b)).astype(BF16)
        k_st = (kk * jnp.exp(b_last - b)).astype(BF16)
        q_in = []
        for j in range(n_sub):
            lo = j * SUB
            qj = q[lo:, :] * jnp.exp(b[lo:, :] - b[lo + SUB - 1:lo + SUB, :])
            if lo:
                qj = jnp.concatenate([jnp.zeros((lo, WIDTH), F32), qj], axis=0)
            q_in.append(qj.astype(BF16))
        return q_in, k_in, q_out, k_st, jnp.exp(b_last)

    def intra_scores(hd, q_in, k_in):
        cols = slice(hd * HEAD_B, (hd + 1) * HEAD_B)
        qs = jnp.concatenate([qj[:, cols] for qj in q_in], axis=1)
        kh = k_in[:, cols]
        ks = jnp.concatenate(
            [jnp.where((row_c >= j * SUB) & (row_c < (j + 1) * SUB), kh, 0.0) for j in range(n_sub)],
            axis=1).astype(BF16)
        return _dot_nt(qs, ks)

    def outputs(nb, c, hd, a_raw, q_out, k_st, decay_last):
        rows = slice(c * CHUNK, (c + 1) * CHUNK)
        cols = slice(hd * HEAD_B, (hd + 1) * HEAD_B)
        a = jnp.where(tril, a_raw, 0.0).astype(BF16)
        st = st_ref[nb, hd]
        o = _dot(a, ib_ref[nb, rows, cols]) + _dot_nt(q_out[:, cols], st.astype(BF16))
        k_pad = jnp.where((row_t >= c * CHUNK) & (row_t < (c + 1) * CHUNK),
                          jnp.concatenate([k_st[:, cols]] * (T_HGRN // CHUNK), axis=0), jnp.zeros((), BF16))
        st_ref[nb, hd] = st * decay_last[:, cols] + _dot(ibt_ref[nb, 0, cols, :], k_pad)
        return o

    def finish(nb, c, hd, o):
        rows = slice(c * CHUNK, (c + 1) * CHUNK)
        cols = slice(hd * HEAD_B, (hd + 1) * HEAD_B)
        ms = jnp.mean(o * o, axis=-1, keepdims=True)
        o = o * lax.rsqrt(ms + EPS) * gon_ref[...]
        out_ref[nb, rows, cols] = (o * sgb_ref[nb, rows, cols].astype(F32)).astype(BF16)

    seqs, heads = range(NB_HGRN), range(N_HEADS_B)
    for c in range(T_HGRN // CHUNK):
        bs = [decays(nb, c) for nb in seqs]
        ops = [operands(nb, c, bs[nb]) for nb in seqs]
        raw = [[intra_scores(hd, ops[nb][0], ops[nb][1]) for hd in heads] for nb in seqs]
        outs = [[outputs(nb, c, hd, raw[nb][hd], ops[nb][2], ops[nb][3], ops[nb][4]) for hd in heads] for nb in seqs]
        for nb in seqs:
            for hd in heads:
                finish(nb, c, hd, outs[nb][hd])


def _hgrn_call(sqb, logf, kk, ib, ibt, sgb, o_norm_b, batch, seq):
    assert T_HGRN == LANES and seq % T_HGRN == 0 and batch % NB_HGRN == 0
    nt = seq // T_HGRN
    as3 = lambda a: a.reshape(batch, seq, WIDTH)
    spec = pl.BlockSpec((NB_HGRN, T_HGRN, WIDTH), lambda b, t: (b, t, 0))
    ibt_spec = pl.BlockSpec((NB_HGRN, 1, WIDTH, T_HGRN), lambda b, t: (b, t, 0, 0))
    out = pl.pallas_call(
        _hgrn_body,
        out_shape=jax.ShapeDtypeStruct((batch, seq, WIDTH), BF16),
        grid=(batch // NB_HGRN, nt),
        in_specs=[spec, spec, spec, spec, ibt_spec, spec, _const_spec((1, HEAD_B))],
        out_specs=spec,
        scratch_shapes=[pltpu.VMEM((NB_HGRN, N_HEADS_B, HEAD_B, HEAD_B), F32)],
        compiler_params=pltpu.CompilerParams(dimension_semantics=("parallel", "arbitrary"),
                                             vmem_limit_bytes=VMEM_LIMIT),
        name="hgrn",
    )(as3(sqb), as3(logf), as3(kk), as3(ib), ibt.reshape(batch, nt, WIDTH, T_HGRN), as3(sgb),
      o_norm_b.astype(F32)[None, :])
    return out.reshape(batch * seq, WIDTH)


def _merge_body(x_ref, oa_ref, ob_ref, qc_ref, szc_ref, gates_ref, kc_ref, vc_ref, wb_ref, wo_ref, out_ref, *, d_model):
    def gated(n, br):
        return gates_ref[:, n * d_model:(n + 1) * d_model].astype(F32) * _dot(br, wb_ref[n])

    heads = [slice(hd * HEAD_DIM_C, (hd + 1) * HEAD_DIM_C) for hd in range(N_HEADS_C)]
    logits = [_dot_nt(qc_ref[:, cols], kc_ref[:, cols]) for cols in heads]
    merged = gated(0, oa_ref[...])
    probs = [jnp.exp(s - jnp.max(s, axis=-1, keepdims=True)) for s in logits]
    oc = [_dot(p.astype(BF16), vc_ref[:, cols]) * (1.0 / jnp.sum(p, axis=-1, keepdims=True))
          for p, cols in zip(probs, heads)]
    merged = merged + gated(1, ob_ref[...])
    oc = (jnp.concatenate(oc, axis=1) * szc_ref[...].astype(F32)).astype(BF16)
    merged = merged + gated(2, oc)
    out_ref[...] = x_ref[...] + _dot(merged.astype(BF16), wo_ref[...])


def _merge_call(x2, oa, ob, qc, szc, gates, kc, vc, w_branch, w_out, batch, seq, mem_len):
    rows, d_model = x2.shape
    tm = TM_MERGE
    nt = seq // tm
    row_spec = lambda w: pl.BlockSpec((tm, w), lambda b, t: (b * nt + t, 0))
    mem_spec = pl.BlockSpec((mem_len, WIDTH), lambda b, t: (b, 0))
    return pl.pallas_call(
        functools.partial(_merge_body, d_model=d_model),
        out_shape=jax.ShapeDtypeStruct((rows, d_model), F32),
        grid=(batch, nt),
        in_specs=[row_spec(d_model), row_spec(WIDTH), row_spec(WIDTH), row_spec(WIDTH), row_spec(WIDTH),
                  row_spec(N_BRANCH * d_model), mem_spec, mem_spec,
                  _const_spec((N_BRANCH, WIDTH, d_model)), _const_spec((d_model, d_model))],
        out_specs=row_spec(d_model),
        compiler_params=pltpu.CompilerParams(dimension_semantics=("parallel", "parallel"),
                                             vmem_limit_bytes=VMEM_LIMIT),
        name="merge",
    )(x2, oa, ob, qc, szc, gates, kc, vc, w_branch.astype(BF16), w_out.astype(BF16))


def _layer(x, mem, norm_in, norm_mem, w_in, q_norm_a, k_norm_a, lower_bounds, o_norm_b, w_mem_kv, q_norm_c, k_norm_c,
           w_branch, w_out):
    batch, seq, d_model = x.shape
    mem_len = mem.shape[1]
    x2 = x.reshape(batch * seq, d_model)
    (qa, ka, vat, sza, qi, ki2, wit, sqb, logf, kk, ib, ibt, sgb, qc, szc, gates) = _proj_call(
        x2, norm_in, w_in, lower_bounds, q_norm_a, k_norm_a, q_norm_c)
    kc, vc = _memkv_call(mem.reshape(batch * mem_len, d_model), norm_mem, w_mem_kv, k_norm_c, mem_len)
    oa = _dsa_call(qa, qi, wit, sza, ka, vat, ki2, batch, seq)
    ob = _hgrn_call(sqb, logf, kk, ib, ibt, sgb, o_norm_b, batch, seq)
    out = _merge_call(x2, oa, ob, qc, szc, gates, kc, vc, w_branch, w_out, batch, seq, mem_len)
    return out.reshape(batch, seq, d_model)


def kernel(x, mem, norm_in, norm_mem, w_in, q_norm_a, k_norm_a, lower_bounds, o_norm_b, w_mem_kv, q_norm_c, k_norm_c,
           w_branch, w_out):
    assert norm_in.shape[0] == 1, "single-layer block"
    return _layer(x, mem, norm_in[0], norm_mem[0], w_in[0], q_norm_a[0], k_norm_a[0], lower_bounds, o_norm_b[0],
                  w_mem_kv[0], q_norm_c[0], k_norm_c[0], w_branch[0], w_out[0])
```

```python
import functools

import jax
import jax.numpy as jnp
import numpy as np
from jax import lax
from jax.experimental import pallas as pl
from jax.experimental.pallas import tpu as pltpu

F32 = jnp.float32
BF16 = jnp.bfloat16
I32 = jnp.int32

D_MODEL = 1024
N_HEADS_A = 8
HEAD_DIM_A = 64
N_IDX_HEADS = 8
IDX_DIM = 64
TOPK_MAX = 256
N_HEADS_B = 4
HEAD_B = 128
N_HEADS_C = 4
HEAD_DIM_C = 128
N_BRANCH = 3
EPS = 1e-6
WIDTH = 512
IDX_SCALE = (IDX_DIM ** -0.5) * (N_IDX_HEADS ** -0.5)

LANES = 128
SUBLANES = 8
VMEM_LIMIT = 56 * 1024 * 1024

TM_PROJ = 256
TQ = 128
TKS = 256
TKA = 256
AUG_RADIX = 256
TKC = 512
SUM_ROWS = 16
CHUNK = 64
SUB = 16
T_HGRN = 128
NB_HGRN = 4
TM_MERGE = 512

NEG_BIG = -1e30
LOG2E = 1.4426950408889634
INT_MIN = -2147483648
KEY_LOWEST = -2139095040

NT = (((1,), (1,)), ((), ()))


def _dot(a, b):
    return jnp.dot(a, b, preferred_element_type=F32)


def _dot_nt(a, b):
    return lax.dot_general(a, b, NT, preferred_element_type=F32)


def _group_meansq(v, group):
    sq = v * v
    lane_group = lax.broadcasted_iota(I32, (v.shape[0], LANES), 1) // group
    blocks = []
    for b in range(v.shape[1] // LANES):
        blk = sq[:, b * LANES:(b + 1) * LANES]
        ms = jnp.zeros(blk.shape, F32)
        for g in range(LANES // group):
            in_group = lane_group == g
            total = jnp.sum(jnp.where(in_group, blk, 0.0), axis=-1, keepdims=True)
            ms = jnp.where(in_group, total, ms)
        blocks.append(ms)
    return jnp.concatenate(blocks, axis=1) * (1.0 / group)


_C_QA, _C_KA, _C_ZA, _C_QB, _C_FB, _C_IB, _C_GB, _C_QC, _C_ZC = [WIDTH * i for i in range(9)]
_C_GATES = 9 * WIDTH
_C_QI = _C_GATES + N_BRANCH * D_MODEL
_C_KI = _C_QI + WIDTH
_C_END = _C_KI + LANES


def _proj_body(x_ref, nin_ref, w_ref, wvt_ref, wwt_ref, lbp_ref, gqa_ref, gka_ref, gqc_ref,
               qa_ref, ka_ref, vat_ref, sza_ref, qi_ref, ki2_ref, wit_ref,
               sqb_ref, logf_ref, kk_ref, ib_ref, ibt_ref, sgb_ref, qc_ref, szc_ref, gates_ref, *, d_model):
    x = x_ref[...]
    ms = jnp.mean(x * x, axis=-1, keepdims=True)
    h = (x * lax.rsqrt(ms + EPS)) * nin_ref[...]
    hb = h.astype(BF16)

    def proj(c0, width):
        return lambda: _dot(hb, w_ref[:, c0:c0 + width])

    def head_norm(group, gain_ref, scale, out_ref):
        def epilogue(v):
            v = v * lax.rsqrt(_group_meansq(v, group) + EPS) * gain_ref[...]
            out_ref[...] = (v * scale).astype(BF16)
        return epilogue

    def store(out_ref, fn=lambda v: v):
        def epilogue(v):
            out_ref[...] = fn(v).astype(out_ref.dtype)
        return epilogue

    def store_time_blocks(out_ref):
        def epilogue(v):
            v = v.astype(BF16)
            for t in range(out_ref.shape[0]):
                out_ref[t] = v[:, t * LANES:(t + 1) * LANES]
        return epilogue

    def store_both_layouts(out_ref, out_t_ref):
        def epilogue(v):
            out_ref[...] = v.astype(BF16)
            store_time_blocks(out_t_ref)(v.T)
        return epilogue

    def forget(v):
        lbp = lbp_ref[...]
        lbe = jnp.exp(lbp - jnp.max(lbp, axis=0, keepdims=True))
        lb = lbe[0:1, :] / jnp.sum(lbe, axis=0, keepdims=True)
        f = lb + (1.0 - lb) * jax.nn.sigmoid(v)
        logf_ref[...] = jnp.log(f)
        kk_ref[...] = (1.0 - f).astype(BF16)

    def gate(n):
        def epilogue(v):
            gates_ref[:, n * d_model:(n + 1) * d_model] = jax.nn.sigmoid(v).astype(BF16)
        return epilogue

    stages = [
        (proj(_C_QA, WIDTH), head_norm(HEAD_DIM_A, gqa_ref, HEAD_DIM_A ** -0.5 * LOG2E, qa_ref)),
        (proj(_C_KA, WIDTH), head_norm(HEAD_DIM_A, gka_ref, 1.0, ka_ref)),
        (lambda: _dot_nt(wvt_ref[...], hb), store_time_blocks(vat_ref)),
        (proj(_C_ZA, WIDTH), store(sza_ref, jax.nn.silu)),
        (proj(_C_QI, WIDTH), store(qi_ref)),
        (proj(_C_KI, LANES), store(ki2_ref)),
        (lambda: _dot_nt(wwt_ref[...], hb), store(wit_ref)),
        (proj(_C_QB, WIDTH), store(sqb_ref, jax.nn.silu)),
        (proj(_C_FB, WIDTH), forget),
        (proj(_C_IB, WIDTH), store_both_layouts(ib_ref, ibt_ref)),
        (proj(_C_GB, WIDTH), store(sgb_ref, jax.nn.silu)),
        (proj(_C_QC, WIDTH), head_norm(HEAD_DIM_C, gqc_ref, HEAD_DIM_C ** -0.5, qc_ref)),
        (proj(_C_ZC, WIDTH), store(szc_ref, jax.nn.silu)),
    ] + [(proj(_C_GATES + n * d_model, d_model), gate(n)) for n in range(N_BRANCH)]
    pending = stages[0][0]()
    for k, (_, epilogue) in enumerate(stages):
        ahead = stages[k + 1][0]() if k + 1 < len(stages) else None
        epilogue(pending)
        pending = ahead


def _const_spec(shape):
    nd = len(shape)
    return pl.BlockSpec(shape, lambda *_: (0,) * nd, pipeline_mode=pl.Buffered(1))


def _proj_call(x2, norm_in, w_in, lower_bounds, q_norm_a, k_norm_a, q_norm_c):
    rows, d_model = x2.shape
    tm = TM_PROJ
    assert rows % tm == 0 and d_model == D_MODEL
    offs = np.cumsum([0] + [WIDTH] * 4 + [N_IDX_HEADS * IDX_DIM, IDX_DIM, N_IDX_HEADS] + [WIDTH] * 6
                     + [N_BRANCH * d_model])
    (o_qa, o_ka, o_va, o_za, o_qi, o_ki, o_wi, o_qb, o_fb, o_ib, o_gb, o_qc, o_zc, o_g, o_end) = [int(o) for o in offs]
    assert o_end == w_in.shape[1]
    wb = w_in.astype(BF16)
    col = lambda a, b: wb[:, a:b]
    w_main = jnp.concatenate([
        col(o_qa, o_ka), col(o_ka, o_va), col(o_za, o_qi), col(o_qb, o_fb), col(o_fb, o_ib), col(o_ib, o_gb),
        col(o_gb, o_qc), col(o_qc, o_zc), col(o_zc, o_g), col(o_g, o_end), col(o_qi, o_ki),
        col(o_ki, o_wi), col(o_ki, o_wi)], axis=1)
    assert w_main.shape[1] == _C_END
    wvt = col(o_va, o_za).T
    wwt = jnp.concatenate([col(o_wi, o_qb).T, jnp.zeros((8, d_model), BF16)], axis=0)
    tile = lambda g, reps: jnp.tile(g.astype(F32), reps)[None, :]
    n_slots = lower_bounds.shape[0]

    row_spec = lambda w: pl.BlockSpec((tm, w), lambda i: (i, 0))
    t_spec = pl.BlockSpec((tm // LANES, WIDTH, LANES), lambda i: (i, 0, 0))
    out_shape = [
        jax.ShapeDtypeStruct((rows, WIDTH), BF16),
        jax.ShapeDtypeStruct((rows, WIDTH), BF16),
        jax.ShapeDtypeStruct((rows // LANES, WIDTH, LANES), BF16),
        jax.ShapeDtypeStruct((rows, WIDTH), BF16),
        jax.ShapeDtypeStruct((rows, WIDTH), BF16),
        jax.ShapeDtypeStruct((rows, LANES), BF16),
        jax.ShapeDtypeStruct((16, rows), F32),
        jax.ShapeDtypeStruct((rows, WIDTH), BF16),
        jax.ShapeDtypeStruct((rows, WIDTH), F32),
        jax.ShapeDtypeStruct((rows, WIDTH), BF16),
        jax.ShapeDtypeStruct((rows, WIDTH), BF16),
        jax.ShapeDtypeStruct((rows // LANES, WIDTH, LANES), BF16),
        jax.ShapeDtypeStruct((rows, WIDTH), BF16),
        jax.ShapeDtypeStruct((rows, WIDTH), BF16),
        jax.ShapeDtypeStruct((rows, WIDTH), BF16),
        jax.ShapeDtypeStruct((rows, N_BRANCH * d_model), BF16),
    ]
    out_specs = [row_spec(WIDTH), row_spec(WIDTH), t_spec, row_spec(WIDTH), row_spec(WIDTH), row_spec(LANES),
                 pl.BlockSpec((16, tm), lambda i: (0, i)),
                 row_spec(WIDTH), row_spec(WIDTH), row_spec(WIDTH), row_spec(WIDTH), t_spec, row_spec(WIDTH),
                 row_spec(WIDTH), row_spec(WIDTH), row_spec(N_BRANCH * d_model)]
    in_specs = [row_spec(d_model), _const_spec((1, d_model)), _const_spec(w_main.shape), _const_spec(wvt.shape),
                _const_spec(wwt.shape), _const_spec((n_slots, WIDTH)), _const_spec((1, WIDTH)),
                _const_spec((1, WIDTH)), _const_spec((1, WIDTH))]
    return pl.pallas_call(
        functools.partial(_proj_body, d_model=d_model),
        out_shape=out_shape, grid=(rows // tm,), in_specs=in_specs, out_specs=out_specs,
        compiler_params=pltpu.CompilerParams(dimension_semantics=("parallel",), vmem_limit_bytes=VMEM_LIMIT),
        name="proj",
    )(x2, norm_in.astype(F32)[None, :], w_main, wvt, wwt, lower_bounds.astype(F32),
      tile(q_norm_a, N_HEADS_A), tile(k_norm_a, N_HEADS_A), tile(q_norm_c, N_HEADS_C))


def _memkv_body(m_ref, nm_ref, w_ref, gk_ref, kc_ref, vc_ref):
    x = m_ref[...]
    ms = jnp.mean(x * x, axis=-1, keepdims=True)
    hb = ((x * lax.rsqrt(ms + EPS)) * nm_ref[...]).astype(BF16)
    kc = _dot(hb, w_ref[:, 0:WIDTH])
    kc = kc * lax.rsqrt(_group_meansq(kc, HEAD_DIM_C) + EPS) * gk_ref[...]
    kc_ref[...] = kc.astype(BF16)
    vc_ref[...] = _dot(hb, w_ref[:, WIDTH:2 * WIDTH]).astype(BF16)


def _memkv_call(mem2, norm_mem, w_mem_kv, k_norm_c, tm):
    rows, d_model = mem2.shape
    row_spec = lambda w: pl.BlockSpec((tm, w), lambda i: (i, 0))
    return pl.pallas_call(
        _memkv_body,
        out_shape=[jax.ShapeDtypeStruct((rows, WIDTH), BF16)] * 2,
        grid=(rows // tm,),
        in_specs=[row_spec(d_model), _const_spec((1, d_model)), _const_spec((d_model, 2 * WIDTH)),
                  _const_spec((1, WIDTH))],
        out_specs=[row_spec(WIDTH), row_spec(WIDTH)],
        compiler_params=pltpu.CompilerParams(dimension_semantics=("parallel",), vmem_limit_bytes=VMEM_LIMIT),
        name="memkv",
    )(mem2, norm_mem.astype(F32)[None, :], w_mem_kv.astype(BF16),
      jnp.tile(k_norm_c.astype(F32), N_HEADS_C)[None, :])


def _key_to_f32(k):
    return pltpu.bitcast(jnp.where(k < 0, k ^ jnp.int32(0x7FFFFFFF), k), F32)


def _slope_log2(h):
    return (2.0 ** (-8.0 * (h + 1) / N_HEADS_A)) * LOG2E


N_SLOPE_PARTS = 3


def _slope_rows(h, row):
    rest = np.float32(_slope_log2(h))
    out = jnp.zeros(row.shape, F32)
    for r in range(N_SLOPE_PARTS):
        part = np.float32(np.asarray(rest, dtype=jnp.bfloat16))
        out = jnp.where(row == r, float(part), out)
        out = jnp.where(row == N_SLOPE_PARTS + r, float(part) * AUG_RADIX, out)
        rest = np.float32(rest - part)
    return out


def _dsa_body(qa_ref, qi_ref, wit_ref, sza_ref, ka_ref, vat_ref, ki2_ref, out_ref,
              sc_ref, qir_ref, qar_ref, m_ref, alpha_ref, acc_ref, pbuf_ref, sbuf_ref, rbuf_ref, ot_ref,
              *, topk, pos_bits):
    i = pl.program_id(1)
    q0 = i * TQ
    n_att = (q0 + TQ + TKA - 1) // TKA
    n_cnt = (q0 + TQ + TKC - 1) // TKC

    row = lax.broadcasted_iota(I32, (LANES, TQ), 0)
    row_lo = row < HEAD_DIM_A
    pairs = range(N_HEADS_A // 2)

    def pair_operand(ref, p):
        t = ref[:, p * LANES:(p + 1) * LANES].astype(F32).T
        return jnp.concatenate([jnp.where(row_lo, t, 0.0), jnp.where(row_lo, 0.0, t)], axis=1).astype(BF16)

    for p in pairs:
        qir_ref[p] = pair_operand(qi_ref, p)

    def index_dots(jt):
        kt = ki2_ref[pl.ds(pl.multiple_of(jt * TKC, TKC), TKC), :]
        return [_dot(kt, qir_ref[p]) for p in pairs]

    first = index_dots(0)
    for p in pairs:
        qar_ref[p, 0:LANES, :] = pair_operand(qa_ref, p)
        qar_ref[p, LANES:2 * LANES, :] = jnp.concatenate(
            [_slope_rows(2 * p, row), _slope_rows(2 * p + 1, row)], axis=1).astype(BF16)
    m_ref[...] = jnp.full(m_ref.shape, NEG_BIG, F32)
    acc_ref[...] = jnp.zeros(acc_ref.shape, F32)
    alpha_ref[...] = jnp.ones(alpha_ref.shape, F32)
    pbuf_ref[...] = jnp.zeros(pbuf_ref.shape, BF16)
    for p in pairs:
        rbuf_ref[p] = first[p]

    row_s = lax.broadcasted_iota(I32, (TKS, TQ), 0)
    tpos = q0 + lax.broadcasted_iota(I32, (TKS, TQ), 1)

    def score_tile(j, carry, look_ahead=True):
        ahead = index_dots(j + 1) if look_ahead else None
        for r in range(TKC // TKS):
            rows = slice(r * TKS, (r + 1) * TKS)
            r0 = pl.multiple_of(j * TKC + r * TKS, TKS)
            acc = jnp.zeros((TKS, TQ), F32)
            for p in pairs:
                rel = jnp.maximum(rbuf_ref[p, rows, :], 0.0)
                acc = acc + rel[:, :TQ] * wit_ref[2 * p:2 * p + 1, :] + rel[:, TQ:] * wit_ref[2 * p + 1:2 * p + 2, :]
            sc = acc * IDX_SCALE
            sc = jnp.where(sc == 0.0, 0.0, sc)
            sc_ref[pl.ds(r0, TKS), :] = jnp.where(r0 + row_s <= tpos, sc, -jnp.inf)
        if look_ahead:
            for p in pairs:
                rbuf_ref[p] = ahead[p]
        return carry

    lax.fori_loop(0, n_cnt - 1, score_tile, 0)
    score_tile(n_cnt - 1, 0, look_ahead=False)

    row_c = lax.broadcasted_iota(I32, (TKC, TQ), 0)

    def count(pred):
        def tile(j, acc):
            r0 = pl.multiple_of(j * TKC, TKC)
            c = jnp.where(pred(sc_ref[pl.ds(r0, TKC), :], r0), jnp.int32(1), jnp.int32(0))
            return acc + jnp.sum(c.reshape(TKC // SUBLANES, SUBLANES, TQ), axis=0)
        acc = lax.fori_loop(0, n_cnt, tile, jnp.zeros((SUBLANES, TQ), I32))
        return jnp.sum(acc, axis=0, keepdims=True)

    def bisect(it, state):
        p, n_p = state
        cand = p ^ lax.shift_left(jnp.int32(1), 31 - it)
        cand_f = _key_to_f32(cand)
        n = count(lambda s, r0: s >= cand_f)
        take = n >= topk
        return jnp.where(take, cand, p), jnp.where(take, n, n_p)

    kth, n_ge = lax.fori_loop(0, 32, bisect, (jnp.full((1, TQ), INT_MIN, I32), jnp.zeros((1, TQ), I32)))
    kth = jnp.maximum(kth, jnp.int32(KEY_LOWEST))
    thr = _key_to_f32(kth)
    nxt = _key_to_f32(kth + 1)
    n_gt = count(lambda s, r0: s >= nxt)
    need = topk - n_gt
    has_tie = n_ge > topk
    any_tie = jnp.max(jnp.where(has_tie, 1, 0)) > 0

    lane_a = lax.broadcasted_iota(I32, (TKA, LANES), 1)
    row_a = lax.broadcasted_iota(I32, (TKA, LANES), 0)
    aug = jnp.where(lane_a < N_SLOPE_PARTS, row_a % AUG_RADIX,
                    jnp.where(lane_a < 2 * N_SLOPE_PARTS, row_a // AUG_RADIX, 0)).astype(F32).astype(BF16)
    row_q = lax.broadcasted_iota(I32, (TKA, TQ), 0)
    half2 = lax.broadcasted_iota(I32, (1, 2 * TQ), 1) < TQ

    ones_rows = jnp.ones((SUM_ROWS, TKA), BF16)

    def value_update(jt):
        for p in pairs:
            vt = jnp.concatenate([vat_ref[jt * (TKA // LANES) + t, p * LANES:(p + 1) * LANES, :]
                                  for t in range(TKA // LANES)], axis=1)
            vt = jnp.concatenate([vt, ones_rows], axis=0)
            acc_ref[p] = alpha_ref[p] * acc_ref[p] + _dot(vt, pbuf_ref[p])

    def score_dots(jt):
        r0 = pl.multiple_of(jt * TKA, TKA)
        return [_dot(jnp.concatenate([ka_ref[pl.ds(r0, TKA), p * LANES:(p + 1) * LANES], aug], axis=1), qar_ref[p])
                for p in pairs]

    for p, s in zip(pairs, score_dots(0)):
        sbuf_ref[p] = s

    def tie_cut():
        def step(it, jp):
            cand = jp | lax.shift_left(jnp.int32(1), pos_bits - 1 - it)
            n = count(lambda s, r0: (s >= thr) & (s < nxt) & (r0 + row_c < cand))
            return jnp.where(n < need, cand, jp)
        return lax.fori_loop(0, pos_bits, step, jnp.zeros((1, TQ), I32))

    cut = lax.cond(any_tie, tie_cut, lambda: jnp.zeros((1, TQ), I32))
    cut = jnp.where(has_tie, cut, jnp.int32(2 ** 30))

    def att_tile(j, carry, look_ahead=True):
        r0 = pl.multiple_of(j * TKA, TKA)
        s_idx = sc_ref[pl.ds(r0, TKA), :]
        sel = (s_idx >= thr) & ((s_idx >= nxt) | (r0 + row_q <= cut))
        bias = jnp.where(sel, 0.0, NEG_BIG)
        bias2 = jnp.concatenate([bias, bias], axis=1)
        base = (r0 - q0).astype(F32)
        value_update(jnp.maximum(j - 1, 0))
        ahead = score_dots(j + 1) if look_ahead else None
        for p in pairs:
            shift = jnp.where(half2, _slope_log2(2 * p), _slope_log2(2 * p + 1)) * base
            s = sbuf_ref[p] + bias2
            m_old = m_ref[p]
            m_new = jnp.maximum(m_old, jnp.max(s, axis=0, keepdims=True) + shift)
            alpha = jnp.exp2(m_old - m_new)
            m_ref[p] = m_new
            alpha_ref[p] = alpha
            pbuf_ref[p] = jnp.exp2(s - (m_new - shift)).astype(BF16)
        if look_ahead:
            for p in pairs:
                sbuf_ref[p] = ahead[p]
        return carry

    lax.fori_loop(0, n_att - 1, att_tile, 0)
    att_tile(n_att - 1, 0, look_ahead=False)
    value_update(n_att - 1)

    for p in pairs:
        o = acc_ref[p, 0:LANES, :] * (1.0 / acc_ref[p, LANES:LANES + 1, :])
        ot_ref[p * LANES:(p + 1) * LANES, :] = jnp.where(row_lo, o[:, :TQ], o[:, TQ:])
    out_ref[...] = (ot_ref[...].T * sza_ref[...].astype(F32)).astype(BF16)


def _dsa_call(qa, qi, wit, sza, ka, vat, ki2, batch, seq):
    assert TQ == LANES and TKS % LANES == 0 and TKC % TKS == 0 and seq % TKC == 0
    assert TKA % LANES == 0 and TKC % TKA == 0 and TKA <= AUG_RADIX * AUG_RADIX
    nq = seq // TQ
    topk = min(TOPK_MAX, seq // 4)
    pos_bits = max(1, int(np.ceil(np.log2(seq))))
    q_spec = lambda w: pl.BlockSpec((TQ, w), lambda b, i: (b * nq + i, 0))
    b_spec = lambda w: pl.BlockSpec((seq, w), lambda b, i: (b, 0))
    n_pair = N_HEADS_A // 2
    return pl.pallas_call(
        functools.partial(_dsa_body, topk=topk, pos_bits=pos_bits),
        out_shape=jax.ShapeDtypeStruct((batch * seq, WIDTH), BF16),
        grid=(batch, nq),
        in_specs=[q_spec(WIDTH), q_spec(WIDTH), pl.BlockSpec((16, TQ), lambda b, i: (0, b * nq + i)), q_spec(WIDTH),
                  b_spec(WIDTH), pl.BlockSpec((seq // LANES, WIDTH, LANES), lambda b, i: (b, 0, 0)), b_spec(LANES)],
        out_specs=q_spec(WIDTH),
        scratch_shapes=[pltpu.VMEM((seq, TQ), F32),
                        pltpu.VMEM((n_pair, LANES, 2 * TQ), BF16), pltpu.VMEM((n_pair, 2 * LANES, 2 * TQ), BF16),
                        pltpu.VMEM((n_pair, 1, 2 * TQ), F32), pltpu.VMEM((n_pair, 1, 2 * TQ), F32),
                        pltpu.VMEM((n_pair, LANES + SUM_ROWS, 2 * TQ), F32),
                        pltpu.VMEM((n_pair, TKA, 2 * TQ), BF16), pltpu.VMEM((n_pair, TKA, 2 * TQ), F32),
                        pltpu.VMEM((n_pair, TKC, 2 * TQ), F32), pltpu.VMEM((WIDTH, TQ), F32)],
        compiler_params=pltpu.CompilerParams(dimension_semantics=("parallel", "arbitrary"),
                                             vmem_limit_bytes=VMEM_LIMIT),
        name="dsa",
    )(qa, qi, wit, sza, ka, vat, ki2)


def _split3(v):
    hi = v.astype(BF16)
    r = v - hi.astype(F32)
    mid = r.astype(BF16)
    lo = (r - mid.astype(F32)).astype(BF16)
    return hi, mid, lo


def _hgrn_body(sqb_ref, logf_ref, kk_ref, ib_ref, ibt_ref, sgb_ref, gon_ref, out_ref, st_ref):
    @pl.when(pl.program_id(1) == 0)
    def _():
        st_ref[...] = jnp.zeros_like(st_ref)

    r_i = lax.broadcasted_iota(I32, (CHUNK, CHUNK), 0)
    c_i = lax.broadcasted_iota(I32, (CHUNK, CHUNK), 1)
    tril = c_i <= r_i
    tri = jnp.where(tril, 1.0, 0.0).astype(BF16)
    n_sub = CHUNK // SUB
    row_c = lax.broadcasted_iota(I32, (CHUNK, LANES), 0)
    row_t = lax.broadcasted_iota(I32, (T_HGRN, LANES), 0)

    def decays(nb, c):
        g = logf_ref[nb, c * CHUNK:(c + 1) * CHUNK, :]
        return sum(_dot(tri, part) for part in _split3(g))

    def operands(nb, c, b):
        rows = slice(c * CHUNK, (c + 1) * CHUNK)
        q = sqb_ref[nb, rows, :].astype(F32)
        kk = kk_ref[nb, rows, :].astype(F32)
        b_last = b[CHUNK - 1:CHUNK, :]
        ref_k = jnp.concatenate(
            [jnp.broadcast_to(b[(j + 1) * SUB - 1:(j + 1) * SUB, :], (SUB, WIDTH)) for j in range(n_sub)], axis=0)
        k_in = kk * jnp.exp(ref_k - b)
        q_out = (q * jnp.exp(b)).astype(BF16)
        k_st = (kk * jnp.exp(b_last - b)).astype(BF16)
        q_in = []
        for j in range(n_sub):
            lo = j * SUB
            qj = q[lo:, :] * jnp.exp(b[lo:, :] - b[lo + SUB - 1:lo + SUB, :])
            if lo:
                qj = jnp.concatenate([jnp.zeros((lo, WIDTH), F32), qj], axis=0)
            q_in.append(qj.astype(BF16))
        return q_in, k_in, q_out, k_st, jnp.exp(b_last)

    def intra_scores(hd, q_in, k_in):
        cols = slice(hd * HEAD_B, (hd + 1) * HEAD_B)
        qs = jnp.concatenate([qj[:, cols] for qj in q_in], axis=1)
        kh = k_in[:, cols]
        ks = jnp.concatenate(
            [jnp.where((row_c >= j * SUB) & (row_c < (j + 1) * SUB), kh, 0.0) for j in range(n_sub)],
            axis=1).astype(BF16)
        return _dot_nt(qs, ks)

    def outputs(nb, c, hd, a_raw, q_out, k_st, decay_last):
        rows = slice(c * CHUNK, (c + 1) * CHUNK)
        cols = slice(hd * HEAD_B, (hd + 1) * HEAD_B)
        a = jnp.where(tril, a_raw, 0.0).astype(BF16)
        st = st_ref[nb, hd]
        o = _dot(a, ib_ref[nb, rows, cols]) + _dot_nt(q_out[:, cols], st.astype(BF16))
        k_pad = jnp.where((row_t >= c * CHUNK) & (row_t < (c + 1) * CHUNK),
                          jnp.concatenate([k_st[:, cols]] * (T_HGRN // CHUNK), axis=0), jnp.zeros((), BF16))
        st_ref[nb, hd] = st * decay_last[:, cols] + _dot(ibt_ref[nb, 0, cols, :], k_pad)
        return o

    def finish(nb, c, hd, o):
        rows = slice(c * CHUNK, (c + 1) * CHUNK)
        cols = slice(hd * HEAD_B, (hd + 1) * HEAD_B)
        ms = jnp.mean(o * o, axis=-1, keepdims=True)
        o = o * lax.rsqrt(ms + EPS) * gon_ref[...]
        out_ref[nb, rows, cols] = (o * sgb_ref[nb, rows, cols].astype(F32)).astype(BF16)

    seqs, heads = range(NB_HGRN), range(N_HEADS_B)
    for c in range(T_HGRN // CHUNK):
        bs = [decays(nb, c) for nb in seqs]
        ops = [operands(nb, c, bs[nb]) for nb in seqs]
        raw = [[intra_scores(hd, ops[nb][0], ops[nb][1]) for hd in heads] for nb in seqs]
        outs = [[outputs(nb, c, hd, raw[nb][hd], ops[nb][2], ops[nb][3], ops[nb][4]) for hd in heads] for nb in seqs]
        for nb in seqs:
            for hd in heads:
                finish(nb, c, hd, outs[nb][hd])


def _hgrn_call(sqb, logf, kk, ib, ibt, sgb, o_norm_b, batch, seq):
    assert T_HGRN == LANES and seq % T_HGRN == 0 and batch % NB_HGRN == 0
    nt = seq // T_HGRN
    as3 = lambda a: a.reshape(batch, seq, WIDTH)
    spec = pl.BlockSpec((NB_HGRN, T_HGRN, WIDTH), lambda b, t: (b, t, 0))
    ibt_spec = pl.BlockSpec((NB_HGRN, 1, WIDTH, T_HGRN), lambda b, t: (b, t, 0, 0))
    out = pl.pallas_call(
        _hgrn_body,
        out_shape=jax.ShapeDtypeStruct((batch, seq, WIDTH), BF16),
        grid=(batch // NB_HGRN, nt),
        in_specs=[spec, spec, spec, spec, ibt_spec, spec, _const_spec((1, HEAD_B))],
        out_specs=spec,
        scratch_shapes=[pltpu.VMEM((NB_HGRN, N_HEADS_B, HEAD_B, HEAD_B), F32)],
        compiler_params=pltpu.CompilerParams(dimension_semantics=("parallel", "arbitrary"),
                                             vmem_limit_bytes=VMEM_LIMIT),
        name="hgrn",
    )(as3(sqb), as3(logf), as3(kk), as3(ib), ibt.reshape(batch, nt, WIDTH, T_HGRN), as3(sgb),
      o_norm_b.astype(F32)[None, :])
    return out.reshape(batch * seq, WIDTH)


def _merge_body(x_ref, oa_ref, ob_ref, qc_ref, szc_ref, gates_ref, kc_ref, vc_ref, wb_ref, wo_ref, out_ref, *, d_model):
    def gated(n, br):
        return gates_ref[:, n * d_model:(n + 1) * d_model].astype(F32) * _dot(br, wb_ref[n])

    heads = [slice(hd * HEAD_DIM_C, (hd + 1) * HEAD_DIM_C) for hd in range(N_HEADS_C)]
    logits = [_dot_nt(qc_ref[:, cols], kc_ref[:, cols]) for cols in heads]
    merged = gated(0, oa_ref[...])
    probs = [jnp.exp(s - jnp.max(s, axis=-1, keepdims=True)) for s in logits]
    oc = [_dot(p.astype(BF16), vc_ref[:, cols]) * (1.0 / jnp.sum(p, axis=-1, keepdims=True))
          for p, cols in zip(probs, heads)]
    merged = merged + gated(1, ob_ref[...])
    oc = (jnp.concatenate(oc, axis=1) * szc_ref[...].astype(F32)).astype(BF16)
    merged = merged + gated(2, oc)
    out_ref[...] = x_ref[...] + _dot(merged.astype(BF16), wo_ref[...])


def _merge_call(x2, oa, ob, qc, szc, gates, kc, vc, w_branch, w_out, batch, seq, mem_len):
    rows, d_model = x2.shape
    tm = TM_MERGE
    nt = seq // tm
    row_spec = lambda w: pl.BlockSpec((tm, w), lambda b, t: (b * nt + t, 0))
    mem_spec = pl.BlockSpec((mem_len, WIDTH), lambda b, t: (b, 0))
    return pl.pallas_call(
        functools.partial(_merge_body, d_model=d_model),
        out_shape=jax.ShapeDtypeStruct((rows, d_model), F32),
        grid=(batch, nt),
        in_specs=[row_spec(d_model), row_spec(WIDTH), row_spec(WIDTH), row_spec(WIDTH), row_spec(WIDTH),
                  row_spec(N_BRANCH * d_model), mem_spec, mem_spec,
                  _const_spec((N_BRANCH, WIDTH, d_model)), _const_spec((d_model, d_model))],
        out_specs=row_spec(d_model),
        compiler_params=pltpu.CompilerParams(dimension_semantics=("parallel", "parallel"),
                                             vmem_limit_bytes=VMEM_LIMIT),
        name="merge",
    )(x2, oa, ob, qc, szc, gates, kc, vc, w_branch.astype(BF16), w_out.astype(BF16))


def _layer(x, mem, norm_in, norm_mem, w_in, q_norm_a, k_norm_a, lower_bounds, o_norm_b, w_mem_kv, q_norm_c, k_norm_c,
           w_branch, w_out):
    batch, seq, d_model = x.shape
    mem_len = mem.shape[1]
    x2 = x.reshape(batch * seq, d_model)
    (qa, ka, vat, sza, qi, ki2, wit, sqb, logf, kk, ib, ibt, sgb, qc, szc, gates) = _proj_call(
        x2, norm_in, w_in, lower_bounds, q_norm_a, k_norm_a, q_norm_c)
    kc, vc = _memkv_call(mem.reshape(batch * mem_len, d_model), norm_mem, w_mem_kv, k_norm_c, mem_len)
    oa = _dsa_call(qa, qi, wit, sza, ka, vat, ki2, batch, seq)
    ob = _hgrn_call(sqb, logf, kk, ib, ibt, sgb, o_norm_b, batch, seq)
    out = _merge_call(x2, oa, ob, qc, szc, gates, kc, vc, w_branch, w_out, batch, seq, mem_len)
    return out.reshape(batch, seq, d_model)


def kernel(x, mem, norm_in, norm_mem, w_in, q_norm_a, k_norm_a, lower_bounds, o_norm_b, w_mem_kv, q_norm_c, k_norm_c,
           w_branch, w_out):
    assert norm_in.shape[0] == 1, "single-layer block"
    return _layer(x, mem, norm_in[0], norm_mem[0], w_in[0], q_norm_a[0], k_norm_a[0], lower_bounds, o_norm_b[0],
                  w_mem_kv[0], q_norm_c[0], k_norm_c[0], w_branch[0], w_out[0])
```

```python
import functools

import jax
import jax.numpy as jnp
import numpy as np
from jax import lax
from jax.experimental import pallas as pl
from jax.experimental.pallas import tpu as pltpu

F32 = jnp.float32
BF16 = jnp.bfloat16
I32 = jnp.int32

D_MODEL = 1024
N_HEADS_A = 8
HEAD_DIM_A = 64
N_IDX_HEADS = 8
IDX_DIM = 64
TOPK_MAX = 256
N_HEADS_B = 4
HEAD_B = 128
N_HEADS_C = 4
HEAD_DIM_C = 128
N_BRANCH = 3
EPS = 1e-6
WIDTH = 512
IDX_SCALE = (IDX_DIM ** -0.5) * (N_IDX_HEADS ** -0.5)

LANES = 128
SUBLANES = 8
VMEM_LIMIT = 56 * 1024 * 1024

TM_PROJ = 256
TQ = 128
TKS = 256
TKA = 256
AUG_RADIX = 256
TKC = 512
SUM_ROWS = 16
CHUNK = 64
SUB = 16
T_HGRN = 128
NB_HGRN = 4
TM_MERGE = 512

NEG_BIG = -1e30
LOG2E = 1.4426950408889634
INT_MIN = -2147483648
KEY_LOWEST = -2139095040

NT = (((1,), (1,)), ((), ()))


def _dot(a, b):
    return jnp.dot(a, b, preferred_element_type=F32)


def _dot_nt(a, b):
    return lax.dot_general(a, b, NT, preferred_element_type=F32)


def _group_meansq(v, group):
    sq = v * v
    lane_group = lax.broadcasted_iota(I32, (v.shape[0], LANES), 1) // group
    blocks = []
    for b in range(v.shape[1] // LANES):
        blk = sq[:, b * LANES:(b + 1) * LANES]
        ms = jnp.zeros(blk.shape, F32)
        for g in range(LANES // group):
            in_group = lane_group == g
            total = jnp.sum(jnp.where(in_group, blk, 0.0), axis=-1, keepdims=True)
            ms = jnp.where(in_group, total, ms)
        blocks.append(ms)
    return jnp.concatenate(blocks, axis=1) * (1.0 / group)


_C_QA, _C_KA, _C_ZA, _C_QB, _C_FB, _C_IB, _C_GB, _C_QC, _C_ZC = [WIDTH * i for i in range(9)]
_C_GATES = 9 * WIDTH
_C_QI = _C_GATES + N_BRANCH * D_MODEL
_C_KI = _C_QI + WIDTH
_C_END = _C_KI + LANES


def _proj_body(x_ref, nin_ref, w_ref, wvt_ref, wwt_ref, lbp_ref, gqa_ref, gka_ref, gqc_ref,
               qa_ref, ka_ref, vat_ref, sza_ref, qi_ref, ki2_ref, wit_ref,
               sqb_ref, logf_ref, kk_ref, ib_ref, ibt_ref, sgb_ref, qc_ref, szc_ref, gates_ref, *, d_model):
    x = x_ref[...]
    ms = jnp.mean(x * x, axis=-1, keepdims=True)
    h = (x * lax.rsqrt(ms + EPS)) * nin_ref[...]
    hb = h.astype(BF16)

    def proj(c0, width):
        return lambda: _dot(hb, w_ref[:, c0:c0 + width])

    def head_norm(group, gain_ref, scale, out_ref):
        def epilogue(v):
            v = v * lax.rsqrt(_group_meansq(v, group) + EPS) * gain_ref[...]
            out_ref[...] = (v * scale).astype(BF16)
        return epilogue

    def store(out_ref, fn=lambda v: v):
        def epilogue(v):
            out_ref[...] = fn(v).astype(out_ref.dtype)
        return epilogue

    def store_time_blocks(out_ref):
        def epilogue(v):
            v = v.astype(BF16)
            for t in range(out_ref.shape[0]):
                out_ref[t] = v[:, t * LANES:(t + 1) * LANES]
        return epilogue

    def store_both_layouts(out_ref, out_t_ref):
        def epilogue(v):
            out_ref[...] = v.astype(BF16)
            store_time_blocks(out_t_ref)(v.T)
        return epilogue

    def forget(v):
        lbp = lbp_ref[...]
        lbe = jnp.exp(lbp - jnp.max(lbp, axis=0, keepdims=True))
        lb = lbe[0:1, :] / jnp.sum(lbe, axis=0, keepdims=True)
        f = lb + (1.0 - lb) * jax.nn.sigmoid(v)
        logf_ref[...] = jnp.log(f)
        kk_ref[...] = (1.0 - f).astype(BF16)

    def gate(n):
        def epilogue(v):
            gates_ref[:, n * d_model:(n + 1) * d_model] = jax.nn.sigmoid(v).astype(BF16)
        return epilogue

    stages = [
        (proj(_C_QA, WIDTH), head_norm(HEAD_DIM_A, gqa_ref, HEAD_DIM_A ** -0.5 * LOG2E, qa_ref)),
        (proj(_C_KA, WIDTH), head_norm(HEAD_DIM_A, gka_ref, 1.0, ka_ref)),
        (lambda: _dot_nt(wvt_ref[...], hb), store_time_blocks(vat_ref)),
        (proj(_C_ZA, WIDTH), store(sza_ref, jax.nn.silu)),
        (proj(_C_QI, WIDTH), store(qi_ref)),
        (proj(_C_KI, LANES), store(ki2_ref)),
        (lambda: _dot_nt(wwt_ref[...], hb), store(wit_ref)),
        (proj(_C_QB, WIDTH), store(sqb_ref, jax.nn.silu)),
        (proj(_C_FB, WIDTH), forget),
        (proj(_C_IB, WIDTH), store_both_layouts(ib_ref, ibt_ref)),
        (proj(_C_GB, WIDTH), store(sgb_ref, jax.nn.silu)),
        (proj(_C_QC, WIDTH), head_norm(HEAD_DIM_C, gqc_ref, HEAD_DIM_C ** -0.5, qc_ref)),
        (proj(_C_ZC, WIDTH), store(szc_ref, jax.nn.silu)),
    ] + [(proj(_C_GATES + n * d_model, d_model), gate(n)) for n in range(N_BRANCH)]
    pending = stages[0][0]()
    for k, (_, epilogue) in enumerate(stages):
        ahead = stages[k + 1][0]() if k + 1 < len(stages) else None
        epilogue(pending)
        pending = ahead


def _const_spec(shape):
    nd = len(shape)
    return pl.BlockSpec(shape, lambda *_: (0,) * nd, pipeline_mode=pl.Buffered(1))


def _proj_call(x2, norm_in, w_in, lower_bounds, q_norm_a, k_norm_a, q_norm_c):
    rows, d_model = x2.shape
    tm = TM_PROJ
    assert rows % tm == 0 and d_model == D_MODEL
    offs = np.cumsum([0] + [WIDTH] * 4 + [N_IDX_HEADS * IDX_DIM, IDX_DIM, N_IDX_HEADS] + [WIDTH] * 6
                     + [N_BRANCH * d_model])
    (o_qa, o_ka, o_va, o_za, o_qi, o_ki, o_wi, o_qb, o_fb, o_ib, o_gb, o_qc, o_zc, o_g, o_end) = [int(o) for o in offs]
    assert o_end == w_in.shape[1]
    wb = w_in.astype(BF16)
    col = lambda a, b: wb[:, a:b]
    w_main = jnp.concatenate([
        col(o_qa, o_ka), col(o_ka, o_va), col(o_za, o_qi), col(o_qb, o_fb), col(o_fb, o_ib), col(o_ib, o_gb),
        col(o_gb, o_qc), col(o_qc, o_zc), col(o_zc, o_g), col(o_g, o_end), col(o_qi, o_ki),
        col(o_ki, o_wi), col(o_ki, o_wi)], axis=1)
    assert w_main.shape[1] == _C_END
    wvt = col(o_va, o_za).T
    wwt = jnp.concatenate([col(o_wi, o_qb).T, jnp.zeros((8, d_model), BF16)], axis=0)
    tile = lambda g, reps: jnp.tile(g.astype(F32), reps)[None, :]
    n_slots = lower_bounds.shape[0]

    row_spec = lambda w: pl.BlockSpec((tm, w), lambda i: (i, 0))
    t_spec = pl.BlockSpec((tm // LANES, WIDTH, LANES), lambda i: (i, 0, 0))
    out_shape = [
        jax.ShapeDtypeStruct((rows, WIDTH), BF16),
        jax.ShapeDtypeStruct((rows, WIDTH), BF16),
        jax.ShapeDtypeStruct((rows // LANES, WIDTH, LANES), BF16),
        jax.ShapeDtypeStruct((rows, WIDTH), BF16),
        jax.ShapeDtypeStruct((rows, WIDTH), BF16),
        jax.ShapeDtypeStruct((rows, LANES), BF16),
        jax.ShapeDtypeStruct((16, rows), F32),
        jax.ShapeDtypeStruct((rows, WIDTH), BF16),
        jax.ShapeDtypeStruct((rows, WIDTH), F32),
        jax.ShapeDtypeStruct((rows, WIDTH), BF16),
        jax.ShapeDtypeStruct((rows, WIDTH), BF16),
        jax.ShapeDtypeStruct((rows // LANES, WIDTH, LANES), BF16),
        jax.ShapeDtypeStruct((rows, WIDTH), BF16),
        jax.ShapeDtypeStruct((rows, WIDTH), BF16),
        jax.ShapeDtypeStruct((rows, WIDTH), BF16),
        jax.ShapeDtypeStruct((rows, N_BRANCH * d_model), BF16),
    ]
    out_specs = [row_spec(WIDTH), row_spec(WIDTH), t_spec, row_spec(WIDTH), row_spec(WIDTH), row_spec(LANES),
                 pl.BlockSpec((16, tm), lambda i: (0, i)),
                 row_spec(WIDTH), row_spec(WIDTH), row_spec(WIDTH), row_spec(WIDTH), t_spec, row_spec(WIDTH),
                 row_spec(WIDTH), row_spec(WIDTH), row_spec(N_BRANCH * d_model)]
    in_specs = [row_spec(d_model), _const_spec((1, d_model)), _const_spec(w_main.shape), _const_spec(wvt.shape),
                _const_spec(wwt.shape), _const_spec((n_slots, WIDTH)), _const_spec((1, WIDTH)),
                _const_spec((1, WIDTH)), _const_spec((1, WIDTH))]
    return pl.pallas_call(
        functools.partial(_proj_body, d_model=d_model),
        out_shape=out_shape, grid=(rows // tm,), in_specs=in_specs, out_specs=out_specs,
        compiler_params=pltpu.CompilerParams(dimension_semantics=("parallel",), vmem_limit_bytes=VMEM_LIMIT),
        name="proj",
    )(x2, norm_in.astype(F32)[None, :], w_main, wvt, wwt, lower_bounds.astype(F32),
      tile(q_norm_a, N_HEADS_A), tile(k_norm_a, N_HEADS_A), tile(q_norm_c, N_HEADS_C))


def _memkv_body(m_ref, nm_ref, w_ref, gk_ref, kc_ref, vc_ref):
    x = m_ref[...]
    ms = jnp.mean(x * x, axis=-1, keepdims=True)
    hb = ((x * lax.rsqrt(ms + EPS)) * nm_ref[...]).astype(BF16)
    kc = _dot(hb, w_ref[:, 0:WIDTH])
    kc = kc * lax.rsqrt(_group_meansq(kc, HEAD_DIM_C) + EPS) * gk_ref[...]
    kc_ref[...] = kc.astype(BF16)
    vc_ref[...] = _dot(hb, w_ref[:, WIDTH:2 * WIDTH]).astype(BF16)


def _memkv_call(mem2, norm_mem, w_mem_kv, k_norm_c, tm):
    rows, d_model = mem2.shape
    row_spec = lambda w: pl.BlockSpec((tm, w), lambda i: (i, 0))
    return pl.pallas_call(
        _memkv_body,
        out_shape=[jax.ShapeDtypeStruct((rows, WIDTH), BF16)] * 2,
        grid=(rows // tm,),
        in_specs=[row_spec(d_model), _const_spec((1, d_model)), _const_spec((d_model, 2 * WIDTH)),
                  _const_spec((1, WIDTH))],
        out_specs=[row_spec(WIDTH), row_spec(WIDTH)],
        compiler_params=pltpu.CompilerParams(dimension_semantics=("parallel",), vmem_limit_bytes=VMEM_LIMIT),
        name="memkv",
    )(mem2, norm_mem.astype(F32)[None, :], w_mem_kv.astype(BF16),
      jnp.tile(k_norm_c.astype(F32), N_HEADS_C)[None, :])


def _key_to_f32(k):
    return pltpu.bitcast(jnp.where(k < 0, k ^ jnp.int32(0x7FFFFFFF), k), F32)


def _slope_log2(h):
    return (2.0 ** (-8.0 * (h + 1) / N_HEADS_A)) * LOG2E


N_SLOPE_PARTS = 3


def _slope_rows(h, row):
    rest = np.float32(_slope_log2(h))
    out = jnp.zeros(row.shape, F32)
    for r in range(N_SLOPE_PARTS):
        part = np.float32(np.asarray(rest, dtype=jnp.bfloat16))
        out = jnp.where(row == r, float(part), out)
        out = jnp.where(row == N_SLOPE_PARTS + r, float(part) * AUG_RADIX, out)
        rest = np.float32(rest - part)
    return out


def _dsa_body(qa_ref, qi_ref, wit_ref, sza_ref, ka_ref, vat_ref, ki2_ref, out_ref,
              sc_ref, qir_ref, qar_ref, m_ref, alpha_ref, acc_ref, pbuf_ref, sbuf_ref, rbuf_ref, ot_ref,
              *, topk, pos_bits):
    i = pl.program_id(1)
    q0 = i * TQ
    n_att = (q0 + TQ + TKA - 1) // TKA
    n_cnt = (q0 + TQ + TKC - 1) // TKC

    row = lax.broadcasted_iota(I32, (LANES, TQ), 0)
    row_lo = row < HEAD_DIM_A
    pairs = range(N_HEADS_A // 2)

    def pair_operand(ref, p):
        t = ref[:, p * LANES:(p + 1) * LANES].astype(F32).T
        return jnp.concatenate([jnp.where(row_lo, t, 0.0), jnp.where(row_lo, 0.0, t)], axis=1).astype(BF16)

    for p in pairs:
        qir_ref[p] = pair_operand(qi_ref, p)

    def index_dots(jt):
        kt = ki2_ref[pl.ds(pl.multiple_of(jt * TKC, TKC), TKC), :]
        return [_dot(kt, qir_ref[p]) for p in pairs]

    first = index_dots(0)
    for p in pairs:
        qar_ref[p, 0:LANES, :] = pair_operand(qa_ref, p)
        qar_ref[p, LANES:2 * LANES, :] = jnp.concatenate(
            [_slope_rows(2 * p, row), _slope_rows(2 * p + 1, row)], axis=1).astype(BF16)
    m_ref[...] = jnp.full(m_ref.shape, NEG_BIG, F32)
    acc_ref[...] = jnp.zeros(acc_ref.shape, F32)
    alpha_ref[...] = jnp.ones(alpha_ref.shape, F32)
    pbuf_ref[...] = jnp.zeros(pbuf_ref.shape, BF16)
    for p in pairs:
        rbuf_ref[p] = first[p]

    row_s = lax.broadcasted_iota(I32, (TKS, TQ), 0)
    tpos = q0 + lax.broadcasted_iota(I32, (TKS, TQ), 1)

    def score_tile(j, carry, look_ahead=True):
        ahead = index_dots(j + 1) if look_ahead else None
        for r in range(TKC // TKS):
            rows = slice(r * TKS, (r + 1) * TKS)
            r0 = pl.multiple_of(j * TKC + r * TKS, TKS)
            acc = jnp.zeros((TKS, TQ), F32)
            for p in pairs:
                rel = jnp.maximum(rbuf_ref[p, rows, :], 0.0)
                acc = acc + rel[:, :TQ] * wit_ref[2 * p:2 * p + 1, :] + rel[:, TQ:] * wit_ref[2 * p + 1:2 * p + 2, :]
            sc = acc * IDX_SCALE
            sc = jnp.where(sc == 0.0, 0.0, sc)
            sc_ref[pl.ds(r0, TKS), :] = jnp.where(r0 + row_s <= tpos, sc, -jnp.inf)
        if look_ahead:
            for p in pairs:
                rbuf_ref[p] = ahead[p]
        return carry

    lax.fori_loop(0, n_cnt - 1, score_tile, 0)
    score_tile(n_cnt - 1, 0, look_ahead=False)

    row_c = lax.broadcasted_iota(I32, (TKC, TQ), 0)

    def count(pred):
        def tile(j, acc):
            r0 = pl.multiple_of(j * TKC, TKC)
            c = jnp.where(pred(sc_ref[pl.ds(r0, TKC), :], r0), jnp.int32(1), jnp.int32(0))
            return acc + jnp.sum(c.reshape(TKC // SUBLANES, SUBLANES, TQ), axis=0)
        acc = lax.fori_loop(0, n_cnt, tile, jnp.zeros((SUBLANES, TQ), I32))
        return jnp.sum(acc, axis=0, keepdims=True)

    def bisect(it, state):
        p, n_p = state
        cand = p ^ lax.shift_left(jnp.int32(1), 31 - it)
        cand_f = _key_to_f32(cand)
        n = count(lambda s, r0: s >= cand_f)
        take = n >= topk
        return jnp.where(take, cand, p), jnp.where(take, n, n_p)

    kth, n_ge = lax.fori_loop(0, 32, bisect, (jnp.full((1, TQ), INT_MIN, I32), jnp.zeros((1, TQ), I32)))
    kth = jnp.maximum(kth, jnp.int32(KEY_LOWEST))
    thr = _key_to_f32(kth)
    nxt = _key_to_f32(kth + 1)
    has_tie = n_ge > topk
    any_tie = jnp.max(jnp.where(has_tie, 1, 0)) > 0

    lane_a = lax.broadcasted_iota(I32, (TKA, LANES), 1)
    row_a = lax.broadcasted_iota(I32, (TKA, LANES), 0)
    aug = jnp.where(lane_a < N_SLOPE_PARTS, row_a % AUG_RADIX,
                    jnp.where(lane_a < 2 * N_SLOPE_PARTS, row_a // AUG_RADIX, 0)).astype(F32).astype(BF16)
    row_q = lax.broadcasted_iota(I32, (TKA, TQ), 0)
    half2 = lax.broadcasted_iota(I32, (1, 2 * TQ), 1) < TQ

    ones_rows = jnp.ones((SUM_ROWS, TKA), BF16)

    def value_update(jt):
        for p in pairs:
            vt = jnp.concatenate([vat_ref[jt * (TKA // LANES) + t, p * LANES:(p + 1) * LANES, :]
                                  for t in range(TKA // LANES)], axis=1)
            vt = jnp.concatenate([vt, ones_rows], axis=0)
            acc_ref[p] = alpha_ref[p] * acc_ref[p] + _dot(vt, pbuf_ref[p])

    def score_dots(jt):
        r0 = pl.multiple_of(jt * TKA, TKA)
        return [_dot(jnp.concatenate([ka_ref[pl.ds(r0, TKA), p * LANES:(p + 1) * LANES], aug], axis=1), qar_ref[p])
                for p in pairs]

    for p, s in zip(pairs, score_dots(0)):
        sbuf_ref[p] = s

    def tie_cut():
        need = topk - count(lambda s, r0: s >= nxt)

        def step(it, jp):
            cand = jp | lax.shift_left(jnp.int32(1), pos_bits - 1 - it)
            n = count(lambda s, r0: (s >= thr) & (s < nxt) & (r0 + row_c < cand))
            return jnp.where(n < need, cand, jp)
        return lax.fori_loop(0, pos_bits, step, jnp.zeros((1, TQ), I32))

    cut = lax.cond(any_tie, tie_cut, lambda: jnp.zeros((1, TQ), I32))
    cut = jnp.where(has_tie, cut, jnp.int32(2 ** 30))

    def att_tile(j, carry, look_ahead=True):
        r0 = pl.multiple_of(j * TKA, TKA)
        s_idx = sc_ref[pl.ds(r0, TKA), :]
        sel = (s_idx >= thr) & ((s_idx >= nxt) | (r0 + row_q <= cut))
        bias = jnp.where(sel, 0.0, NEG_BIG)
        bias2 = jnp.concatenate([bias, bias], axis=1)
        base = (r0 - q0).astype(F32)
        value_update(jnp.maximum(j - 1, 0))
        ahead = score_dots(j + 1) if look_ahead else None
        for p in pairs:
            shift = jnp.where(half2, _slope_log2(2 * p), _slope_log2(2 * p + 1)) * base
            s = sbuf_ref[p] + bias2
            m_old = m_ref[p]
            m_new = jnp.maximum(m_old, jnp.max(s, axis=0, keepdims=True) + shift)
            alpha = jnp.exp2(m_old - m_new)
            m_ref[p] = m_new
            alpha_ref[p] = alpha
            pbuf_ref[p] = jnp.exp2(s - (m_new - shift)).astype(BF16)
        if look_ahead:
            for p in pairs:
                sbuf_ref[p] = ahead[p]
        return carry

    lax.fori_loop(0, n_att - 1, att_tile, 0)
    att_tile(n_att - 1, 0, look_ahead=False)
    value_update(n_att - 1)

    for p in pairs:
        o = acc_ref[p, 0:LANES, :] * (1.0 / acc_ref[p, LANES:LANES + 1, :])
        ot_ref[p * LANES:(p + 1) * LANES, :] = jnp.where(row_lo, o[:, :TQ], o[:, TQ:])
    out_ref[...] = (ot_ref[...].T * sza_ref[...].astype(F32)).astype(BF16)


def _dsa_call(qa, qi, wit, sza, ka, vat, ki2, batch, seq):
    assert TQ == LANES and TKS % LANES == 0 and TKC % TKS == 0 and seq % TKC == 0
    assert TKA % LANES == 0 and TKC % TKA == 0 and TKA <= AUG_RADIX * AUG_RADIX
    nq = seq // TQ
    topk = min(TOPK_MAX, seq // 4)
    pos_bits = max(1, int(np.ceil(np.log2(seq))))
    q_spec = lambda w: pl.BlockSpec((TQ, w), lambda b, i: (b * nq + i, 0))
    b_spec = lambda w: pl.BlockSpec((seq, w), lambda b, i: (b, 0))
    n_pair = N_HEADS_A // 2
    return pl.pallas_call(
        functools.partial(_dsa_body, topk=topk, pos_bits=pos_bits),
        out_shape=jax.ShapeDtypeStruct((batch * seq, WIDTH), BF16),
        grid=(batch, nq),
        in_specs=[q_spec(WIDTH), q_spec(WIDTH), pl.BlockSpec((16, TQ), lambda b, i: (0, b * nq + i)), q_spec(WIDTH),
                  b_spec(WIDTH), pl.BlockSpec((seq // LANES, WIDTH, LANES), lambda b, i: (b, 0, 0)), b_spec(LANES)],
        out_specs=q_spec(WIDTH),
        scratch_shapes=[pltpu.VMEM((seq, TQ), F32),
                        pltpu.VMEM((n_pair, LANES, 2 * TQ), BF16), pltpu.VMEM((n_pair, 2 * LANES, 2 * TQ), BF16),
                        pltpu.VMEM((n_pair, 1, 2 * TQ), F32), pltpu.VMEM((n_pair, 1, 2 * TQ), F32),
                        pltpu.VMEM((n_pair, LANES + SUM_ROWS, 2 * TQ), F32),
                        pltpu.VMEM((n_pair, TKA, 2 * TQ), BF16), pltpu.VMEM((n_pair, TKA, 2 * TQ), F32),
                        pltpu.VMEM((n_pair, TKC, 2 * TQ), F32), pltpu.VMEM((WIDTH, TQ), F32)],
        compiler_params=pltpu.CompilerParams(dimension_semantics=("parallel", "arbitrary"),
                                             vmem_limit_bytes=VMEM_LIMIT),
        name="dsa",
    )(qa, qi, wit, sza, ka, vat, ki2)


def _split3(v):
    hi = v.astype(BF16)
    r = v - hi.astype(F32)
    mid = r.astype(BF16)
    lo = (r - mid.astype(F32)).astype(BF16)
    return hi, mid, lo


def _hgrn_body(sqb_ref, logf_ref, kk_ref, ib_ref, ibt_ref, sgb_ref, gon_ref, out_ref, st_ref):
    @pl.when(pl.program_id(1) == 0)
    def _():
        st_ref[...] = jnp.zeros_like(st_ref)

    r_i = lax.broadcasted_iota(I32, (CHUNK, CHUNK), 0)
    c_i = lax.broadcasted_iota(I32, (CHUNK, CHUNK), 1)
    tril = c_i <= r_i
    tri = jnp.where(tril, 1.0, 0.0).astype(BF16)
    n_sub = CHUNK // SUB
    row_c = lax.broadcasted_iota(I32, (CHUNK, LANES), 0)
    row_t = lax.broadcasted_iota(I32, (T_HGRN, LANES), 0)

    def decays(nb, c):
        g = logf_ref[nb, c * CHUNK:(c + 1) * CHUNK, :]
        return sum(_dot(tri, part) for part in _split3(g))

    def operands(nb, c, b):
        rows = slice(c * CHUNK, (c + 1) * CHUNK)
        q = sqb_ref[nb, rows, :].astype(F32)
        kk = kk_ref[nb, rows, :].astype(F32)
        b_last = b[CHUNK - 1:CHUNK, :]
        ref_k = jnp.concatenate(
            [jnp.broadcast_to(b[(j + 1) * SUB - 1:(j + 1) * SUB, :], (SUB, WIDTH)) for j in range(n_sub)], axis=0)
        k_in = kk * jnp.exp(ref_k - b)
        q_out = (q * jnp.exp(b)).astype(BF16)
        k_st = (kk * jnp.exp(b_last - b)).astype(BF16)
        q_in = []
        for j in range(n_sub):
            lo = j * SUB
            qj = q[lo:, :] * jnp.exp(b[lo:, :] - b[lo + SUB - 1:lo + SUB, :])
            if lo:
                qj = jnp.concatenate([jnp.zeros((lo, WIDTH), F32), qj], axis=0)
            q_in.append(qj.astype(BF16))
        return q_in, k_in, q_out, k_st, jnp.exp(b_last)

    def intra_scores(hd, q_in, k_in):
        cols = slice(hd * HEAD_B, (hd + 1) * HEAD_B)
        qs = jnp.concatenate([qj[:, cols] for qj in q_in], axis=1)
        kh = k_in[:, cols]
        ks = jnp.concatenate(
            [jnp.where((row_c >= j * SUB) & (row_c < (j + 1) * SUB), kh, 0.0) for j in range(n_sub)],
            axis=1).astype(BF16)
        return _dot_nt(qs, ks)

    def outputs(nb, c, hd, a_raw, q_out, k_st, decay_last):
        rows = slice(c * CHUNK, (c + 1) * CHUNK)
        cols = slice(hd * HEAD_B, (hd + 1) * HEAD_B)
        a = jnp.where(tril, a_raw, 0.0).astype(BF16)
        st = st_ref[nb, hd]
        o = _dot(a, ib_ref[nb, rows, cols]) + _dot_nt(q_out[:, cols], st.astype(BF16))
        k_pad = jnp.where((row_t >= c * CHUNK) & (row_t < (c + 1) * CHUNK),
                          jnp.concatenate([k_st[:, cols]] * (T_HGRN // CHUNK), axis=0), jnp.zeros((), BF16))
        st_ref[nb, hd] = st * decay_last[:, cols] + _dot(ibt_ref[nb, 0, cols, :], k_pad)
        return o

    def finish(nb, c, hd, o):
        rows = slice(c * CHUNK, (c + 1) * CHUNK)
        cols = slice(hd * HEAD_B, (hd + 1) * HEAD_B)
        ms = jnp.mean(o * o, axis=-1, keepdims=True)
        o = o * lax.rsqrt(ms + EPS) * gon_ref[...]
        out_ref[nb, rows, cols] = (o * sgb_ref[nb, rows, cols].astype(F32)).astype(BF16)

    seqs, heads = range(NB_HGRN), range(N_HEADS_B)
    for c in range(T_HGRN // CHUNK):
        bs = [decays(nb, c) for nb in seqs]
        ops = [operands(nb, c, bs[nb]) for nb in seqs]
        raw = [[intra_scores(hd, ops[nb][0], ops[nb][1]) for hd in heads] for nb in seqs]
        outs = [[outputs(nb, c, hd, raw[nb][hd], ops[nb][2], ops[nb][3], ops[nb][4]) for hd in heads] for nb in seqs]
        for nb in seqs:
            for hd in heads:
                finish(nb, c, hd, outs[nb][hd])


def _hgrn_call(sqb, logf, kk, ib, ibt, sgb, o_norm_b, batch, seq):
    assert T_HGRN == LANES and seq % T_HGRN == 0 and batch % NB_HGRN == 0
    nt = seq // T_HGRN
    as3 = lambda a: a.reshape(batch, seq, WIDTH)
    spec = pl.BlockSpec((NB_HGRN, T_HGRN, WIDTH), lambda b, t: (b, t, 0))
    ibt_spec = pl.BlockSpec((NB_HGRN, 1, WIDTH, T_HGRN), lambda b, t: (b, t, 0, 0))
    out = pl.pallas_call(
        _hgrn_body,
        out_shape=jax.ShapeDtypeStruct((batch, seq, WIDTH), BF16),
        grid=(batch // NB_HGRN, nt),
        in_specs=[spec, spec, spec, spec, ibt_spec, spec, _const_spec((1, HEAD_B))],
        out_specs=spec,
        scratch_shapes=[pltpu.VMEM((NB_HGRN, N_HEADS_B, HEAD_B, HEAD_B), F32)],
        compiler_params=pltpu.CompilerParams(dimension_semantics=("parallel", "arbitrary"),
                                             vmem_limit_bytes=VMEM_LIMIT),
        name="hgrn",
    )(as3(sqb), as3(logf), as3(kk), as3(ib), ibt.reshape(batch, nt, WIDTH, T_HGRN), as3(sgb),
      o_norm_b.astype(F32)[None, :])
    return out.reshape(batch * seq, WIDTH)


def _merge_body(x_ref, oa_ref, ob_ref, qc_ref, szc_ref, gates_ref, kc_ref, vc_ref, wb_ref, wo_ref, out_ref, *, d_model):
    def gated(n, br):
        return gates_ref[:, n * d_model:(n + 1) * d_model].astype(F32) * _dot(br, wb_ref[n])

    heads = [slice(hd * HEAD_DIM_C, (hd + 1) * HEAD_DIM_C) for hd in range(N_HEADS_C)]
    logits = [_dot_nt(qc_ref[:, cols], kc_ref[:, cols]) for cols in heads]
    merged = gated(0, oa_ref[...])
    probs = [jnp.exp(s - jnp.max(s, axis=-1, keepdims=True)) for s in logits]
    oc = [_dot(p.astype(BF16), vc_ref[:, cols]) * (1.0 / jnp.sum(p, axis=-1, keepdims=True))
          for p, cols in zip(probs, heads)]
    merged = merged + gated(1, ob_ref[...])
    oc = (jnp.concatenate(oc, axis=1) * szc_ref[...].astype(F32)).astype(BF16)
    merged = merged + gated(2, oc)
    out_ref[...] = x_ref[...] + _dot(merged.astype(BF16), wo_ref[...])


def _merge_call(x2, oa, ob, qc, szc, gates, kc, vc, w_branch, w_out, batch, seq, mem_len):
    rows, d_model = x2.shape
    tm = TM_MERGE
    nt = seq // tm
    row_spec = lambda w: pl.BlockSpec((tm, w), lambda b, t: (b * nt + t, 0))
    mem_spec = pl.BlockSpec((mem_len, WIDTH), lambda b, t: (b, 0))
    return pl.pallas_call(
        functools.partial(_merge_body, d_model=d_model),
        out_shape=jax.ShapeDtypeStruct((rows, d_model), F32),
        grid=(batch, nt),
        in_specs=[row_spec(d_model), row_spec(WIDTH), row_spec(WIDTH), row_spec(WIDTH), row_spec(WIDTH),
                  row_spec(N_BRANCH * d_model), mem_spec, mem_spec,
                  _const_spec((N_BRANCH, WIDTH, d_model)), _const_spec((d_model, d_model))],
        out_specs=row_spec(d_model),
        compiler_params=pltpu.CompilerParams(dimension_semantics=("parallel", "parallel"),
                                             vmem_limit_bytes=VMEM_LIMIT),
        name="merge",
    )(x2, oa, ob, qc, szc, gates, kc, vc, w_branch.astype(BF16), w_out.astype(BF16))


def _layer(x, mem, norm_in, norm_mem, w_in, q_norm_a, k_norm_a, lower_bounds, o_norm_b, w_mem_kv, q_norm_c, k_norm_c,
           w_branch, w_out):
    batch, seq, d_model = x.shape
    mem_len = mem.shape[1]
    x2 = x.reshape(batch * seq, d_model)
    (qa, ka, vat, sza, qi, ki2, wit, sqb, logf, kk, ib, ibt, sgb, qc, szc, gates) = _proj_call(
        x2, norm_in, w_in, lower_bounds, q_norm_a, k_norm_a, q_norm_c)
    kc, vc = _memkv_call(mem.reshape(batch * mem_len, d_model), norm_mem, w_mem_kv, k_norm_c, mem_len)
    oa = _dsa_call(qa, qi, wit, sza, ka, vat, ki2, batch, seq)
    ob = _hgrn_call(sqb, logf, kk, ib, ibt, sgb, o_norm_b, batch, seq)
    out = _merge_call(x2, oa, ob, qc, szc, gates, kc, vc, w_branch, w_out, batch, seq, mem_len)
    return out.reshape(batch, seq, d_model)


def kernel(x, mem, norm_in, norm_mem, w_in, q_norm_a, k_norm_a, lower_bounds, o_norm_b, w_mem_kv, q_norm_c, k_norm_c,
           w_branch, w_out):
    assert norm_in.shape[0] == 1, "single-layer block"
    return _layer(x, mem, norm_in[0], norm_mem[0], w_in[0], q_norm_a[0], k_norm_a[0], lower_bounds, o_norm_b[0],
                  w_mem_kv[0], q_norm_c[0], k_norm_c[0], w_branch[0], w_out[0])
```
